```python
import math
import jax, jax.numpy as jnp
from jax import lax
import numpy as np

D_MODEL = 2048
BATCH = 8
SEQ = 2048
DEPTH = 4

NORM_EPS = 1e-6
GDN_HEADS = 16
GDN_HEAD_DIM = D_MODEL // GDN_HEADS
GDN_CONV = 4
GDN_CHUNK = 64
GDN_WIDTH = GDN_HEADS * GDN_HEAD_DIM
GDN_PROJ = 4 * GDN_WIDTH + 2 * GDN_HEADS
NSA_HEADS = 16
NSA_GROUPS = 4
NSA_HPG = NSA_HEADS // NSA_GROUPS
NSA_HEAD_DIM = D_MODEL // NSA_HEADS
NSA_KV_WIDTH = NSA_GROUPS * NSA_HEAD_DIM
NSA_N_BRANCH = 3
CMP_BLOCK = 32
CMP_STRIDE = 16
CMP_HIDDEN = 512
SEL_BLOCK = 64
SEL_TOPK = 16
WINDOW = 512
NSA_QBLOCK = 64
BIG = 1e9
ROPE_THETA = 500000.0
ROPE_DIM = NSA_HEAD_DIM // 4
FFN_DIM = 5632
N_EXPERTS = 8
TOP_K = 2
EXPERT_DIM = 2048

kernel_name = 'hybrid_gdn_nsa_yoco_moe'


def rmsnorm(x, w):
    xf = x.astype(jnp.float32)
    y = xf * lax.rsqrt(jnp.mean(xf * xf, axis=-1, keepdims=True) + NORM_EPS)
    return (y * w.astype(jnp.float32)).astype(x.dtype)


def l2norm(x):
    xf = x.astype(jnp.float32)
    return xf * lax.rsqrt(jnp.sum(xf * xf, axis=-1, keepdims=True) + NORM_EPS)


def causal_conv(x, w):
    k_width = w.shape[0]
    s = x.shape[1]
    xp = jnp.pad(x, ((0, 0), (k_width - 1, 0), (0, 0)))
    y = xp[:, 0:s] * w[0]
    for j in range(1, k_width):
        y = y + xp[:, j:j + s] * w[j]
    return y


def masked_softmax(s, mask):
    s = jnp.where(mask, s.astype(jnp.float32), -jnp.inf)
    m = jnp.max(s, axis=-1, keepdims=True)
    m = jnp.where(jnp.isfinite(m), m, 0.0)
    p = jnp.where(mask, jnp.exp(s - m), 0.0)
    return p / jnp.maximum(jnp.sum(p, axis=-1, keepdims=True), 1e-30)


def swiglu(h, w_gate, w_up, w_down):
    return (jax.nn.silu(h @ w_gate) * (h @ w_up)) @ w_down


def rope_tables(positions):
    inv = ROPE_THETA ** (-jnp.arange(0, ROPE_DIM, 2, dtype=jnp.float32) / ROPE_DIM)
    ang = positions.astype(jnp.float32)[..., None] * inv
    return jnp.cos(ang)[:, :, None, :], jnp.sin(ang)[:, :, None, :]


def apply_rope(x, cos, sin):
    half = ROPE_DIM // 2
    x1 = x[..., :half].astype(jnp.float32)
    x2 = x[..., half:ROPE_DIM].astype(jnp.float32)
    rot = jnp.concatenate([x1 * cos - x2 * sin, x2 * cos + x1 * sin], axis=-1).astype(x.dtype)
    return jnp.concatenate([rot, x[..., ROPE_DIM:]], axis=-1)


def chunk_gated_delta(q, k, v, g, beta):
    b, s, h, dk = q.shape
    dv = v.shape[-1]
    c = GDN_CHUNK
    n = s // c

    def chunks(t):
        return t.reshape(b, n, c, h, -1).transpose(1, 0, 3, 2, 4)

    q = chunks(q) * (dk ** -0.5)
    k = chunks(k)
    v = chunks(v)
    g = g.reshape(b, n, c, h).transpose(1, 0, 3, 2)
    beta = beta.reshape(b, n, c, h).transpose(1, 0, 3, 2)
    gc = jnp.cumsum(g, axis=-1)
    tril = jnp.tril(jnp.ones((c, c), dtype=bool))
    strict = jnp.tril(jnp.ones((c, c), dtype=bool), -1)
    decay = jnp.exp(jnp.where(tril, gc[..., :, None] - gc[..., None, :], -jnp.inf))
    k_beta = k * beta[..., None]
    v_beta = v * beta[..., None]
    lower = jnp.where(strict, jnp.einsum('nbhid,nbhjd->nbhij', k_beta, k) * decay, 0.0)
    eye = jnp.eye(c, dtype=jnp.float32)
    t_inv = lax.linalg.triangular_solve(eye + lower, jnp.broadcast_to(eye, lower.shape),
                                        left_side=True, lower=True, unit_diagonal=True)
    u = t_inv @ v_beta
    w = t_inv @ (k_beta * jnp.exp(gc)[..., None])
    a_qk = jnp.einsum('nbhid,nbhjd->nbhij', q, k) * decay
    q_dec = q * jnp.exp(gc)[..., None]
    k_dec = k * jnp.exp(gc[..., -1:] - gc)[..., None]
    g_last = jnp.exp(gc[..., -1])

    def step(state, xs):
        u_i, w_i, q_i, k_i, a_i, gl_i = xs
        v_new = u_i - w_i @ state
        o_i = q_i @ state + a_i @ v_new
        state = state * gl_i[..., None, None] + jnp.swapaxes(k_i, -1, -2) @ v_new
        return state, o_i

    state0 = jnp.zeros((b, h, dk, dv), jnp.float32)
    _, o = lax.scan(step, state0, (u, w, q_dec, k_dec, a_qk, g_last))
    return o.transpose(1, 0, 3, 2, 4).reshape(b, s, h, dv)


def gated_deltanet(h, w_in, conv_w, a_log, dt_bias, out_norm, w_out):
    b, s, _ = h.shape
    proj = h @ w_in
    qkv = jax.nn.silu(causal_conv(proj[..., :3 * GDN_WIDTH], conv_w))
    z = proj[..., 3 * GDN_WIDTH:4 * GDN_WIDTH]
    beta_logit = proj[..., 4 * GDN_WIDTH:4 * GDN_WIDTH + GDN_HEADS]
    a = proj[..., 4 * GDN_WIDTH + GDN_HEADS:]
    q, k, v = [t.reshape(b, s, GDN_HEADS, GDN_HEAD_DIM) for t in jnp.split(qkv, 3, axis=-1)]
    q = l2norm(q)
    k = l2norm(k)
    beta = jax.nn.sigmoid(beta_logit.astype(jnp.float32))
    g = -jnp.exp(a_log.astype(jnp.float32)) * jax.nn.softplus(a.astype(jnp.float32) + dt_bias.astype(jnp.float32))
    o = chunk_gated_delta(q, k, v.astype(jnp.float32), g, beta)
    o = rmsnorm(o, out_norm) * jax.nn.silu(z.reshape(b, s, GDN_HEADS, GDN_HEAD_DIM).astype(jnp.float32))
    return o.reshape(b, s, GDN_WIDTH).astype(h.dtype) @ w_out


def nsa_shared_kv(h, positions, kv_norm, w_kv, cmp_pos_k, cmp_w1_k, cmp_w2_k, cmp_pos_v, cmp_w1_v, cmp_w2_v):
    b, s, _ = h.shape
    kv = rmsnorm(h, kv_norm) @ w_kv
    kc, vc, ks, vs, kw, vw = [t.reshape(b, s, NSA_GROUPS, NSA_HEAD_DIM) for t in jnp.split(kv, 6, axis=-1)]
    cos, sin = rope_tables(positions)
    ks = apply_rope(ks, cos, sin)
    kw = apply_rope(kw, cos, sin)
    n_cmp = (s - CMP_BLOCK) // CMP_STRIDE + 1
    blk_idx = CMP_STRIDE * jnp.arange(n_cmp)[:, None] + jnp.arange(CMP_BLOCK)[None, :]

    def compress(t, pos, w1, w2):
        blk = t[:, blk_idx] + pos[:, None, :]
        blk = blk.transpose(0, 3, 1, 2, 4).reshape(b, NSA_GROUPS, n_cmp, CMP_BLOCK * NSA_HEAD_DIM)
        return jax.nn.silu(blk @ w1) @ w2

    k_cmp = compress(kc, cmp_pos_k, cmp_w1_k, cmp_w2_k)
    v_cmp = compress(vc, cmp_pos_v, cmp_w1_v, cmp_w2_v)
    n_sel = s // SEL_BLOCK
    to_bg = lambda t: t.transpose(0, 2, 1, 3)
    k_sel = to_bg(ks).reshape(b, NSA_GROUPS, n_sel, SEL_BLOCK, NSA_HEAD_DIM)
    v_sel = to_bg(vs).reshape(b, NSA_GROUPS, n_sel, SEL_BLOCK, NSA_HEAD_DIM)
    pad = ((0, 0), (0, 0), (WINDOW, 0), (0, 0))
    k_win = jnp.pad(to_bg(kw), pad)
    v_win = jnp.pad(to_bg(vw), pad)
    return k_cmp, v_cmp, k_sel, v_sel, k_win, v_win


def nsa_attention(h, positions, w_q, w_o, shared):
    k_cmp, v_cmp, k_sel, v_sel, k_win, v_win = shared
    b, s, _ = h.shape
    dh = NSA_HEAD_DIM
    qg = h @ w_q
    q = qg[..., :NSA_HEADS * dh].reshape(b, s, NSA_HEADS, dh)
    gates = jax.nn.sigmoid(qg[..., NSA_HEADS * dh:].astype(jnp.float32)).reshape(b, s, NSA_HEADS, NSA_N_BRANCH)
    cos, sin = rope_tables(positions)
    q_rot = apply_rope(q, cos, sin)
    by_group = lambda t: t.reshape(b, s, NSA_GROUPS, NSA_HPG, t.shape[-1]).transpose(0, 2, 3, 1, 4)
    q, q_rot, gates = by_group(q), by_group(q_rot), by_group(gates)
    n_cmp = k_cmp.shape[2]
    n_sel = k_sel.shape[2]
    topk = min(SEL_TOPK, n_sel)
    scale = dh ** -0.5
    c_idx = jnp.arange(n_cmp)
    j_idx = jnp.arange(n_sel)
    cmp_end = c_idx * CMP_STRIDE + CMP_BLOCK - 1
    overlap = jnp.clip(jnp.minimum(c_idx[:, None] * CMP_STRIDE + CMP_BLOCK, (j_idx[None, :] + 1) * SEL_BLOCK)
                       - jnp.maximum(c_idx[:, None] * CMP_STRIDE, j_idx[None, :] * SEL_BLOCK), 0).astype(jnp.float32) / CMP_STRIDE
    bi = jnp.arange(b)[:, None, None, None]
    gi = jnp.arange(NSA_GROUPS)[None, :, None, None]
    r_idx = jnp.arange(SEL_BLOCK)
    qb_len = NSA_QBLOCK

    def block(n):
        t0 = n * qb_len
        t = t0 + jnp.arange(qb_len)
        qb = lax.dynamic_slice_in_dim(q, t0, qb_len, axis=3)
        qr = lax.dynamic_slice_in_dim(q_rot, t0, qb_len, axis=3)
        gb = lax.dynamic_slice_in_dim(gates, t0, qb_len, axis=3)
        s_c = jnp.einsum('bghqd,bgcd->bghqc', qb, k_cmp) * scale
        p_c = masked_softmax(s_c, cmp_end[None, :] <= t[:, None])
        o_c = jnp.einsum('bghqc,bgcd->bghqd', p_c, v_cmp.astype(jnp.float32))
        imp = jnp.einsum('bghqc,cj->bgqj', p_c, overlap)
        cur = t // SEL_BLOCK
        forced = (j_idx[None, :] == 0) | (j_idx[None, :] == cur[:, None]) | (j_idx[None, :] == cur[:, None] - 1)
        causal = j_idx[None, :] * SEL_BLOCK <= t[:, None]
        score = jnp.where(forced, BIG, jnp.where(causal, imp, -BIG))
        _, sel = lax.top_k(score, topk)
        k_g = k_sel[bi, gi, sel]
        v_g = v_sel[bi, gi, sel]
        tok = sel[..., None] * SEL_BLOCK + r_idx
        mask_s = (tok <= t[None, None, :, None, None]).reshape(b, NSA_GROUPS, 1, qb_len, topk * SEL_BLOCK)
        s_s = jnp.einsum('bghqd,bgqnrd->bghqnr', qr, k_g).reshape(b, NSA_GROUPS, NSA_HPG, qb_len, topk * SEL_BLOCK) * scale
        p_s = masked_softmax(s_s, mask_s).reshape(b, NSA_GROUPS, NSA_HPG, qb_len, topk, SEL_BLOCK)
        o_s = jnp.einsum('bghqnr,bgqnrd->bghqd', p_s, v_g.astype(jnp.float32))
        k_w = lax.dynamic_slice_in_dim(k_win, t0, qb_len + WINDOW, axis=2)
        v_w = lax.dynamic_slice_in_dim(v_win, t0, qb_len + WINDOW, axis=2)
        s_pos = t0 - WINDOW + jnp.arange(qb_len + WINDOW)
        mask_w = (s_pos[None, :] <= t[:, None]) & (s_pos[None, :] > t[:, None] - WINDOW) & (s_pos[None, :] >= 0)
        s_w = jnp.einsum('bghqd,bgkd->bghqk', qr, k_w) * scale
        p_w = masked_softmax(s_w, mask_w)
        o_w = jnp.einsum('bghqk,bgkd->bghqd', p_w, v_w.astype(jnp.float32))
        return gb[..., 0:1] * o_c + gb[..., 1:2] * o_s + gb[..., 2:3] * o_w

    o = lax.map(block, jnp.arange(s // qb_len))
    o = o.transpose(1, 0, 4, 2, 3, 5).reshape(b, s, NSA_HEADS * dh).astype(h.dtype)
    return o @ w_o


def moe_ffn(h, router, w_gate, w_up, w_down):
    b, s, d = h.shape
    tok = h.reshape(b * s, d)
    probs = jax.nn.softmax((tok @ router).astype(jnp.float32), axis=-1)
    top_p, top_i = lax.top_k(probs, TOP_K)
    top_p = top_p / jnp.sum(top_p, axis=-1, keepdims=True)
    comb = jnp.sum(jax.nn.one_hot(top_i, N_EXPERTS, dtype=jnp.float32) * top_p[..., None], axis=1)
    out = jnp.zeros((b * s, d), jnp.float32)
    for e in range(N_EXPERTS):
        out = out + comb[:, e:e + 1] * swiglu(tok, w_gate[e], w_up[e], w_down[e]).astype(jnp.float32)
    return out.reshape(b, s, d).astype(h.dtype)


def setup_inputs(seed: int = 0) -> dict:
    key = jax.random.key(seed)
    keys = iter(jax.random.split(key, 32))
    f32 = jnp.float32
    n_a = DEPTH // 2
    n_b = DEPTH - n_a
    n_dense = (DEPTH + 1) // 2
    n_moe = DEPTH // 2

    def nrm(shape, fan_in):
        return jax.random.normal(next(keys), shape, f32) * fan_in ** -0.5

    def gain(shape):
        return 1.0 + 0.02 * jax.random.normal(next(keys), shape, f32)

    x = jax.random.normal(next(keys), (BATCH, SEQ, D_MODEL), f32)
    positions = jnp.arange(SEQ, dtype=jnp.int32)[None, :] + jax.random.randint(next(keys), (BATCH, 1), 0, 4096, dtype=jnp.int32)
    a_log = jnp.log(jax.random.uniform(next(keys), (n_a, GDN_HEADS), f32, 1.0, 16.0))
    dt = jnp.exp(jax.random.uniform(next(keys), (n_a, GDN_HEADS), f32, math.log(1e-3), math.log(1e-1)))
    dt_bias = dt + jnp.log(-jnp.expm1(-dt))
    return {
        'x': x,
        'positions': positions,
        'attn_norm': gain((DEPTH, D_MODEL)),
        'ffn_norm': gain((DEPTH, D_MODEL)),
        'final_norm': gain((D_MODEL,)),
        'gdn_w_in': nrm((n_a, D_MODEL, GDN_PROJ), D_MODEL),
        'gdn_conv': nrm((n_a, GDN_CONV, 3 * GDN_WIDTH), GDN_CONV),
        'gdn_a_log': a_log,
        'gdn_dt_bias': dt_bias,
        'gdn_out_norm': gain((n_a, GDN_HEAD_DIM)),
        'gdn_w_out': nrm((n_a, GDN_WIDTH, D_MODEL), GDN_WIDTH),
        'kv_norm': gain((D_MODEL,)),
        'nsa_w_kv': nrm((D_MODEL, 6 * NSA_KV_WIDTH), D_MODEL),
        'cmp_pos_k': 0.02 * jax.random.normal(next(keys), (CMP_BLOCK, NSA_HEAD_DIM), f32),
        'cmp_w1_k': nrm((CMP_BLOCK * NSA_HEAD_DIM, CMP_HIDDEN), CMP_BLOCK * NSA_HEAD_DIM),
        'cmp_w2_k': nrm((CMP_HIDDEN, NSA_HEAD_DIM), CMP_HIDDEN),
        'cmp_pos_v': 0.02 * jax.random.normal(next(keys), (CMP_BLOCK, NSA_HEAD_DIM), f32),
        'cmp_w1_v': nrm((CMP_BLOCK * NSA_HEAD_DIM, CMP_HIDDEN), CMP_BLOCK * NSA_HEAD_DIM),
        'cmp_w2_v': nrm((CMP_HIDDEN, NSA_HEAD_DIM), CMP_HIDDEN),
        'nsa_w_q': nrm((n_b, D_MODEL, NSA_HEADS * NSA_HEAD_DIM + NSA_HEADS * NSA_N_BRANCH), D_MODEL),
        'nsa_w_o': nrm((n_b, NSA_HEADS * NSA_HEAD_DIM, D_MODEL), NSA_HEADS * NSA_HEAD_DIM),
        'ffn_w_gate': nrm((n_dense, D_MODEL, FFN_DIM), D_MODEL),
        'ffn_w_up': nrm((n_dense, D_MODEL, FFN_DIM), D_MODEL),
        'ffn_w_down': nrm((n_dense, FFN_DIM, D_MODEL), FFN_DIM),
        'moe_router': nrm((n_moe, D_MODEL, N_EXPERTS), D_MODEL),
        'moe_w_gate': nrm((n_moe, N_EXPERTS, D_MODEL, EXPERT_DIM), D_MODEL),
        'moe_w_up': nrm((n_moe, N_EXPERTS, D_MODEL, EXPERT_DIM), D_MODEL),
        'moe_w_down': nrm((n_moe, N_EXPERTS, EXPERT_DIM, D_MODEL), EXPERT_DIM),
    }


def reference(x, positions, attn_norm, ffn_norm, final_norm, gdn_w_in, gdn_conv, gdn_a_log, gdn_dt_bias,
              gdn_out_norm, gdn_w_out, kv_norm, nsa_w_kv, cmp_pos_k, cmp_w1_k, cmp_w2_k, cmp_pos_v, cmp_w1_v,
              cmp_w2_v, nsa_w_q, nsa_w_o, ffn_w_gate, ffn_w_up, ffn_w_down, moe_router, moe_w_gate, moe_w_up,
              moe_w_down):
    n_a = DEPTH // 2
    h = x
    shared = None
    for l in range(DEPTH):
        hn = rmsnorm(h, attn_norm[l])
        if l < n_a:
            h = h + gated_deltanet(hn, gdn_w_in[l], gdn_conv[l], gdn_a_log[l], gdn_dt_bias[l], gdn_out_norm[l], gdn_w_out[l])
        else:
            h = h + nsa_attention(hn, positions, nsa_w_q[l - n_a], nsa_w_o[l - n_a], shared)
        hn = rmsnorm(h, ffn_norm[l])
        if l % 2 == 0:
            h = h + swiglu(hn, ffn_w_gate[l // 2], ffn_w_up[l // 2], ffn_w_down[l // 2])
        else:
            h = h + moe_ffn(hn, moe_router[l // 2], moe_w_gate[l // 2], moe_w_up[l // 2], moe_w_down[l // 2])
        if l == n_a - 1:
            shared = nsa_shared_kv(h, positions, kv_norm, nsa_w_kv, cmp_pos_k, cmp_w1_k, cmp_w2_k,
                                   cmp_pos_v, cmp_w1_v, cmp_w2_v)
    return rmsnorm(h, final_norm)
```

```python
import functools
import math

import jax
import jax.numpy as jnp
from jax import lax
from jax.experimental import pallas as pl
from jax.experimental.pallas import tpu as pltpu

F32 = jnp.float32
BF16 = jnp.bfloat16

NORM_EPS = 1e-6
LANES = 128
GDN_HEADS = 16
GDN_CONV = 4
GDN_CHUNK = 64
NSA_HEADS = 16
NSA_GROUPS = 4
NSA_HPG = NSA_HEADS // NSA_GROUPS
NSA_N_BRANCH = 3
CMP_BLOCK = 32
CMP_STRIDE = 16
SEL_BLOCK = 64
SEL_TOPK = 16
WINDOW = 512
BIG = 1e9
NEG = -1e30
ROPE_THETA = 500000.0
ROPE_DIM = 32
TOP_K = 2

VMEM_LIMIT_BYTES = 56 * 1024 * 1024


def _params(n_axes):
    return pltpu.CompilerParams(dimension_semantics=("arbitrary",) * n_axes,
                                vmem_limit_bytes=VMEM_LIMIT_BYTES)


def _rms(x, gain):
    ms = jnp.mean(x * x, axis=-1, keepdims=True)
    return x * lax.rsqrt(ms + NORM_EPS) * gain


def _sigmoid(x):
    return 1.0 / (1.0 + jnp.exp(-x))


def _dot(a, b):
    return jnp.dot(a, b, preferred_element_type=F32)


def _dot_nt(a, b):
    return lax.dot_general(a, b, (((1,), (1,)), ((), ())), preferred_element_type=F32)


def _dot_tn(a, b):
    return lax.dot_general(a, b, (((0,), (0,)), ((), ())), preferred_element_type=F32)


def _pick_lane(x, idx):
    lane = lax.broadcasted_iota(jnp.int32, x.shape, 1)
    return jnp.sum(jnp.where(lane == idx, x, 0.0), axis=-1, keepdims=True)


def _rms_matmul_kernel(h_ref, g_ref, w_ref, we_ref, o_ref, oe_ref, xn_ref, *, row_chunk):
    j = pl.program_id(1)

    @pl.when(j == 0)
    def _():
        for r in range(0, h_ref.shape[0], row_chunk):
            xn_ref[r:r + row_chunk, :] = _rms(h_ref[r:r + row_chunk, :], g_ref[...]).astype(BF16)
        oe_ref[...] = _dot(xn_ref[...], we_ref[...])

    o_ref[...] = _dot(xn_ref[...], w_ref[...]).astype(o_ref.dtype)


def rms_matmul(h, gain, w, w_extra, *, tm=1024, tn=1024, out_dtype=F32):
    n, d = h.shape
    nm = w.shape[1]
    tm = min(tm, n)
    tn = min(tn, nm)
    return pl.pallas_call(
        functools.partial(_rms_matmul_kernel, row_chunk=256),
        grid=(n // tm, nm // tn),
        in_specs=[
            pl.BlockSpec((tm, d), lambda i, j: (i, 0)),
            pl.BlockSpec((1, d), lambda i, j: (0, 0)),
            pl.BlockSpec((d, tn), lambda i, j: (0, j)),
            pl.BlockSpec((d, LANES), lambda i, j: (0, 0)),
        ],
        out_specs=[
            pl.BlockSpec((tm, tn), lambda i, j: (i, j)),
            pl.BlockSpec((tm, LANES), lambda i, j: (i, 0)),
        ],
        out_shape=[jax.ShapeDtypeStruct((n, nm), out_dtype),
                   jax.ShapeDtypeStruct((n, LANES), F32)],
        scratch_shapes=[pltpu.VMEM((tm, d), BF16)],
        compiler_params=_params(2),
    )(h, gain.reshape(1, d), w, w_extra)


def _matmul_res_kernel(x_ref, w_ref, r_ref, o_ref):
    o_ref[...] = r_ref[...] + _dot(x_ref[...], w_ref[...])


def matmul_residual(x, w, res, *, tm=1024, tn=512):
    n, k = x.shape
    nm = w.shape[1]
    tm = min(tm, n)
    return pl.pallas_call(
        _matmul_res_kernel,
        grid=(n // tm, nm // tn),
        in_specs=[
            pl.BlockSpec((tm, k), lambda i, j: (i, 0)),
            pl.BlockSpec((k, tn), lambda i, j: (0, j)),
            pl.BlockSpec((tm, tn), lambda i, j: (i, j)),
        ],
        out_specs=pl.BlockSpec((tm, tn), lambda i, j: (i, j)),
        out_shape=jax.ShapeDtypeStruct((n, nm), F32),
        compiler_params=_params(2),
    )(x, w, res)


def _router_kernel(h_ref, g_ref, r_ref, o_ref, *, n_experts, row_chunk):
    for r in range(0, h_ref.shape[0], row_chunk):
        xn = _rms(h_ref[r:r + row_chunk, :], g_ref[...])
        logits = jnp.dot(xn, r_ref[...], preferred_element_type=F32,
                         precision=lax.Precision.HIGHEST)
        lane = lax.broadcasted_iota(jnp.int32, logits.shape, 1)
        valid = lane < n_experts
        z = jnp.where(valid, logits, NEG)
        z = z - jnp.max(z, axis=-1, keepdims=True)
        p = jnp.where(valid, jnp.exp(z), 0.0)
        p = p / jnp.sum(p, axis=-1, keepdims=True)
        p = jnp.where(valid, p, -1.0)
        p1 = jnp.max(p, axis=-1, keepdims=True)
        i1 = jnp.min(jnp.where(p == p1, lane, LANES), axis=-1, keepdims=True)
        pm = jnp.where(lane == i1, -1.0, p)
        p2 = jnp.max(pm, axis=-1, keepdims=True)
        i2 = jnp.min(jnp.where(pm == p2, lane, LANES), axis=-1, keepdims=True)
        den = p1 + p2
        o_ref[r:r + row_chunk, :] = (jnp.where(lane == i1, p1 / den, 0.0)
                                     + jnp.where(lane == i2, p2 / den, 0.0))


def route_tokens(h, gain, router, *, tm=1024):
    n, d = h.shape
    n_experts = router.shape[1]
    tm = min(tm, n)
    r_pad = jnp.pad(router, ((0, 0), (0, LANES - n_experts)))
    return pl.pallas_call(
        functools.partial(_router_kernel, n_experts=n_experts, row_chunk=256),
        grid=(n // tm,),
        in_specs=[
            pl.BlockSpec((tm, d), lambda i: (i, 0)),
            pl.BlockSpec((1, d), lambda i: (0, 0)),
            pl.BlockSpec((d, LANES), lambda i: (0, 0)),
        ],
        out_specs=pl.BlockSpec((tm, LANES), lambda i: (i, 0)),
        out_shape=jax.ShapeDtypeStruct((n, LANES), F32),
        compiler_params=_params(1),
    )(h, gain.reshape(1, d), r_pad)


def _ffn_kernel(h_ref, g_ref, c_ref, wg_ref, wu_ref, wd_ref, fg_ref, o_ref, xn_ref, *,
                n_e, n_f, use_comb, final_norm, row_chunk):
    e = pl.program_id(1)
    f = pl.program_id(2)

    @pl.when((e == 0) & (f == 0))
    def _():
        for r in range(0, h_ref.shape[0], row_chunk):
            x = h_ref[r:r + row_chunk, :]
            xn_ref[r:r + row_chunk, :] = _rms(x, g_ref[...]).astype(BF16)
            o_ref[r:r + row_chunk, :] = x

    xn = xn_ref[...]
    gate = _dot(xn, wg_ref[0])
    up = _dot(xn, wu_ref[0])
    act = gate * _sigmoid(gate) * up
    if use_comb:
        act = act * _pick_lane(c_ref[...], e)
    o_ref[...] += _dot(act.astype(BF16), wd_ref[0])

    if final_norm:
        @pl.when((e == n_e - 1) & (f == n_f - 1))
        def _():
            for r in range(0, h_ref.shape[0], row_chunk):
                o_ref[r:r + row_chunk, :] = _rms(o_ref[r:r + row_chunk, :], fg_ref[...])


def ffn_residual(h, gain, comb, w_gate, w_up, w_down, final_gain, *, tm=512, tf=512):
    n, d = h.shape
    n_e, _, ff = w_gate.shape
    tm = min(tm, n)
    use_comb = comb is not None
    final_norm = final_gain is not None
    if comb is None:
        comb = jnp.ones((n, LANES), F32)
    if final_gain is None:
        final_gain = jnp.ones((d,), F32)
    n_f = ff // tf
    return pl.pallas_call(
        functools.partial(_ffn_kernel, n_e=n_e, n_f=n_f, use_comb=use_comb,
                          final_norm=final_norm, row_chunk=256),
        grid=(n // tm, n_e, n_f),
        in_specs=[
            pl.BlockSpec((tm, d), lambda i, e, f: (i, 0)),
            pl.BlockSpec((1, d), lambda i, e, f: (0, 0)),
            pl.BlockSpec((tm, LANES), lambda i, e, f: (i, 0)),
            pl.BlockSpec((1, d, tf), lambda i, e, f: (e, 0, f)),
            pl.BlockSpec((1, d, tf), lambda i, e, f: (e, 0, f)),
            pl.BlockSpec((1, tf, d), lambda i, e, f: (e, f, 0)),
            pl.BlockSpec((1, d), lambda i, e, f: (0, 0)),
        ],
        out_specs=pl.BlockSpec((tm, d), lambda i, e, f: (i, 0)),
        out_shape=jax.ShapeDtypeStruct((n, d), F32),
        scratch_shapes=[pltpu.VMEM((tm, d), BF16)],
        compiler_params=_params(3),
    )(h, gain.reshape(1, d), comb, w_gate, w_up, w_down, final_gain.reshape(1, d))


def _tri_inv(low):
    c = low.shape[0]
    ii = lax.broadcasted_iota(jnp.int32, (c, c), 0)
    jj = lax.broadcasted_iota(jnp.int32, (c, c), 1)
    t = jnp.where(ii == jj, 1.0, 0.0) - low
    p = low
    for _ in range(int(math.log2(c)) - 1):
        pb = p.astype(BF16)
        p = _dot(pb, pb)
        t = t + _dot(t.astype(BF16), p.astype(BF16))
    return t


def _gdn_kernel(alog_ref, dtb_ref, q_ref, k_ref, v_ref, z_ref, cq_ref, ck_ref, cv_ref,
                a1_ref, a2_ref, b2_ref, on_ref, o_ref,
                xp_ref, qs_ref, ks_ref, vs_ref, os_ref, gc1_ref, gc2_ref, be2_ref, *,
                seq, row_chunk):
    hd = pl.program_id(1)
    dk = q_ref.shape[1]
    c = GDN_CHUNK
    n_chunks = seq // c
    pad = 8

    def conv_silu(x_ref, w_ref, dst_ref, normalise, scale):
        xp_ref[0:pad, :] = jnp.zeros((pad, dk), F32)
        xp_ref[pad:pad + seq, :] = x_ref[...]
        w = w_ref[...]
        for r in range(0, seq, row_chunk):
            y = xp_ref[pad - 3 + r:pad - 3 + r + row_chunk, :] * w[0:1, :]
            for j in range(1, GDN_CONV):
                y = y + xp_ref[pad - 3 + j + r:pad - 3 + j + r + row_chunk, :] * w[j:j + 1, :]
            y = y * _sigmoid(y)
            if normalise:
                y = y * lax.rsqrt(jnp.sum(y * y, axis=-1, keepdims=True) + NORM_EPS)
            if scale != 1.0:
                y = y * scale
            dst_ref[r:r + row_chunk, :] = y

    conv_silu(q_ref, cq_ref, qs_ref, True, dk ** -0.5)
    conv_silu(k_ref, ck_ref, ks_ref, True, 1.0)
    conv_silu(v_ref, cv_ref, vs_ref, False, 1.0)

    a_coef = -jnp.exp(alog_ref[hd])
    dtb = dtb_ref[hd]

    def log_decay(a):
        x = a + dtb
        return a_coef * (jnp.maximum(x, 0.0) + jnp.log(1.0 + jnp.exp(-jnp.abs(x))))

    ii = lax.broadcasted_iota(jnp.int32, (c, c), 0)
    jj = lax.broadcasted_iota(jnp.int32, (c, c), 1)
    upper = jnp.where(ii <= jj, 1.0, 0.0)
    lower = jnp.where(jj <= ii, 1.0, 0.0)
    gc1_ref[...] = jnp.dot(log_decay(a1_ref[0, 0]), upper, preferred_element_type=F32,
                           precision=lax.Precision.HIGHEST)
    gc2_ref[...] = jnp.dot(lower, log_decay(a2_ref[0, 0]), preferred_element_type=F32,
                           precision=lax.Precision.HIGHEST)
    be2_ref[...] = _sigmoid(b2_ref[0, 0])

    def step(ci, state):
        r0 = pl.multiple_of(ci * c, c)
        q = qs_ref[pl.ds(r0, c), :]
        k = ks_ref[pl.ds(r0, c), :]
        v = vs_ref[pl.ds(r0, c), :]
        g_row = gc1_ref[pl.ds(ci, 1), :]
        g_col = _pick_lane(gc2_ref[...], ci)
        b_col = _pick_lane(be2_ref[...], ci)
        g_last = g_row[:, c - 1:c]
        causal = ii >= jj
        decay = jnp.where(causal, jnp.exp(jnp.where(causal, g_col - g_row, 0.0)), 0.0)
        kb = k * b_col
        kbf = k.astype(BF16)
        low = jnp.where(ii > jj, _dot_nt(kb.astype(BF16), kbf) * decay, 0.0)
        t_inv = _tri_inv(low)
        e_col = jnp.exp(g_col)
        rhs = jnp.concatenate([v * b_col, kb * e_col], axis=1).astype(BF16)
        uw = _dot(t_inv.astype(BF16), rhs)
        u = uw[:, :dk]
        w = uw[:, dk:]
        a_qk = _dot_nt(q.astype(BF16), kbf) * decay
        q_dec = q * e_col
        k_dec = k * jnp.exp(g_last - g_col)
        sb = state.astype(BF16)
        ws_qs = _dot(jnp.concatenate([w, q_dec], axis=0).astype(BF16), sb)
        v_new = u - ws_qs[:c]
        vnb = v_new.astype(BF16)
        os_ref[pl.ds(r0, c), :] = ws_qs[c:] + _dot(a_qk.astype(BF16), vnb)
        return state * jnp.exp(g_last) + _dot_tn(k_dec.astype(BF16), vnb)

    lax.fori_loop(0, n_chunks, step, jnp.zeros((dk, dk), F32))

    for r in range(0, seq, row_chunk):
        z = z_ref[r:r + row_chunk, :]
        o = _rms(os_ref[r:r + row_chunk, :], on_ref[...])
        o_ref[r:r + row_chunk, :] = (o * (z * _sigmoid(z))).astype(o_ref.dtype)


def gdn_core(proj, gates, conv_w, a_log, dt_bias, out_norm, *, batch, seq):
    n = proj.shape[0]
    nh = GDN_HEADS
    dk = proj.shape[1] // (4 * nh)
    c = GDN_CHUNK
    n_chunks = seq // c
    bl = gates[:, :nh].reshape(batch, seq, nh).transpose(0, 2, 1)
    al = gates[:, nh:2 * nh].reshape(batch, seq, nh).transpose(0, 2, 1)
    a1 = al.reshape(batch, nh, n_chunks, c)
    a2 = a1.transpose(0, 1, 3, 2)
    b2 = bl.reshape(batch, nh, n_chunks, c).transpose(0, 1, 3, 2)
    smem = pl.BlockSpec(memory_space=pltpu.SMEM)
    col = lambda off: pl.BlockSpec((seq, dk), lambda b, h, off=off: (b, off * nh + h))
    cw = lambda off: pl.BlockSpec((GDN_CONV, dk), lambda b, h, off=off: (0, off * nh + h))
    g1 = pl.BlockSpec((1, 1, n_chunks, c), lambda b, h: (b, h, 0, 0))
    g2 = pl.BlockSpec((1, 1, c, n_chunks), lambda b, h: (b, h, 0, 0))
    return pl.pallas_call(
        functools.partial(_gdn_kernel, seq=seq, row_chunk=256),
        grid=(batch, nh),
        in_specs=[smem, smem, col(0), col(1), col(2), col(3), cw(0), cw(1), cw(2),
                  g1, g2, g2, pl.BlockSpec((1, dk), lambda b, h: (0, 0))],
        out_specs=pl.BlockSpec((seq, dk), lambda b, h: (b, h)),
        out_shape=jax.ShapeDtypeStruct((n, nh * dk), BF16),
        scratch_shapes=[
            pltpu.VMEM((seq + 8, dk), F32),
            pltpu.VMEM((seq, dk), F32), pltpu.VMEM((seq, dk), F32), pltpu.VMEM((seq, dk), F32),
            pltpu.VMEM((seq, dk), F32),
            pltpu.VMEM((n_chunks, c), F32), pltpu.VMEM((c, n_chunks), F32),
            pltpu.VMEM((c, n_chunks), F32),
        ],
        compiler_params=_params(2),
    )(a_log, dt_bias, proj, proj, proj, proj, conv_w, conv_w, conv_w, a1, a2, b2,
      out_norm.reshape(1, dk))


def _rope(x, cos, sin_a, sin_b):
    half = ROPE_DIM // 2
    return (x * cos + pltpu.roll(x, LANES - half, 1) * sin_a + pltpu.roll(x, half, 1) * sin_b)


def _kv_rope_kernel(x_ref, cos_ref, sa_ref, sb_ref, o_ref):
    j = pl.program_id(1)
    tensor = j // NSA_GROUPS
    x = x_ref[...]
    roped = _rope(x, cos_ref[...], sa_ref[...], sb_ref[...])
    o_ref[...] = jnp.where((tensor == 2) | (tensor == 4), roped, x).astype(o_ref.dtype)


def kv_rope(kv, cos, sin_a, sin_b, *, tm=1024):
    n, width = kv.shape
    tm = min(tm, n)
    tab = pl.BlockSpec((tm, LANES), lambda i, j: (i, 0))
    return pl.pallas_call(
        _kv_rope_kernel,
        grid=(n // tm, width // LANES),
        in_specs=[pl.BlockSpec((tm, LANES), lambda i, j: (i, j)), tab, tab, tab],
        out_specs=pl.BlockSpec((tm, LANES), lambda i, j: (i, j)),
        out_shape=jax.ShapeDtypeStruct((n, width), BF16),
        compiler_params=_params(2),
    )(kv, cos, sin_a, sin_b)


def _compress_kernel(kc_ref, vc_ref, pk_ref, pv_ref, w1k_ref, w1v_ref, w2k_ref, w2v_ref,
                     ok_ref, ov_ref, tp_ref, *, seq):
    dh = kc_ref.shape[1]
    n_rows = ok_ref.shape[2]
    hidden = w1k_ref.shape[1]

    def run(t_ref, pos_ref, w1_ref, w2_ref, out_ref):
        tp_ref[0:seq, :] = t_ref[...]
        tp_ref[seq:seq + CMP_STRIDE, :] = jnp.zeros((CMP_STRIDE, dh), F32)
        acc = jnp.zeros((n_rows, hidden), F32)
        for l in range(CMP_BLOCK):
            x = tp_ref[pl.ds(l, n_rows, stride=CMP_STRIDE), :] + pos_ref[l:l + 1, :]
            acc = acc + _dot(x.astype(BF16), w1_ref[l * dh:(l + 1) * dh, :])
        hid = acc * _sigmoid(acc)
        out_ref[0, 0] = _dot(hid.astype(BF16), w2_ref[...]).astype(out_ref.dtype)

    run(kc_ref, pk_ref, w1k_ref, w2k_ref, ok_ref)
    run(vc_ref, pv_ref, w1v_ref, w2v_ref, ov_ref)


def nsa_compress(kv, pos_k, pos_v, w1k, w1v, w2k, w2v, *, batch, seq):
    dh = pos_k.shape[1]
    n_rows = seq // CMP_STRIDE
    hidden = w1k.shape[1]
    full = lambda shape: pl.BlockSpec(shape, lambda b, g: (0,) * len(shape))
    out = pl.BlockSpec((1, 1, n_rows, dh), lambda b, g: (b, g, 0, 0))
    return pl.pallas_call(
        functools.partial(_compress_kernel, seq=seq),
        grid=(batch, NSA_GROUPS),
        in_specs=[
            pl.BlockSpec((seq, dh), lambda b, g: (b, g)),
            pl.BlockSpec((seq, dh), lambda b, g: (b, NSA_GROUPS + g)),
            full((CMP_BLOCK, dh)), full((CMP_BLOCK, dh)),
            full((CMP_BLOCK * dh, hidden)), full((CMP_BLOCK * dh, hidden)),
            full((hidden, dh)), full((hidden, dh)),
        ],
        out_specs=[out, out],
        out_shape=[jax.ShapeDtypeStruct((batch, NSA_GROUPS, n_rows, dh), BF16)] * 2,
        scratch_shapes=[pltpu.VMEM((seq + CMP_STRIDE, dh), F32)],
        compiler_params=_params(2),
    )(kv, kv, pos_k, pos_v, w1k, w1v, w2k, w2v)


def _masked_softmax(s, mask):
    sm = jnp.where(mask, s, NEG)
    m = jnp.max(sm, axis=-1, keepdims=True)
    p = jnp.where(mask, jnp.exp(sm - m), 0.0)
    return p / jnp.maximum(jnp.sum(p, axis=-1, keepdims=True), 1e-30)


def _nsa_kernel(q_ref, gate_ref, cos_ref, sa_ref, sb_ref, kc_ref, vc_ref, ks_ref, vs_ref,
                kw_ref, vw_ref, ov_ref, o_ref, *, seq, tq, kc_len):
    g = pl.program_id(1)
    n = pl.program_id(2)
    t0 = n * tq
    dh = LANES
    hpg = NSA_HPG
    m_rows = hpg * tq
    n_sel = seq // SEL_BLOCK
    topk = min(SEL_TOPK, n_sel)
    scale = dh ** -0.5

    stack = lambda x: jnp.concatenate([x] * hpg, axis=0)
    q = q_ref[...]
    qs = jnp.concatenate([q[:, h * dh:(h + 1) * dh] for h in range(hpg)], axis=0) * scale
    qr = _rope(qs, stack(cos_ref[...]), stack(sa_ref[...]), stack(sb_ref[...]))
    qb = qs.astype(BF16)
    qrb = qr.astype(BF16)
    row = lax.broadcasted_iota(jnp.int32, (m_rows, 1), 0)
    tq_col = t0 + (row & (tq - 1))

    s_c = _dot_nt(qb, kc_ref[0, 0])
    lane_c = lax.broadcasted_iota(jnp.int32, s_c.shape, 1)
    p_c = _masked_softmax(s_c, lane_c * CMP_STRIDE + (CMP_BLOCK - 1) <= tq_col)
    o_c = _dot(p_c.astype(BF16), vc_ref[0, 0])

    p_sum = p_c[0:tq]
    for h in range(1, hpg):
        p_sum = p_sum + p_c[h * tq:(h + 1) * tq]
    imp = jnp.dot(p_sum, ov_ref[...], preferred_element_type=F32,
                  precision=lax.Precision.HIGHEST)
    lane = lax.broadcasted_iota(jnp.int32, (tq, LANES), 1)
    t1 = t0 + lax.broadcasted_iota(jnp.int32, (tq, 1), 0)
    cur = t1 // SEL_BLOCK
    forced = (lane == 0) | (lane == cur) | (lane == cur - 1)
    score = jnp.where(forced, BIG, jnp.where(lane * SEL_BLOCK <= t1, imp, -BIG))
    score = jnp.where(lane < n_sel, score, -3e38)
    rank = jnp.zeros((tq, LANES), F32)
    for j2 in range(n_sel):
        col = score[:, j2:j2 + 1]
        beats = jnp.where(col > score, 1.0, jnp.where((col == score) & (lane > j2), 1.0, 0.0))
        rank = rank + beats
    sel = jnp.where((rank < topk) & (lane < n_sel), 1.0, 0.0)
    sel4 = stack(sel).astype(BF16)

    def sel_step(ci, carry):
        m_i, l_i, acc = carry
        k0 = pl.multiple_of(ci * kc_len, kc_len)
        s = _dot_nt(qrb, ks_ref[pl.ds(k0, kc_len), :])
        blk = lax.broadcasted_iota(jnp.int32, (LANES, kc_len), 0)
        pos = k0 + lax.broadcasted_iota(jnp.int32, (LANES, kc_len), 1)
        expand = jnp.where(blk == pos // SEL_BLOCK, 1.0, 0.0).astype(BF16)
        picked = _dot(sel4, expand)
        kpos = k0 + lax.broadcasted_iota(jnp.int32, (m_rows, kc_len), 1)
        mask = (picked > 0.5) & (kpos <= tq_col)
        sm = jnp.where(mask, s, NEG)
        m_new = jnp.maximum(m_i, jnp.max(sm, axis=-1, keepdims=True))
        alpha = jnp.exp(m_i - m_new)
        p = jnp.where(mask, jnp.exp(sm - m_new), 0.0)
        l_new = alpha * l_i + jnp.sum(p, axis=-1, keepdims=True)
        acc_new = alpha * acc + _dot(p.astype(BF16), vs_ref[pl.ds(k0, kc_len), :])
        return m_new, l_new, acc_new

    n_kc = (t0 + tq + kc_len - 1) // kc_len
    init = (jnp.full((m_rows, 1), NEG, F32), jnp.zeros((m_rows, 1), F32),
            jnp.zeros((m_rows, dh), F32))
    _, l_s, acc_s = lax.fori_loop(0, n_kc, sel_step, init)
    o_s = acc_s / jnp.maximum(l_s, 1e-30)

    wk = WINDOW + tq
    start = pl.multiple_of(jnp.maximum(t0 - WINDOW, 0), LANES)
    s_w = _dot_nt(qrb, kw_ref[pl.ds(start, wk), :])
    kpos_w = start + lax.broadcasted_iota(jnp.int32, (m_rows, wk), 1)
    p_w = _masked_softmax(s_w, (kpos_w <= tq_col) & (kpos_w > tq_col - WINDOW))
    o_w = _dot(p_w.astype(BF16), vw_ref[pl.ds(start, wk), :])

    gates = _sigmoid(gate_ref[...])
    for h in range(hpg):
        base = (g * hpg + h) * NSA_N_BRANCH
        rows = slice(h * tq, (h + 1) * tq)
        out = (_pick_lane(gates, base) * o_c[rows] + _pick_lane(gates, base + 1) * o_s[rows]
               + _pick_lane(gates, base + 2) * o_w[rows])
        o_ref[:, h * dh:(h + 1) * dh] = out.astype(o_ref.dtype)


def nsa_attention(q, gates, cos, sin_a, sin_b, k_cmp, v_cmp, kvr, overlap, *, batch, seq,
                  tq=128, kc_len=512):
    n = q.shape[0]
    dh = LANES
    gw = NSA_HPG * dh
    nq = seq // tq
    kc_len = min(kc_len, seq)
    rows = lambda width: pl.BlockSpec((tq, width), lambda b, g, i: (b * nq + i, 0))
    cmp_spec = pl.BlockSpec((1, 1, k_cmp.shape[2], dh), lambda b, g, i: (b, g, 0, 0))
    kv_spec = lambda t: pl.BlockSpec((seq, dh), lambda b, g, i, t=t: (b, t * NSA_GROUPS + g))
    return pl.pallas_call(
        functools.partial(_nsa_kernel, seq=seq, tq=tq, kc_len=kc_len),
        grid=(batch, NSA_GROUPS, nq),
        in_specs=[
            pl.BlockSpec((tq, gw), lambda b, g, i: (b * nq + i, g)),
            rows(LANES), rows(LANES), rows(LANES), rows(LANES),
            cmp_spec, cmp_spec, kv_spec(2), kv_spec(3), kv_spec(4), kv_spec(5),
            pl.BlockSpec((LANES, LANES), lambda b, g, i: (0, 0)),
        ],
        out_specs=pl.BlockSpec((tq, gw), lambda b, g, i: (b * nq + i, g)),
        out_shape=jax.ShapeDtypeStruct((n, NSA_HEADS * dh), BF16),
        compiler_params=_params(3),
    )(q, gates, cos, sin_a, sin_b, k_cmp, v_cmp, kvr, kvr, kvr, kvr, overlap)


def _rope_tables(positions):
    half = ROPE_DIM // 2
    inv = ROPE_THETA ** (-jnp.arange(0, ROPE_DIM, 2, dtype=F32) / ROPE_DIM)
    ang = positions.astype(F32).reshape(-1, 1) * inv
    cos, sin = jnp.cos(ang), jnp.sin(ang)
    n = ang.shape[0]
    ones = jnp.ones((n, LANES - ROPE_DIM), F32)
    zeros_h = jnp.zeros((n, half), F32)
    zeros_t = jnp.zeros((n, LANES - ROPE_DIM), F32)
    cos_t = jnp.concatenate([cos, cos, ones], axis=1)
    sin_a = jnp.concatenate([-sin, zeros_h, zeros_t], axis=1)
    sin_b = jnp.concatenate([zeros_h, sin, zeros_t], axis=1)
    return cos_t, sin_a, sin_b


def _overlap_table(seq):
    n_cmp = (seq - CMP_BLOCK) // CMP_STRIDE + 1
    n_sel = seq // SEL_BLOCK
    c_idx = jnp.arange(n_cmp)
    j_idx = jnp.arange(n_sel)
    ov = jnp.clip(jnp.minimum(c_idx[:, None] * CMP_STRIDE + CMP_BLOCK, (j_idx[None, :] + 1) * SEL_BLOCK)
                  - jnp.maximum(c_idx[:, None] * CMP_STRIDE, j_idx[None, :] * SEL_BLOCK), 0)
    ov = ov.astype(F32) / CMP_STRIDE
    return jnp.pad(ov, ((0, LANES - n_cmp), (0, LANES - n_sel)))


def _split_pad(w, main):
    extra = w[:, main:]
    return w[:, :main].astype(BF16), jnp.pad(extra, ((0, 0), (0, LANES - extra.shape[1]))).astype(BF16)


def kernel(x, positions, attn_norm, ffn_norm, final_norm, gdn_w_in, gdn_conv, gdn_a_log, gdn_dt_bias,
           gdn_out_norm, gdn_w_out, kv_norm, nsa_w_kv, cmp_pos_k, cmp_w1_k, cmp_w2_k, cmp_pos_v, cmp_w1_v,
           cmp_w2_v, nsa_w_q, nsa_w_o, ffn_w_gate, ffn_w_up, ffn_w_down, moe_router, moe_w_gate, moe_w_up,
           moe_w_down):
    batch, seq, d = x.shape
    depth = attn_norm.shape[0]
    n_a = depth // 2
    h = x.reshape(batch * seq, d)
    gdn_width = gdn_w_out.shape[1]
    nsa_width = nsa_w_o.shape[1]
    shared = None
    for l in range(depth):
        if l < n_a:
            w_main, w_gates = _split_pad(gdn_w_in[l], 4 * gdn_width)
            proj, gates = rms_matmul(h, attn_norm[l], w_main, w_gates)
            o = gdn_core(proj, gates, gdn_conv[l], gdn_a_log[l], gdn_dt_bias[l], gdn_out_norm[l],
                         batch=batch, seq=seq)
            h = matmul_residual(o, gdn_w_out[l].astype(BF16), h)
        else:
            cos, sin_a, sin_b, k_cmp, v_cmp, kvr, overlap = shared
            w_main, w_gates = _split_pad(nsa_w_q[l - n_a], nsa_width)
            q, gates = rms_matmul(h, attn_norm[l], w_main, w_gates)
            o = nsa_attention(q, gates, cos, sin_a, sin_b, k_cmp, v_cmp, kvr, overlap,
                              batch=batch, seq=seq)
            h = matmul_residual(o, nsa_w_o[l - n_a].astype(BF16), h)
        final_gain = final_norm if l == depth - 1 else None
        if l % 2 == 0:
            i = l // 2
            h = ffn_residual(h, ffn_norm[l], None, ffn_w_gate[i][None].astype(BF16),
                             ffn_w_up[i][None].astype(BF16), ffn_w_down[i][None].astype(BF16), final_gain)
        else:
            i = l // 2
            comb = route_tokens(h, ffn_norm[l], moe_router[i])
            h = ffn_residual(h, ffn_norm[l], comb, moe_w_gate[i].astype(BF16), moe_w_up[i].astype(BF16),
                             moe_w_down[i].astype(BF16), final_gain)
        if l == n_a - 1:
            cos, sin_a, sin_b = _rope_tables(positions)
            kv, _ = rms_matmul(h, kv_norm, nsa_w_kv.astype(BF16), jnp.zeros((d, LANES), BF16))
            kvr = kv_rope(kv, cos, sin_a, sin_b)
            k_cmp, v_cmp = nsa_compress(kv, cmp_pos_k, cmp_pos_v, cmp_w1_k.astype(BF16),
                                        cmp_w1_v.astype(BF16), cmp_w2_k.astype(BF16),
                                        cmp_w2_v.astype(BF16), batch=batch, seq=seq)
            shared = (cos, sin_a, sin_b, k_cmp, v_cmp, kvr, _overlap_table(seq))
    return h.reshape(batch, seq, d)
```

```python
import functools
import math

import jax
import jax.numpy as jnp
from jax import lax
from jax.experimental import pallas as pl
from jax.experimental.pallas import tpu as pltpu

F32 = jnp.float32
BF16 = jnp.bfloat16

NORM_EPS = 1e-6
LANES = 128
GDN_HEADS = 16
GDN_CONV = 4
GDN_CHUNK = 64
GDN_GROUP = 256
GDN_GROUPS_IN_FLIGHT = 4
GDN_HEADS_PER_STEP = 4
NSA_HEADS = 16
NSA_GROUPS = 4
NSA_HPG = NSA_HEADS // NSA_GROUPS
NSA_N_BRANCH = 3
CMP_BLOCK = 32
CMP_STRIDE = 16
SEL_BLOCK = 64
SEL_TOPK = 16
WINDOW = 512
BIG = 1e9
NEG = -1e30
ROPE_THETA = 500000.0
ROPE_DIM = 32
TOP_K = 2

VMEM_LIMIT_BYTES = 56 * 1024 * 1024


def _params(n_axes):
    return pltpu.CompilerParams(dimension_semantics=("arbitrary",) * n_axes,
                                vmem_limit_bytes=VMEM_LIMIT_BYTES)


def _rms(x, gain):
    ms = jnp.mean(x * x, axis=-1, keepdims=True)
    return x * lax.rsqrt(ms + NORM_EPS) * gain


def _sigmoid(x):
    return 1.0 / (1.0 + jnp.exp(-x))


def _dot(a, b):
    return jnp.dot(a, b, preferred_element_type=F32)


def _dot_nt(a, b):
    return lax.dot_general(a, b, (((1,), (1,)), ((), ())), preferred_element_type=F32)


def _dot_tn(a, b):
    return lax.dot_general(a, b, (((0,), (0,)), ((), ())), preferred_element_type=F32)


def _pick_lane(x, idx):
    lane = lax.broadcasted_iota(jnp.int32, x.shape, 1)
    return jnp.sum(jnp.where(lane == idx, x, 0.0), axis=-1, keepdims=True)


def _rms_matmul_kernel(h_ref, g_ref, w_ref, we_ref, o_ref, oe_ref, xn_ref, *, row_chunk):
    j = pl.program_id(1)

    @pl.when(j == 0)
    def _():
        for r in range(0, h_ref.shape[0], row_chunk):
            xn_ref[r:r + row_chunk, :] = _rms(h_ref[r:r + row_chunk, :], g_ref[...]).astype(BF16)
        oe_ref[...] = _dot(xn_ref[...], we_ref[...])

    o_ref[...] = _dot(xn_ref[...], w_ref[...]).astype(o_ref.dtype)


def rms_matmul(h, gain, w, w_extra, *, tm=1024, tn=1024, out_dtype=F32):
    n, d = h.shape
    nm = w.shape[1]
    tm = min(tm, n)
    tn = min(tn, nm)
    return pl.pallas_call(
        functools.partial(_rms_matmul_kernel, row_chunk=256),
        grid=(n // tm, nm // tn),
        in_specs=[
            pl.BlockSpec((tm, d), lambda i, j: (i, 0)),
            pl.BlockSpec((1, d), lambda i, j: (0, 0)),
            pl.BlockSpec((d, tn), lambda i, j: (0, j)),
            pl.BlockSpec((d, LANES), lambda i, j: (0, 0)),
        ],
        out_specs=[
            pl.BlockSpec((tm, tn), lambda i, j: (i, j)),
            pl.BlockSpec((tm, LANES), lambda i, j: (i, 0)),
        ],
        out_shape=[jax.ShapeDtypeStruct((n, nm), out_dtype),
                   jax.ShapeDtypeStruct((n, LANES), F32)],
        scratch_shapes=[pltpu.VMEM((tm, d), BF16)],
        compiler_params=_params(2),
    )(h, gain.reshape(1, d), w, w_extra)


def _matmul_res_kernel(x_ref, w_ref, r_ref, o_ref):
    o_ref[...] = r_ref[...] + _dot(x_ref[...], w_ref[...])


def matmul_residual(x, w, res, *, tm=1024, tn=512):
    n, k = x.shape
    nm = w.shape[1]
    tm = min(tm, n)
    return pl.pallas_call(
        _matmul_res_kernel,
        grid=(n // tm, nm // tn),
        in_specs=[
            pl.BlockSpec((tm, k), lambda i, j: (i, 0)),
            pl.BlockSpec((k, tn), lambda i, j: (0, j)),
            pl.BlockSpec((tm, tn), lambda i, j: (i, j)),
        ],
        out_specs=pl.BlockSpec((tm, tn), lambda i, j: (i, j)),
        out_shape=jax.ShapeDtypeStruct((n, nm), F32),
        compiler_params=_params(2),
    )(x, w, res)


def _router_kernel(h_ref, g_ref, r_ref, o_ref, *, n_experts, row_chunk):
    for r in range(0, h_ref.shape[0], row_chunk):
        xn = _rms(h_ref[r:r + row_chunk, :], g_ref[...])
        logits = jnp.dot(xn, r_ref[...], preferred_element_type=F32,
                         precision=lax.Precision.HIGHEST)
        lane = lax.broadcasted_iota(jnp.int32, logits.shape, 1)
        valid = lane < n_experts
        z = jnp.where(valid, logits, NEG)
        z = z - jnp.max(z, axis=-1, keepdims=True)
        p = jnp.where(valid, jnp.exp(z), 0.0)
        p = p / jnp.sum(p, axis=-1, keepdims=True)
        p = jnp.where(valid, p, -1.0)
        p1 = jnp.max(p, axis=-1, keepdims=True)
        i1 = jnp.min(jnp.where(p == p1, lane, LANES), axis=-1, keepdims=True)
        pm = jnp.where(lane == i1, -1.0, p)
        p2 = jnp.max(pm, axis=-1, keepdims=True)
        i2 = jnp.min(jnp.where(pm == p2, lane, LANES), axis=-1, keepdims=True)
        den = p1 + p2
        o_ref[r:r + row_chunk, :] = (jnp.where(lane == i1, p1 / den, 0.0)
                                     + jnp.where(lane == i2, p2 / den, 0.0))


def route_tokens(h, gain, router, *, tm=1024):
    n, d = h.shape
    n_experts = router.shape[1]
    tm = min(tm, n)
    r_pad = jnp.pad(router, ((0, 0), (0, LANES - n_experts)))
    return pl.pallas_call(
        functools.partial(_router_kernel, n_experts=n_experts, row_chunk=256),
        grid=(n // tm,),
        in_specs=[
            pl.BlockSpec((tm, d), lambda i: (i, 0)),
            pl.BlockSpec((1, d), lambda i: (0, 0)),
            pl.BlockSpec((d, LANES), lambda i: (0, 0)),
        ],
        out_specs=pl.BlockSpec((tm, LANES), lambda i: (i, 0)),
        out_shape=jax.ShapeDtypeStruct((n, LANES), F32),
        compiler_params=_params(1),
    )(h, gain.reshape(1, d), r_pad)


def _ffn_kernel(h_ref, g_ref, c_ref, wg_ref, wu_ref, wd_ref, fg_ref, o_ref, xn_ref, *,
                n_e, n_f, use_comb, final_norm, row_chunk):
    e = pl.program_id(1)
    f = pl.program_id(2)

    @pl.when((e == 0) & (f == 0))
    def _():
        for r in range(0, h_ref.shape[0], row_chunk):
            x = h_ref[r:r + row_chunk, :]
            xn_ref[r:r + row_chunk, :] = _rms(x, g_ref[...]).astype(BF16)
            o_ref[r:r + row_chunk, :] = x

    xn = xn_ref[...]
    gate = _dot(xn, wg_ref[0])
    up = _dot(xn, wu_ref[0])
    act = gate * _sigmoid(gate) * up
    if use_comb:
        act = act * _pick_lane(c_ref[...], e)
    o_ref[...] += _dot(act.astype(BF16), wd_ref[0])

    if final_norm:
        @pl.when((e == n_e - 1) & (f == n_f - 1))
        def _():
            for r in range(0, h_ref.shape[0], row_chunk):
                o_ref[r:r + row_chunk, :] = _rms(o_ref[r:r + row_chunk, :], fg_ref[...])


def ffn_residual(h, gain, comb, w_gate, w_up, w_down, final_gain, *, tm=512, tf=512):
    n, d = h.shape
    n_e, _, ff = w_gate.shape
    tm = min(tm, n)
    use_comb = comb is not None
    final_norm = final_gain is not None
    if comb is None:
        comb = jnp.ones((n, LANES), F32)
    if final_gain is None:
        final_gain = jnp.ones((d,), F32)
    n_f = ff // tf
    return pl.pallas_call(
        functools.partial(_ffn_kernel, n_e=n_e, n_f=n_f, use_comb=use_comb,
                          final_norm=final_norm, row_chunk=256),
        grid=(n // tm, n_e, n_f),
        in_specs=[
            pl.BlockSpec((tm, d), lambda i, e, f: (i, 0)),
            pl.BlockSpec((1, d), lambda i, e, f: (0, 0)),
            pl.BlockSpec((tm, LANES), lambda i, e, f: (i, 0)),
            pl.BlockSpec((1, d, tf), lambda i, e, f: (e, 0, f)),
            pl.BlockSpec((1, d, tf), lambda i, e, f: (e, 0, f)),
            pl.BlockSpec((1, tf, d), lambda i, e, f: (e, f, 0)),
            pl.BlockSpec((1, d), lambda i, e, f: (0, 0)),
        ],
        out_specs=pl.BlockSpec((tm, d), lambda i, e, f: (i, 0)),
        out_shape=jax.ShapeDtypeStruct((n, d), F32),
        scratch_shapes=[pltpu.VMEM((tm, d), BF16)],
        compiler_params=_params(3),
    )(h, gain.reshape(1, d), comb, w_gate, w_up, w_down, final_gain.reshape(1, d))


def _tri_inv_many(lows, nilpotency):
    c = lows[0].shape[0]
    ii = lax.broadcasted_iota(jnp.int32, (c, c), 0)
    jj = lax.broadcasted_iota(jnp.int32, (c, c), 1)
    eye = jnp.where(ii == jj, 1.0, 0.0)
    lb = [low.astype(BF16) for low in lows]
    ps = [_dot(b, b).astype(BF16) for b in lb]
    ts = [eye - low for low in lows]
    levels = int(math.log2(nilpotency)) - 1
    for level in range(levels):
        tb = [t.astype(BF16) for t in ts]
        if level + 1 < levels:
            nxt = [_dot(p, p).astype(BF16) for p in ps]
        ts = [t + _dot(b, p) for t, b, p in zip(ts, tb, ps)]
        if level + 1 < levels:
            ps = nxt
    return ts


def _gdn_local_kernel(alog_ref, dtb_ref, q_ref, k_ref, v_ref, cq_ref, ck_ref, cv_ref,
                      a1_ref, a2_ref, b2_ref,
                      u_ref, w_ref, qd_ref, kd_ref, aqk_ref, gl_ref,
                      xp_ref, qs_ref, ks_ref, vs_ref, gc1_ref, gc2_ref, gl2_ref, be2_ref, *,
                      seq, row_chunk):
    hd = pl.program_id(1)
    dk = q_ref.shape[1]
    c = GDN_CHUNK
    grp = GDN_GROUP
    n_groups = seq // grp
    pad = 8

    def conv_silu(x_ref, w_ref, dst_ref, normalise, scale):
        xp_ref[0:pad, :] = jnp.zeros((pad, dk), F32)
        xp_ref[pad:pad + seq, :] = x_ref[...].astype(F32)
        w = w_ref[...]
        for r in range(0, seq, row_chunk):
            y = xp_ref[pad - 3 + r:pad - 3 + r + row_chunk, :] * w[0:1, :]
            for j in range(1, GDN_CONV):
                y = y + xp_ref[pad - 3 + j + r:pad - 3 + j + r + row_chunk, :] * w[j:j + 1, :]
            y = y * _sigmoid(y)
            if normalise:
                y = y * lax.rsqrt(jnp.sum(y * y, axis=-1, keepdims=True) + NORM_EPS)
            if scale != 1.0:
                y = y * scale
            dst_ref[r:r + row_chunk, :] = y

    conv_silu(q_ref, cq_ref, qs_ref, True, dk ** -0.5)
    conv_silu(k_ref, ck_ref, ks_ref, True, 1.0)
    conv_silu(v_ref, cv_ref, vs_ref, False, 1.0)

    a_coef = -jnp.exp(alog_ref[hd])
    dtb = dtb_ref[hd]

    def log_decay(a):
        x = a + dtb
        return a_coef * (jnp.maximum(x, 0.0) + jnp.log(1.0 + jnp.exp(-jnp.abs(x))))

    ii = lax.broadcasted_iota(jnp.int32, (grp, grp), 0)
    jj = lax.broadcasted_iota(jnp.int32, (grp, grp), 1)
    same = (ii // c) == (jj // c)
    hi = lax.Precision.HIGHEST
    g1 = log_decay(a1_ref[0, 0])
    g2 = log_decay(a2_ref[0, 0])
    ones_b = jnp.where(same, 1.0, 0.0)
    gc1_ref[...] = jnp.dot(g1, jnp.where(same & (ii <= jj), 1.0, 0.0),
                           preferred_element_type=F32, precision=hi)
    gc2_ref[...] = jnp.dot(jnp.where(same & (jj <= ii), 1.0, 0.0), g2,
                           preferred_element_type=F32, precision=hi)
    gl2_ref[...] = jnp.dot(ones_b, g2, preferred_element_type=F32, precision=hi)
    gl_ref[0, 0] = jnp.exp(jnp.dot(g1, ones_b, preferred_element_type=F32, precision=hi))
    be2_ref[...] = _sigmoid(b2_ref[0, 0])

    causal = same & (ii >= jj)
    strict = same & (ii > jj)
    n_par = GDN_GROUPS_IN_FLIGHT

    def groups(it, carry):
        gis = [it * n_par + t for t in range(n_par)]
        rows = [pl.ds(pl.multiple_of(gi * grp, grp), grp) for gi in gis]
        q = [qs_ref[r, :] for r in rows]
        k = [ks_ref[r, :] for r in rows]
        v = [vs_ref[r, :] for r in rows]
        g_row = [gc1_ref[pl.ds(gi, 1), :] for gi in gis]
        g_col = [_pick_lane(gc2_ref[...], gi) for gi in gis]
        gl_col = [_pick_lane(gl2_ref[...], gi) for gi in gis]
        b_col = [_pick_lane(be2_ref[...], gi) for gi in gis]
        kbf = [x.astype(BF16) for x in k]
        kb = [x * b for x, b in zip(k, b_col)]
        kk = [_dot_nt(x.astype(BF16), y) for x, y in zip(kb, kbf)]
        qk = [_dot_nt(x.astype(BF16), y) for x, y in zip(q, kbf)]
        decay = [jnp.where(causal, jnp.exp(jnp.where(causal, gc - gr, 0.0)), 0.0)
                 for gc, gr in zip(g_col, g_row)]
        t_inv = _tri_inv_many([jnp.where(strict, x * d, 0.0) for x, d in zip(kk, decay)], c)
        e_col = [jnp.exp(gc) for gc in g_col]
        rhs = [jnp.concatenate([x * b, y * e], axis=1).astype(BF16)
               for x, b, y, e in zip(v, b_col, kb, e_col)]
        uw = [_dot(t.astype(BF16), r) for t, r in zip(t_inv, rhs)]
        for t in range(n_par):
            u_ref[rows[t], :] = uw[t][:, :dk].astype(u_ref.dtype)
            w_ref[rows[t], :] = uw[t][:, dk:].astype(w_ref.dtype)
            a_qk = qk[t] * decay[t]
            a_c = a_qk[:, 0:c]
            for blk in range(1, grp // c):
                a_c = a_c + a_qk[:, blk * c:(blk + 1) * c]
            aqk_ref[0, 0, rows[t], :] = a_c.astype(aqk_ref.dtype)
            qd_ref[rows[t], :] = (q[t] * e_col[t]).astype(qd_ref.dtype)
            kd_ref[rows[t], :] = (k[t] * jnp.exp(gl_col[t] - g_col[t])).astype(kd_ref.dtype)
        return carry

    lax.fori_loop(0, n_groups // n_par, groups, 0)


def _gdn_scan_kernel(gl_ref, u_ref, w_ref, qd_ref, kd_ref, aqk_ref, z_ref, on_ref, o_ref, *,
                     seq, heads_per_step):
    b = pl.program_id(0)
    hb = pl.program_id(1)
    c = GDN_CHUNK
    dk = on_ref.shape[1]

    def step(ci, states):
        rows = pl.ds(pl.multiple_of(ci * c, c), c)
        hs = range(heads_per_step)
        cols = [slice(hh * dk, (hh + 1) * dk) for hh in hs]
        ws_qs = [_dot(jnp.concatenate([w_ref[rows, cols[hh]], qd_ref[rows, cols[hh]]], axis=0),
                      states[hh].astype(BF16)) for hh in hs]
        v_new = [(u_ref[rows, cols[hh]].astype(F32) - ws_qs[hh][:c]).astype(BF16) for hh in hs]
        intra = [_dot(aqk_ref[0, hh, rows, :], v_new[hh]) for hh in hs]
        outer = [_dot_tn(kd_ref[rows, cols[hh]], v_new[hh]) for hh in hs]
        new_states = []
        for hh in hs:
            decay = gl_ref[b, hb * heads_per_step + hh, ci]
            new_states.append(states[hh] * decay + outer[hh])
            z = z_ref[rows, cols[hh]].astype(F32)
            o = ws_qs[hh][c:] + intra[hh]
            o_ref[rows, cols[hh]] = (_rms(o, on_ref[...]) * (z * _sigmoid(z))).astype(o_ref.dtype)
        return tuple(new_states)

    lax.fori_loop(0, seq // c, step,
                  tuple(jnp.zeros((dk, dk), F32) for _ in range(heads_per_step)))


def gdn_core(proj, gates, conv_w, a_log, dt_bias, out_norm, *, batch, seq):
    n = proj.shape[0]
    nh = GDN_HEADS
    dk = proj.shape[1] // (4 * nh)
    c = GDN_CHUNK
    grp = GDN_GROUP
    n_groups = seq // grp
    hps = GDN_HEADS_PER_STEP
    bl = gates[:, :nh].reshape(batch, seq, nh).transpose(0, 2, 1)
    al = gates[:, nh:2 * nh].reshape(batch, seq, nh).transpose(0, 2, 1)
    a1 = al.reshape(batch, nh, n_groups, grp)
    a2 = a1.transpose(0, 1, 3, 2)
    b2 = bl.reshape(batch, nh, n_groups, grp).transpose(0, 1, 3, 2)
    smem = pl.BlockSpec(memory_space=pltpu.SMEM)
    col = lambda off: pl.BlockSpec((seq, dk), lambda b, h, off=off: (b, off * nh + h))
    cw = lambda off: pl.BlockSpec((GDN_CONV, dk), lambda b, h, off=off: (0, off * nh + h))
    g1 = pl.BlockSpec((1, 1, n_groups, grp), lambda b, h: (b, h, 0, 0))
    g2 = pl.BlockSpec((1, 1, grp, n_groups), lambda b, h: (b, h, 0, 0))
    head_out = pl.BlockSpec((seq, dk), lambda b, h: (b, h))
    wide = jax.ShapeDtypeStruct((n, nh * dk), BF16)
    u, w, qd, kd, aqk, gl = pl.pallas_call(
        functools.partial(_gdn_local_kernel, seq=seq, row_chunk=256),
        grid=(batch, nh),
        in_specs=[smem, smem, col(0), col(1), col(2), cw(0), cw(1), cw(2), g1, g2, g2],
        out_specs=[head_out, head_out, head_out, head_out,
                   pl.BlockSpec((1, 1, seq, c), lambda b, h: (b, h, 0, 0)), g1],
        out_shape=[wide, wide, wide, wide,
                   jax.ShapeDtypeStruct((batch, nh, seq, c), BF16),
                   jax.ShapeDtypeStruct((batch, nh, n_groups, grp), F32)],
        scratch_shapes=[
            pltpu.VMEM((seq + 8, dk), F32),
            pltpu.VMEM((seq, dk), F32), pltpu.VMEM((seq, dk), F32), pltpu.VMEM((seq, dk), F32),
            pltpu.VMEM((n_groups, grp), F32), pltpu.VMEM((grp, n_groups), F32),
            pltpu.VMEM((grp, n_groups), F32), pltpu.VMEM((grp, n_groups), F32),
        ],
        compiler_params=_params(2),
    )(a_log, dt_bias, proj, proj, proj, conv_w, conv_w, conv_w, a1, a2, b2)
    chunk_decay = gl[..., c - 1::c].reshape(batch, nh, seq // c)
    heads = pl.BlockSpec((seq, hps * dk), lambda b, h: (b, h))
    return pl.pallas_call(
        functools.partial(_gdn_scan_kernel, seq=seq, heads_per_step=hps),
        grid=(batch, nh // hps),
        in_specs=[smem, heads, heads, heads, heads,
                  pl.BlockSpec((1, hps, seq, c), lambda b, h: (b, h, 0, 0)),
                  pl.BlockSpec((seq, hps * dk), lambda b, h: (b, 3 * (nh // hps) + h)),
                  pl.BlockSpec((1, dk), lambda b, h: (0, 0))],
        out_specs=heads,
        out_shape=wide,
        compiler_params=_params(2),
    )(chunk_decay, u, w, qd, kd, aqk, proj, out_norm.reshape(1, dk))


def _rope(x, cos, sin_a, sin_b):
    half = ROPE_DIM // 2
    return (x * cos + pltpu.roll(x, LANES - half, 1) * sin_a + pltpu.roll(x, half, 1) * sin_b)


def _kv_rope_kernel(x_ref, cos_ref, sa_ref, sb_ref, o_ref):
    j = pl.program_id(1)
    tensor = j // NSA_GROUPS
    x = x_ref[...]
    roped = _rope(x, cos_ref[...], sa_ref[...], sb_ref[...])
    o_ref[...] = jnp.where((tensor == 2) | (tensor == 4), roped, x).astype(o_ref.dtype)


def kv_rope(kv, cos, sin_a, sin_b, *, tm=1024):
    n, width = kv.shape
    tm = min(tm, n)
    tab = pl.BlockSpec((tm, LANES), lambda i, j: (i, 0))
    return pl.pallas_call(
        _kv_rope_kernel,
        grid=(n // tm, width // LANES),
        in_specs=[pl.BlockSpec((tm, LANES), lambda i, j: (i, j)), tab, tab, tab],
        out_specs=pl.BlockSpec((tm, LANES), lambda i, j: (i, j)),
        out_shape=jax.ShapeDtypeStruct((n, width), BF16),
        compiler_params=_params(2),
    )(kv, cos, sin_a, sin_b)


def _compress_kernel(kc_ref, vc_ref, pk_ref, pv_ref, w1k_ref, w1v_ref, w2k_ref, w2v_ref,
                     ok_ref, ov_ref, tp_ref, *, seq):
    dh = kc_ref.shape[1]
    n_rows = ok_ref.shape[2]
    hidden = w1k_ref.shape[1]

    def run(t_ref, pos_ref, w1_ref, w2_ref, out_ref):
        tp_ref[0:seq, :] = t_ref[...]
        tp_ref[seq:seq + CMP_STRIDE, :] = jnp.zeros((CMP_STRIDE, dh), F32)
        acc = jnp.zeros((n_rows, hidden), F32)
        for l in range(CMP_BLOCK):
            x = tp_ref[pl.ds(l, n_rows, stride=CMP_STRIDE), :] + pos_ref[l:l + 1, :]
            acc = acc + _dot(x.astype(BF16), w1_ref[l * dh:(l + 1) * dh, :])
        hid = acc * _sigmoid(acc)
        out_ref[0, 0] = _dot(hid.astype(BF16), w2_ref[...]).astype(out_ref.dtype)

    run(kc_ref, pk_ref, w1k_ref, w2k_ref, ok_ref)
    run(vc_ref, pv_ref, w1v_ref, w2v_ref, ov_ref)


def nsa_compress(kv, pos_k, pos_v, w1k, w1v, w2k, w2v, *, batch, seq):
    dh = pos_k.shape[1]
    n_rows = seq // CMP_STRIDE
    hidden = w1k.shape[1]
    full = lambda shape: pl.BlockSpec(shape, lambda b, g: (0,) * len(shape))
    out = pl.BlockSpec((1, 1, n_rows, dh), lambda b, g: (b, g, 0, 0))
    return pl.pallas_call(
        functools.partial(_compress_kernel, seq=seq),
        grid=(batch, NSA_GROUPS),
        in_specs=[
            pl.BlockSpec((seq, dh), lambda b, g: (b, g)),
            pl.BlockSpec((seq, dh), lambda b, g: (b, NSA_GROUPS + g)),
            full((CMP_BLOCK, dh)), full((CMP_BLOCK, dh)),
            full((CMP_BLOCK * dh, hidden)), full((CMP_BLOCK * dh, hidden)),
            full((hidden, dh)), full((hidden, dh)),
        ],
        out_specs=[out, out],
        out_shape=[jax.ShapeDtypeStruct((batch, NSA_GROUPS, n_rows, dh), BF16)] * 2,
        scratch_shapes=[pltpu.VMEM((seq + CMP_STRIDE, dh), F32)],
        compiler_params=_params(2),
    )(kv, kv, pos_k, pos_v, w1k, w1v, w2k, w2v)


def _masked_softmax(s, mask):
    sm = jnp.where(mask, s, NEG)
    m = jnp.max(sm, axis=-1, keepdims=True)
    p = jnp.where(mask, jnp.exp(sm - m), 0.0)
    return p / jnp.maximum(jnp.sum(p, axis=-1, keepdims=True), 1e-30)


def _nsa_kernel(q_ref, gate_ref, cos_ref, sa_ref, sb_ref, kc_ref, vc_ref, ks_ref, vs_ref,
                kw_ref, vw_ref, ov_ref, o_ref, *, seq, tq, kc_len):
    g = pl.program_id(1)
    n = pl.program_id(2)
    t0 = n * tq
    dh = LANES
    hpg = NSA_HPG
    m_rows = hpg * tq
    n_sel = seq // SEL_BLOCK
    topk = min(SEL_TOPK, n_sel)
    scale = dh ** -0.5

    stack = lambda x: jnp.concatenate([x] * hpg, axis=0)
    q = q_ref[...]
    qs = jnp.concatenate([q[:, h * dh:(h + 1) * dh] for h in range(hpg)], axis=0) * scale
    qr = _rope(qs, stack(cos_ref[...]), stack(sa_ref[...]), stack(sb_ref[...]))
    qb = qs.astype(BF16)
    qrb = qr.astype(BF16)
    row = lax.broadcasted_iota(jnp.int32, (m_rows, 1), 0)
    tq_col = t0 + (row & (tq - 1))

    s_c = _dot_nt(qb, kc_ref[0, 0])
    lane_c = lax.broadcasted_iota(jnp.int32, s_c.shape, 1)
    p_c = _masked_softmax(s_c, lane_c * CMP_STRIDE + (CMP_BLOCK - 1) <= tq_col)
    o_c = _dot(p_c.astype(BF16), vc_ref[0, 0])

    p_sum = p_c[0:tq]
    for h in range(1, hpg):
        p_sum = p_sum + p_c[h * tq:(h + 1) * tq]
    imp = jnp.dot(p_sum, ov_ref[...], preferred_element_type=F32,
                  precision=lax.Precision.HIGHEST)
    lane = lax.broadcasted_iota(jnp.int32, (tq, LANES), 1)
    t1 = t0 + lax.broadcasted_iota(jnp.int32, (tq, 1), 0)
    cur = t1 // SEL_BLOCK
    forced = (lane == 0) | (lane == cur) | (lane == cur - 1)
    score = jnp.where(forced, BIG, jnp.where(lane * SEL_BLOCK <= t1, imp, -BIG))
    score = jnp.where(lane < n_sel, score, -3e38)
    rank = jnp.zeros((tq, LANES), F32)
    for j2 in range(n_sel):
        col = score[:, j2:j2 + 1]
        beats = jnp.where(col > score, 1.0, jnp.where((col == score) & (lane > j2), 1.0, 0.0))
        rank = rank + beats
    sel = jnp.where((rank < topk) & (lane < n_sel), 1.0, 0.0)
    sel4 = stack(sel).astype(BF16)

    def sel_step(ci, carry):
        m_i, l_i, acc = carry
        k0 = pl.multiple_of(ci * kc_len, kc_len)
        s = _dot_nt(qrb, ks_ref[pl.ds(k0, kc_len), :])
        blk = lax.broadcasted_iota(jnp.int32, (LANES, kc_len), 0)
        pos = k0 + lax.broadcasted_iota(jnp.int32, (LANES, kc_len), 1)
        expand = jnp.where(blk == pos // SEL_BLOCK, 1.0, 0.0).astype(BF16)
        picked = _dot(sel4, expand)
        kpos = k0 + lax.broadcasted_iota(jnp.int32, (m_rows, kc_len), 1)
        mask = (picked > 0.5) & (kpos <= tq_col)
        sm = jnp.where(mask, s, NEG)
        m_new = jnp.maximum(m_i, jnp.max(sm, axis=-1, keepdims=True))
        alpha = jnp.exp(m_i - m_new)
        p = jnp.where(mask, jnp.exp(sm - m_new), 0.0)
        l_new = alpha * l_i + jnp.sum(p, axis=-1, keepdims=True)
        acc_new = alpha * acc + _dot(p.astype(BF16), vs_ref[pl.ds(k0, kc_len), :])
        return m_new, l_new, acc_new

    n_kc = (t0 + tq + kc_len - 1) // kc_len
    init = (jnp.full((m_rows, 1), NEG, F32), jnp.zeros((m_rows, 1), F32),
            jnp.zeros((m_rows, dh), F32))
    _, l_s, acc_s = lax.fori_loop(0, n_kc, sel_step, init)
    o_s = acc_s / jnp.maximum(l_s, 1e-30)

    wk = WINDOW + tq
    start = pl.multiple_of(jnp.maximum(t0 - WINDOW, 0), LANES)
    s_w = _dot_nt(qrb, kw_ref[pl.ds(start, wk), :])
    kpos_w = start + lax.broadcasted_iota(jnp.int32, (m_rows, wk), 1)
    p_w = _masked_softmax(s_w, (kpos_w <= tq_col) & (kpos_w > tq_col - WINDOW))
    o_w = _dot(p_w.astype(BF16), vw_ref[pl.ds(start, wk), :])

    gates = _sigmoid(gate_ref[...])
    for h in range(hpg):
        base = (g * hpg + h) * NSA_N_BRANCH
        rows = slice(h * tq, (h + 1) * tq)
        out = (_pick_lane(gates, base) * o_c[rows] + _pick_lane(gates, base + 1) * o_s[rows]
               + _pick_lane(gates, base + 2) * o_w[rows])
        o_ref[:, h * dh:(h + 1) * dh] = out.astype(o_ref.dtype)


def nsa_attention(q, gates, cos, sin_a, sin_b, k_cmp, v_cmp, kvr, overlap, *, batch, seq,
                  tq=128, kc_len=512):
    n = q.shape[0]
    dh = LANES
    gw = NSA_HPG * dh
    nq = seq // tq
    kc_len = min(kc_len, seq)
    rows = lambda width: pl.BlockSpec((tq, width), lambda b, g, i: (b * nq + i, 0))
    cmp_spec = pl.BlockSpec((1, 1, k_cmp.shape[2], dh), lambda b, g, i: (b, g, 0, 0))
    kv_spec = lambda t: pl.BlockSpec((seq, dh), lambda b, g, i, t=t: (b, t * NSA_GROUPS + g))
    return pl.pallas_call(
        functools.partial(_nsa_kernel, seq=seq, tq=tq, kc_len=kc_len),
        grid=(batch, NSA_GROUPS, nq),
        in_specs=[
            pl.BlockSpec((tq, gw), lambda b, g, i: (b * nq + i, g)),
            rows(LANES), rows(LANES), rows(LANES), rows(LANES),
            cmp_spec, cmp_spec, kv_spec(2), kv_spec(3), kv_spec(4), kv_spec(5),
            pl.BlockSpec((LANES, LANES), lambda b, g, i: (0, 0)),
        ],
        out_specs=pl.BlockSpec((tq, gw), lambda b, g, i: (b * nq + i, g)),
        out_shape=jax.ShapeDtypeStruct((n, NSA_HEADS * dh), BF16),
        compiler_params=_params(3),
    )(q, gates, cos, sin_a, sin_b, k_cmp, v_cmp, kvr, kvr, kvr, kvr, overlap)


def _rope_tables(positions):
    half = ROPE_DIM // 2
    inv = ROPE_THETA ** (-jnp.arange(0, ROPE_DIM, 2, dtype=F32) / ROPE_DIM)
    ang = positions.astype(F32).reshape(-1, 1) * inv
    cos, sin = jnp.cos(ang), jnp.sin(ang)
    n = ang.shape[0]
    ones = jnp.ones((n, LANES - ROPE_DIM), F32)
    zeros_h = jnp.zeros((n, half), F32)
    zeros_t = jnp.zeros((n, LANES - ROPE_DIM), F32)
    cos_t = jnp.concatenate([cos, cos, ones], axis=1)
    sin_a = jnp.concatenate([-sin, zeros_h, zeros_t], axis=1)
    sin_b = jnp.concatenate([zeros_h, sin, zeros_t], axis=1)
    return cos_t, sin_a, sin_b


def _overlap_table(seq):
    n_cmp = (seq - CMP_BLOCK) // CMP_STRIDE + 1
    n_sel = seq // SEL_BLOCK
    c_idx = jnp.arange(n_cmp)
    j_idx = jnp.arange(n_sel)
    ov = jnp.clip(jnp.minimum(c_idx[:, None] * CMP_STRIDE + CMP_BLOCK, (j_idx[None, :] + 1) * SEL_BLOCK)
                  - jnp.maximum(c_idx[:, None] * CMP_STRIDE, j_idx[None, :] * SEL_BLOCK), 0)
    ov = ov.astype(F32) / CMP_STRIDE
    return jnp.pad(ov, ((0, LANES - n_cmp), (0, LANES - n_sel)))


def _split_pad(w, main):
    extra = w[:, main:]
    return w[:, :main].astype(BF16), jnp.pad(extra, ((0, 0), (0, LANES - extra.shape[1]))).astype(BF16)


def kernel(x, positions, attn_norm, ffn_norm, final_norm, gdn_w_in, gdn_conv, gdn_a_log, gdn_dt_bias,
           gdn_out_norm, gdn_w_out, kv_norm, nsa_w_kv, cmp_pos_k, cmp_w1_k, cmp_w2_k, cmp_pos_v, cmp_w1_v,
           cmp_w2_v, nsa_w_q, nsa_w_o, ffn_w_gate, ffn_w_up, ffn_w_down, moe_router, moe_w_gate, moe_w_up,
           moe_w_down):
    batch, seq, d = x.shape
    depth = attn_norm.shape[0]
    n_a = depth // 2
    h = x.reshape(batch * seq, d)
    gdn_width = gdn_w_out.shape[1]
    nsa_width = nsa_w_o.shape[1]
    shared = None
    for l in range(depth):
        if l < n_a:
            w_main, w_gates = _split_pad(gdn_w_in[l], 4 * gdn_width)
            proj, gates = rms_matmul(h, attn_norm[l], w_main, w_gates, out_dtype=BF16)
            o = gdn_core(proj, gates, gdn_conv[l], gdn_a_log[l], gdn_dt_bias[l], gdn_out_norm[l],
                         batch=batch, seq=seq)
            h = matmul_residual(o, gdn_w_out[l].astype(BF16), h)
        else:
            cos, sin_a, sin_b, k_cmp, v_cmp, kvr, overlap = shared
            w_main, w_gates = _split_pad(nsa_w_q[l - n_a], nsa_width)
            q, gates = rms_matmul(h, attn_norm[l], w_main, w_gates)
            o = nsa_attention(q, gates, cos, sin_a, sin_b, k_cmp, v_cmp, kvr, overlap,
                              batch=batch, seq=seq)
            h = matmul_residual(o, nsa_w_o[l - n_a].astype(BF16), h)
        final_gain = final_norm if l == depth - 1 else None
        if l % 2 == 0:
            i = l // 2
            h = ffn_residual(h, ffn_norm[l], None, ffn_w_gate[i][None].astype(BF16),
                             ffn_w_up[i][None].astype(BF16), ffn_w_down[i][None].astype(BF16), final_gain)
        else:
            i = l // 2
            comb = route_tokens(h, ffn_norm[l], moe_router[i])
            h = ffn_residual(h, ffn_norm[l], comb, moe_w_gate[i].astype(BF16), moe_w_up[i].astype(BF16),
                             moe_w_down[i].astype(BF16), final_gain)
        if l == n_a - 1:
            cos, sin_a, sin_b = _rope_tables(positions)
            kv, _ = rms_matmul(h, kv_norm, nsa_w_kv.astype(BF16), jnp.zeros((d, LANES), BF16))
            kvr = kv_rope(kv, cos, sin_a, sin_b)
            k_cmp, v_cmp = nsa_compress(kv, cmp_pos_k, cmp_pos_v, cmp_w1_k.astype(BF16),
                                        cmp_w1_v.astype(BF16), cmp_w2_k.astype(BF16),
                                        cmp_w2_v.astype(BF16), batch=batch, seq=seq)
            shared = (cos, sin_a, sin_b, k_cmp, v_cmp, kvr, _overlap_table(seq))
    return h.reshape(batch, seq, d)
```

```python
import functools
import math

import jax
import jax.numpy as jnp
from jax import lax
from jax.experimental import pallas as pl
from jax.experimental.pallas import tpu as pltpu

F32 = jnp.float32
BF16 = jnp.bfloat16

NORM_EPS = 1e-6
LANES = 128
GDN_HEADS = 16
GDN_CONV = 4
GDN_CHUNK = 64
GDN_GROUP = 256
GDN_GROUPS_IN_FLIGHT = 4
GDN_HEADS_PER_STEP = 4
NSA_HEADS = 16
NSA_GROUPS = 4
NSA_HPG = NSA_HEADS // NSA_GROUPS
NSA_N_BRANCH = 3
CMP_BLOCK = 32
CMP_STRIDE = 16
SEL_BLOCK = 64
SEL_TOPK = 16
WINDOW = 512
BIG = 1e9
NEG = -1e30
ROPE_THETA = 500000.0
ROPE_DIM = 32
TOP_K = 2

VMEM_LIMIT_BYTES = 56 * 1024 * 1024


def _params(n_axes):
    return pltpu.CompilerParams(dimension_semantics=("arbitrary",) * n_axes,
                                vmem_limit_bytes=VMEM_LIMIT_BYTES)


def _rms(x, gain):
    ms = jnp.mean(x * x, axis=-1, keepdims=True)
    return x * lax.rsqrt(ms + NORM_EPS) * gain


def _sigmoid(x):
    return 1.0 / (1.0 + jnp.exp(-x))


def _dot(a, b):
    return jnp.dot(a, b, preferred_element_type=F32)


def _dot_nt(a, b):
    return lax.dot_general(a, b, (((1,), (1,)), ((), ())), preferred_element_type=F32)


def _dot_tn(a, b):
    return lax.dot_general(a, b, (((0,), (0,)), ((), ())), preferred_element_type=F32)


def _pick_lane(x, idx):
    lane = lax.broadcasted_iota(jnp.int32, x.shape, 1)
    return jnp.sum(jnp.where(lane == idx, x, 0.0), axis=-1, keepdims=True)


def _rms_matmul_kernel(h_ref, g_ref, w_ref, we_ref, o_ref, oe_ref, xn_ref, *, row_chunk):
    j = pl.program_id(1)

    @pl.when(j == 0)
    def _():
        for r in range(0, h_ref.shape[0], row_chunk):
            xn_ref[r:r + row_chunk, :] = _rms(h_ref[r:r + row_chunk, :], g_ref[...]).astype(BF16)
        oe_ref[...] = _dot(xn_ref[...], we_ref[...])

    o_ref[...] = _dot(xn_ref[...], w_ref[...]).astype(o_ref.dtype)


def rms_matmul(h, gain, w, w_extra, *, tm=1024, tn=1024, out_dtype=F32):
    n, d = h.shape
    nm = w.shape[1]
    tm = min(tm, n)
    tn = min(tn, nm)
    return pl.pallas_call(
        functools.partial(_rms_matmul_kernel, row_chunk=256),
        grid=(n // tm, nm // tn),
        in_specs=[
            pl.BlockSpec((tm, d), lambda i, j: (i, 0)),
            pl.BlockSpec((1, d), lambda i, j: (0, 0)),
            pl.BlockSpec((d, tn), lambda i, j: (0, j)),
            pl.BlockSpec((d, LANES), lambda i, j: (0, 0)),
        ],
        out_specs=[
            pl.BlockSpec((tm, tn), lambda i, j: (i, j)),
            pl.BlockSpec((tm, LANES), lambda i, j: (i, 0)),
        ],
        out_shape=[jax.ShapeDtypeStruct((n, nm), out_dtype),
                   jax.ShapeDtypeStruct((n, LANES), F32)],
        scratch_shapes=[pltpu.VMEM((tm, d), BF16)],
        compiler_params=_params(2),
    )(h, gain.reshape(1, d), w, w_extra)


def _matmul_res_kernel(x_ref, w_ref, r_ref, o_ref):
    o_ref[...] = r_ref[...] + _dot(x_ref[...], w_ref[...])


def matmul_residual(x, w, res, *, tm=1024, tn=512):
    n, k = x.shape
    nm = w.shape[1]
    tm = min(tm, n)
    return pl.pallas_call(
        _matmul_res_kernel,
        grid=(n // tm, nm // tn),
        in_specs=[
            pl.BlockSpec((tm, k), lambda i, j: (i, 0)),
            pl.BlockSpec((k, tn), lambda i, j: (0, j)),
            pl.BlockSpec((tm, tn), lambda i, j: (i, j)),
        ],
        out_specs=pl.BlockSpec((tm, tn), lambda i, j: (i, j)),
        out_shape=jax.ShapeDtypeStruct((n, nm), F32),
        compiler_params=_params(2),
    )(x, w, res)


def _router_kernel(h_ref, g_ref, r_ref, o_ref, *, n_experts, row_chunk):
    for r in range(0, h_ref.shape[0], row_chunk):
        xn = _rms(h_ref[r:r + row_chunk, :], g_ref[...])
        logits = jnp.dot(xn, r_ref[...], preferred_element_type=F32,
                         precision=lax.Precision.HIGHEST)
        lane = lax.broadcasted_iota(jnp.int32, logits.shape, 1)
        valid = lane < n_experts
        z = jnp.where(valid, logits, NEG)
        z = z - jnp.max(z, axis=-1, keepdims=True)
        p = jnp.where(valid, jnp.exp(z), 0.0)
        p = p / jnp.sum(p, axis=-1, keepdims=True)
        p = jnp.where(valid, p, -1.0)
        p1 = jnp.max(p, axis=-1, keepdims=True)
        i1 = jnp.min(jnp.where(p == p1, lane, LANES), axis=-1, keepdims=True)
        pm = jnp.where(lane == i1, -1.0, p)
        p2 = jnp.max(pm, axis=-1, keepdims=True)
        i2 = jnp.min(jnp.where(pm == p2, lane, LANES), axis=-1, keepdims=True)
        den = p1 + p2
        o_ref[r:r + row_chunk, :] = (jnp.where(lane == i1, p1 / den, 0.0)
                                     + jnp.where(lane == i2, p2 / den, 0.0))


def route_tokens(h, gain, router, *, tm=1024):
    n, d = h.shape
    n_experts = router.shape[1]
    tm = min(tm, n)
    r_pad = jnp.pad(router, ((0, 0), (0, LANES - n_experts)))
    return pl.pallas_call(
        functools.partial(_router_kernel, n_experts=n_experts, row_chunk=256),
        grid=(n // tm,),
        in_specs=[
            pl.BlockSpec((tm, d), lambda i: (i, 0)),
            pl.BlockSpec((1, d), lambda i: (0, 0)),
            pl.BlockSpec((d, LANES), lambda i: (0, 0)),
        ],
        out_specs=pl.BlockSpec((tm, LANES), lambda i: (i, 0)),
        out_shape=jax.ShapeDtypeStruct((n, LANES), F32),
        compiler_params=_params(1),
    )(h, gain.reshape(1, d), r_pad)


def _ffn_kernel(h_ref, g_ref, c_ref, wg_ref, wu_ref, wd_ref, fg_ref, o_ref, xn_ref, *,
                n_e, n_f, use_comb, final_norm, row_chunk):
    e = pl.program_id(1)
    f = pl.program_id(2)

    @pl.when((e == 0) & (f == 0))
    def _():
        for r in range(0, h_ref.shape[0], row_chunk):
            x = h_ref[r:r + row_chunk, :]
            xn_ref[r:r + row_chunk, :] = _rms(x, g_ref[...]).astype(BF16)
            o_ref[r:r + row_chunk, :] = x

    xn = xn_ref[...]
    gate = _dot(xn, wg_ref[0])
    up = _dot(xn, wu_ref[0])
    act = gate * _sigmoid(gate) * up
    if use_comb:
        act = act * _pick_lane(c_ref[...], e)
    o_ref[...] += _dot(act.astype(BF16), wd_ref[0])

    if final_norm:
        @pl.when((e == n_e - 1) & (f == n_f - 1))
        def _():
            for r in range(0, h_ref.shape[0], row_chunk):
                o_ref[r:r + row_chunk, :] = _rms(o_ref[r:r + row_chunk, :], fg_ref[...])


def ffn_residual(h, gain, comb, w_gate, w_up, w_down, final_gain, *, tm=512, tf=512):
    n, d = h.shape
    n_e, _, ff = w_gate.shape
    tm = min(tm, n)
    use_comb = comb is not None
    final_norm = final_gain is not None
    if comb is None:
        comb = jnp.ones((n, LANES), F32)
    if final_gain is None:
        final_gain = jnp.ones((d,), F32)
    n_f = ff // tf
    return pl.pallas_call(
        functools.partial(_ffn_kernel, n_e=n_e, n_f=n_f, use_comb=use_comb,
                          final_norm=final_norm, row_chunk=256),
        grid=(n // tm, n_e, n_f),
        in_specs=[
            pl.BlockSpec((tm, d), lambda i, e, f: (i, 0)),
            pl.BlockSpec((1, d), lambda i, e, f: (0, 0)),
            pl.BlockSpec((tm, LANES), lambda i, e, f: (i, 0)),
            pl.BlockSpec((1, d, tf), lambda i, e, f: (e, 0, f)),
            pl.BlockSpec((1, d, tf), lambda i, e, f: (e, 0, f)),
            pl.BlockSpec((1, tf, d), lambda i, e, f: (e, f, 0)),
            pl.BlockSpec((1, d), lambda i, e, f: (0, 0)),
        ],
        out_specs=pl.BlockSpec((tm, d), lambda i, e, f: (i, 0)),
        out_shape=jax.ShapeDtypeStruct((n, d), F32),
        scratch_shapes=[pltpu.VMEM((tm, d), BF16)],
        compiler_params=_params(3),
    )(h, gain.reshape(1, d), comb, w_gate, w_up, w_down, final_gain.reshape(1, d))


def _tri_inv_many(lows, nilpotency):
    c = lows[0].shape[0]
    ii = lax.broadcasted_iota(jnp.int32, (c, c), 0)
    jj = lax.broadcasted_iota(jnp.int32, (c, c), 1)
    eye = jnp.where(ii == jj, 1.0, 0.0)
    lb = [low.astype(BF16) for low in lows]
    ps = [_dot(b, b).astype(BF16) for b in lb]
    ts = [eye - low for low in lows]
    levels = int(math.log2(nilpotency)) - 1
    for level in range(levels):
        tb = [t.astype(BF16) for t in ts]
        if level + 1 < levels:
            nxt = [_dot(p, p).astype(BF16) for p in ps]
        ts = [t + _dot(b, p) for t, b, p in zip(ts, tb, ps)]
        if level + 1 < levels:
            ps = nxt
    return ts


def _gdn_local_kernel(alog_ref, dtb_ref, q_ref, k_ref, v_ref, cq_ref, ck_ref, cv_ref,
                      a1_ref, a2_ref, b2_ref,
                      u_ref, w_ref, qd_ref, kd_ref, aqk_ref, gl_ref,
                      xp_ref, qs_ref, ks_ref, vs_ref, gc1_ref, gc2_ref, gl2_ref, be2_ref, *,
                      seq, row_chunk):
    hd = pl.program_id(1)
    dk = q_ref.shape[1]
    c = GDN_CHUNK
    grp = GDN_GROUP
    n_groups = seq // grp
    pad = 8

    def conv_silu(x_ref, w_ref, dst_ref, normalise, scale):
        xp_ref[0:pad, :] = jnp.zeros((pad, dk), F32)
        xp_ref[pad:pad + seq, :] = x_ref[...].astype(F32)
        w = w_ref[...]
        for r in range(0, seq, row_chunk):
            y = xp_ref[pad - 3 + r:pad - 3 + r + row_chunk, :] * w[0:1, :]
            for j in range(1, GDN_CONV):
                y = y + xp_ref[pad - 3 + j + r:pad - 3 + j + r + row_chunk, :] * w[j:j + 1, :]
            y = y * _sigmoid(y)
            if normalise:
                y = y * lax.rsqrt(jnp.sum(y * y, axis=-1, keepdims=True) + NORM_EPS)
            if scale != 1.0:
                y = y * scale
            dst_ref[r:r + row_chunk, :] = y

    conv_silu(q_ref, cq_ref, qs_ref, True, dk ** -0.5)
    conv_silu(k_ref, ck_ref, ks_ref, True, 1.0)
    conv_silu(v_ref, cv_ref, vs_ref, False, 1.0)

    a_coef = -jnp.exp(alog_ref[hd])
    dtb = dtb_ref[hd]

    def log_decay(a):
        x = a + dtb
        return a_coef * (jnp.maximum(x, 0.0) + jnp.log(1.0 + jnp.exp(-jnp.abs(x))))

    ii = lax.broadcasted_iota(jnp.int32, (grp, grp), 0)
    jj = lax.broadcasted_iota(jnp.int32, (grp, grp), 1)
    same = (ii // c) == (jj // c)
    hi = lax.Precision.HIGHEST
    g1 = log_decay(a1_ref[0, 0])
    g2 = log_decay(a2_ref[0, 0])
    ones_b = jnp.where(same, 1.0, 0.0)
    gc1_ref[...] = jnp.dot(g1, jnp.where(same & (ii <= jj), 1.0, 0.0),
                           preferred_element_type=F32, precision=hi)
    gc2_ref[...] = jnp.dot(jnp.where(same & (jj <= ii), 1.0, 0.0), g2,
                           preferred_element_type=F32, precision=hi)
    gl2_ref[...] = jnp.dot(ones_b, g2, preferred_element_type=F32, precision=hi)
    gl_ref[0, 0] = jnp.exp(jnp.dot(g1, ones_b, preferred_element_type=F32, precision=hi))
    be2_ref[...] = _sigmoid(b2_ref[0, 0])

    causal = same & (ii >= jj)
    strict = same & (ii > jj)
    n_par = GDN_GROUPS_IN_FLIGHT

    def groups(it, carry):
        gis = [it * n_par + t for t in range(n_par)]
        rows = [pl.ds(pl.multiple_of(gi * grp, grp), grp) for gi in gis]
        q = [qs_ref[r, :] for r in rows]
        k = [ks_ref[r, :] for r in rows]
        v = [vs_ref[r, :] for r in rows]
        g_row = [gc1_ref[pl.ds(gi, 1), :] for gi in gis]
        g_col = [_pick_lane(gc2_ref[...], gi) for gi in gis]
        gl_col = [_pick_lane(gl2_ref[...], gi) for gi in gis]
        b_col = [_pick_lane(be2_ref[...], gi) for gi in gis]
        kbf = [x.astype(BF16) for x in k]
        kb = [x * b for x, b in zip(k, b_col)]
        kk = [_dot_nt(x.astype(BF16), y) for x, y in zip(kb, kbf)]
        qk = [_dot_nt(x.astype(BF16), y) for x, y in zip(q, kbf)]
        decay = [jnp.where(causal, jnp.exp(jnp.where(causal, gc - gr, 0.0)), 0.0)
                 for gc, gr in zip(g_col, g_row)]
        t_inv = _tri_inv_many([jnp.where(strict, x * d, 0.0) for x, d in zip(kk, decay)], c)
        e_col = [jnp.exp(gc) for gc in g_col]
        rhs = [jnp.concatenate([x * b, y * e], axis=1).astype(BF16)
               for x, b, y, e in zip(v, b_col, kb, e_col)]
        uw = [_dot(t.astype(BF16), r) for t, r in zip(t_inv, rhs)]
        for t in range(n_par):
            u_ref[rows[t], :] = uw[t][:, :dk].astype(u_ref.dtype)
            w_ref[rows[t], :] = uw[t][:, dk:].astype(w_ref.dtype)
            a_qk = qk[t] * decay[t]
            a_c = a_qk[:, 0:c]
            for blk in range(1, grp // c):
                a_c = a_c + a_qk[:, blk * c:(blk + 1) * c]
            aqk_ref[0, 0, rows[t], :] = a_c.astype(aqk_ref.dtype)
            qd_ref[rows[t], :] = (q[t] * e_col[t]).astype(qd_ref.dtype)
            kd_ref[rows[t], :] = (k[t] * jnp.exp(gl_col[t] - g_col[t])).astype(kd_ref.dtype)
        return carry

    lax.fori_loop(0, n_groups // n_par, groups, 0)


def _gdn_scan_kernel(gl_ref, u_ref, w_ref, qd_ref, kd_ref, aqk_ref, z_ref, on_ref, o_ref, *,
                     seq, heads_per_step):
    b = pl.program_id(0)
    hb = pl.program_id(1)
    c = GDN_CHUNK
    dk = on_ref.shape[1]

    def step(ci, states):
        rows = pl.ds(pl.multiple_of(ci * c, c), c)
        hs = range(heads_per_step)
        cols = [slice(hh * dk, (hh + 1) * dk) for hh in hs]
        ws_qs = [_dot(jnp.concatenate([w_ref[rows, cols[hh]], qd_ref[rows, cols[hh]]], axis=0),
                      states[hh].astype(BF16)) for hh in hs]
        v_new = [(u_ref[rows, cols[hh]].astype(F32) - ws_qs[hh][:c]).astype(BF16) for hh in hs]
        intra = [_dot(aqk_ref[0, hh, rows, :], v_new[hh]) for hh in hs]
        outer = [_dot_tn(kd_ref[rows, cols[hh]], v_new[hh]) for hh in hs]
        new_states = []
        for hh in hs:
            decay = gl_ref[b, hb * heads_per_step + hh, ci]
            new_states.append(states[hh] * decay + outer[hh])
            z = z_ref[rows, cols[hh]].astype(F32)
            o = ws_qs[hh][c:] + intra[hh]
            o_ref[rows, cols[hh]] = (_rms(o, on_ref[...]) * (z * _sigmoid(z))).astype(o_ref.dtype)
        return tuple(new_states)

    lax.fori_loop(0, seq // c, step,
                  tuple(jnp.zeros((dk, dk), F32) for _ in range(heads_per_step)))


def gdn_core(proj, gates, conv_w, a_log, dt_bias, out_norm, *, batch, seq):
    n = proj.shape[0]
    nh = GDN_HEADS
    dk = proj.shape[1] // (4 * nh)
    c = GDN_CHUNK
    grp = GDN_GROUP
    n_groups = seq // grp
    hps = GDN_HEADS_PER_STEP
    bl = gates[:, :nh].reshape(batch, seq, nh).transpose(0, 2, 1)
    al = gates[:, nh:2 * nh].reshape(batch, seq, nh).transpose(0, 2, 1)
    a1 = al.reshape(batch, nh, n_groups, grp)
    a2 = a1.transpose(0, 1, 3, 2)
    b2 = bl.reshape(batch, nh, n_groups, grp).transpose(0, 1, 3, 2)
    smem = pl.BlockSpec(memory_space=pltpu.SMEM)
    col = lambda off: pl.BlockSpec((seq, dk), lambda b, h, off=off: (b, off * nh + h))
    cw = lambda off: pl.BlockSpec((GDN_CONV, dk), lambda b, h, off=off: (0, off * nh + h))
    g1 = pl.BlockSpec((1, 1, n_groups, grp), lambda b, h: (b, h, 0, 0))
    g2 = pl.BlockSpec((1, 1, grp, n_groups), lambda b, h: (b, h, 0, 0))
    head_out = pl.BlockSpec((seq, dk), lambda b, h: (b, h))
    wide = jax.ShapeDtypeStruct((n, nh * dk), BF16)
    u, w, qd, kd, aqk, gl = pl.pallas_call(
        functools.partial(_gdn_local_kernel, seq=seq, row_chunk=256),
        grid=(batch, nh),
        in_specs=[smem, smem, col(0), col(1), col(2), cw(0), cw(1), cw(2), g1, g2, g2],
        out_specs=[head_out, head_out, head_out, head_out,
                   pl.BlockSpec((1, 1, seq, c), lambda b, h: (b, h, 0, 0)), g1],
        out_shape=[wide, wide, wide, wide,
                   jax.ShapeDtypeStruct((batch, nh, seq, c), BF16),
                   jax.ShapeDtypeStruct((batch, nh, n_groups, grp), F32)],
        scratch_shapes=[
            pltpu.VMEM((seq + 8, dk), F32),
            pltpu.VMEM((seq, dk), F32), pltpu.VMEM((seq, dk), F32), pltpu.VMEM((seq, dk), F32),
            pltpu.VMEM((n_groups, grp), F32), pltpu.VMEM((grp, n_groups), F32),
            pltpu.VMEM((grp, n_groups), F32), pltpu.VMEM((grp, n_groups), F32),
        ],
        compiler_params=_params(2),
    )(a_log, dt_bias, proj, proj, proj, conv_w, conv_w, conv_w, a1, a2, b2)
    chunk_decay = gl[..., c - 1::c].reshape(batch, nh, seq // c)
    heads = pl.BlockSpec((seq, hps * dk), lambda b, h: (b, h))
    return pl.pallas_call(
        functools.partial(_gdn_scan_kernel, seq=seq, heads_per_step=hps),
        grid=(batch, nh // hps),
        in_specs=[smem, heads, heads, heads, heads,
                  pl.BlockSpec((1, hps, seq, c), lambda b, h: (b, h, 0, 0)),
                  pl.BlockSpec((seq, hps * dk), lambda b, h: (b, 3 * (nh // hps) + h)),
                  pl.BlockSpec((1, dk), lambda b, h: (0, 0))],
        out_specs=heads,
        out_shape=wide,
        compiler_params=_params(2),
    )(chunk_decay, u, w, qd, kd, aqk, proj, out_norm.reshape(1, dk))


def _rope(x, cos, sin_a, sin_b):
    half = ROPE_DIM // 2
    return (x * cos + pltpu.roll(x, LANES - half, 1) * sin_a + pltpu.roll(x, half, 1) * sin_b)


def _kv_rope_kernel(x_ref, cos_ref, sa_ref, sb_ref, o_ref):
    j = pl.program_id(1)
    tensor = j // NSA_GROUPS
    x = x_ref[...]
    roped = _rope(x, cos_ref[...], sa_ref[...], sb_ref[...])
    o_ref[...] = jnp.where((tensor == 2) | (tensor == 4), roped, x).astype(o_ref.dtype)


def kv_rope(kv, cos, sin_a, sin_b, *, tm=1024):
    n, width = kv.shape
    tm = min(tm, n)
    tab = pl.BlockSpec((tm, LANES), lambda i, j: (i, 0))
    return pl.pallas_call(
        _kv_rope_kernel,
        grid=(n // tm, width // LANES),
        in_specs=[pl.BlockSpec((tm, LANES), lambda i, j: (i, j)), tab, tab, tab],
        out_specs=pl.BlockSpec((tm, LANES), lambda i, j: (i, j)),
        out_shape=jax.ShapeDtypeStruct((n, width), BF16),
        compiler_params=_params(2),
    )(kv, cos, sin_a, sin_b)


def _compress_kernel(kc_ref, vc_ref, pk_ref, pv_ref, w1k_ref, w1v_ref, w2k_ref, w2v_ref,
                     ok_ref, ov_ref, tp_ref, *, seq):
    dh = kc_ref.shape[1]
    n_rows = ok_ref.shape[2]
    hidden = w1k_ref.shape[1]

    def run(t_ref, pos_ref, w1_ref, w2_ref, out_ref):
        tp_ref[0:seq, :] = t_ref[...]
        tp_ref[seq:seq + CMP_STRIDE, :] = jnp.zeros((CMP_STRIDE, dh), F32)
        acc = jnp.zeros((n_rows, hidden), F32)
        for l in range(CMP_BLOCK):
            x = tp_ref[pl.ds(l, n_rows, stride=CMP_STRIDE), :] + pos_ref[l:l + 1, :]
            acc = acc + _dot(x.astype(BF16), w1_ref[l * dh:(l + 1) * dh, :])
        hid = acc * _sigmoid(acc)
        out_ref[0, 0] = _dot(hid.astype(BF16), w2_ref[...]).astype(out_ref.dtype)

    run(kc_ref, pk_ref, w1k_ref, w2k_ref, ok_ref)
    run(vc_ref, pv_ref, w1v_ref, w2v_ref, ov_ref)


def nsa_compress(kv, pos_k, pos_v, w1k, w1v, w2k, w2v, *, batch, seq):
    dh = pos_k.shape[1]
    n_rows = seq // CMP_STRIDE
    hidden = w1k.shape[1]
    full = lambda shape: pl.BlockSpec(shape, lambda b, g: (0,) * len(shape))
    out = pl.BlockSpec((1, 1, n_rows, dh), lambda b, g: (b, g, 0, 0))
    return pl.pallas_call(
        functools.partial(_compress_kernel, seq=seq),
        grid=(batch, NSA_GROUPS),
        in_specs=[
            pl.BlockSpec((seq, dh), lambda b, g: (b, g)),
            pl.BlockSpec((seq, dh), lambda b, g: (b, NSA_GROUPS + g)),
            full((CMP_BLOCK, dh)), full((CMP_BLOCK, dh)),
            full((CMP_BLOCK * dh, hidden)), full((CMP_BLOCK * dh, hidden)),
            full((hidden, dh)), full((hidden, dh)),
        ],
        out_specs=[out, out],
        out_shape=[jax.ShapeDtypeStruct((batch, NSA_GROUPS, n_rows, dh), BF16)] * 2,
        scratch_shapes=[pltpu.VMEM((seq + CMP_STRIDE, dh), F32)],
        compiler_params=_params(2),
    )(kv, kv, pos_k, pos_v, w1k, w1v, w2k, w2v)


def _masked_softmax(s, mask):
    sm = jnp.where(mask, s, NEG)
    m = jnp.max(sm, axis=-1, keepdims=True)
    p = jnp.where(mask, jnp.exp(sm - m), 0.0)
    return p / jnp.maximum(jnp.sum(p, axis=-1, keepdims=True), 1e-30)


def _nsa_kernel(q_ref, gate_ref, cos_ref, sa_ref, sb_ref, kc_ref, vc_ref, ks_ref, vs_ref,
                kw_ref, vw_ref, ovt_ref, eb_ref, wb_ref, o_ref, *, seq, tq, kc_len):
    g = pl.program_id(1)
    n = pl.program_id(2)
    t0 = n * tq
    dh = LANES
    hpg = NSA_HPG
    m_rows = hpg * tq
    n_sel = seq // SEL_BLOCK
    topk = min(SEL_TOPK, n_sel)
    scale = dh ** -0.5

    stack = lambda x: jnp.concatenate([x] * hpg, axis=0)
    q = q_ref[...]
    qs = jnp.concatenate([q[:, h * dh:(h + 1) * dh] for h in range(hpg)], axis=0) * scale
    qr = _rope(qs, stack(cos_ref[...]), stack(sa_ref[...]), stack(sb_ref[...]))
    qb = qs.astype(BF16)
    qrb = qr.astype(BF16)
    row = lax.broadcasted_iota(jnp.int32, (m_rows, 1), 0)
    tq_col = t0 + (row & (tq - 1))

    s_c = _dot_nt(qb, kc_ref[0, 0])
    lane_c = lax.broadcasted_iota(jnp.int32, s_c.shape, 1)
    p_c = _masked_softmax(s_c, lane_c * CMP_STRIDE + (CMP_BLOCK - 1) <= tq_col)
    o_c = _dot(p_c.astype(BF16), vc_ref[0, 0])

    p_sum = p_c[0:tq]
    for h in range(1, hpg):
        p_sum = p_sum + p_c[h * tq:(h + 1) * tq]
    imp_t = lax.dot_general(ovt_ref[...], p_sum, (((1,), (1,)), ((), ())),
                            preferred_element_type=F32, precision=lax.Precision.HIGHEST)
    nsp = -(-n_sel // 8) * 8
    imp_t = imp_t[0:nsp]
    blk = lax.broadcasted_iota(jnp.int32, (nsp, tq), 0)
    t1 = t0 + lax.broadcasted_iota(jnp.int32, (nsp, tq), 1)
    cur = t1 // SEL_BLOCK
    forced = (blk == 0) | (blk == cur) | (blk == cur - 1)
    visible = blk * SEL_BLOCK <= t1
    score = jnp.where(forced, BIG, jnp.where(visible, imp_t, -BIG))
    score = jnp.where(blk < n_sel, score, -3e38)
    rank = jnp.zeros((nsp, tq), F32)
    for j2 in range(n_sel):
        other = score[j2:j2 + 1, :]
        rank = rank + jnp.where(other > score, 1.0,
                                jnp.where((other == score) & (blk > j2), 1.0, 0.0))
    chosen = (rank < topk) & visible & (blk < n_sel)
    bias_t = jnp.where(chosen, 0.0, NEG)
    if nsp < LANES:
        bias_t = jnp.concatenate([bias_t, jnp.full((LANES - nsp, tq), NEG, F32)], axis=0)
    sel_bias = stack(bias_t.T).astype(BF16)

    lhs_sel = jnp.concatenate([qrb, sel_bias], axis=1)

    def sel_step(ci, carry, diagonal):
        m_i, l_i, acc = carry
        k0 = pl.multiple_of(ci * kc_len, kc_len)
        keys = jnp.concatenate([ks_ref[pl.ds(k0, kc_len), :], eb_ref[pl.ds(k0, kc_len), :]], axis=1)
        s = _dot_nt(lhs_sel, keys)
        if diagonal:
            kpos = k0 + lax.broadcasted_iota(jnp.int32, (m_rows, kc_len), 1)
            s = jnp.where(kpos <= tq_col, s, NEG)
        m_new = jnp.maximum(m_i, jnp.max(s, axis=-1, keepdims=True))
        alpha = jnp.exp(m_i - m_new)
        p = jnp.exp(s - m_new)
        l_new = alpha * l_i + jnp.sum(p, axis=-1, keepdims=True)
        acc_new = alpha * acc + _dot(p.astype(BF16), vs_ref[pl.ds(k0, kc_len), :])
        return m_new, l_new, acc_new

    n_kc = (t0 + tq + kc_len - 1) // kc_len
    init = (jnp.full((m_rows, 1), NEG, F32), jnp.zeros((m_rows, 1), F32),
            jnp.zeros((m_rows, dh), F32))
    carry = lax.fori_loop(0, n_kc - 1, functools.partial(sel_step, diagonal=False), init)
    _, l_s, acc_s = sel_step(n_kc - 1, carry, True)
    o_s = acc_s / jnp.maximum(l_s, 1e-30)

    wk = WINDOW + tq
    start = pl.multiple_of(jnp.maximum(t0 - WINDOW, 0), LANES)
    case = jnp.minimum(n, WINDOW // tq)
    row_id = lax.broadcasted_iota(jnp.int32, (m_rows, LANES), 0) & (tq - 1)
    lane_id = lax.broadcasted_iota(jnp.int32, (m_rows, LANES), 1)
    lhs_w = jnp.concatenate([qrb, jnp.where(row_id == lane_id, 1.0, 0.0).astype(BF16)], axis=1)
    keys_w = jnp.concatenate([kw_ref[pl.ds(start, wk), :], wb_ref[case]], axis=1)
    s_w = _dot_nt(lhs_w, keys_w)
    p_w = jnp.exp(s_w - jnp.max(s_w, axis=-1, keepdims=True))
    p_w = p_w / jnp.maximum(jnp.sum(p_w, axis=-1, keepdims=True), 1e-30)
    o_w = _dot(p_w.astype(BF16), vw_ref[pl.ds(start, wk), :])

    gates = _sigmoid(gate_ref[...])
    for h in range(hpg):
        base = (g * hpg + h) * NSA_N_BRANCH
        rows = slice(h * tq, (h + 1) * tq)
        out = (_pick_lane(gates, base) * o_c[rows] + _pick_lane(gates, base + 1) * o_s[rows]
               + _pick_lane(gates, base + 2) * o_w[rows])
        o_ref[:, h * dh:(h + 1) * dh] = out.astype(o_ref.dtype)


def nsa_attention(q, gates, cos, sin_a, sin_b, k_cmp, v_cmp, kvr, tables, *, batch, seq, kc_len=512):
    overlap_t, block_onehot, window_bias = tables
    n = q.shape[0]
    dh = LANES
    tq = LANES
    gw = NSA_HPG * dh
    nq = seq // tq
    kc_len = min(kc_len, seq)
    rows = lambda width: pl.BlockSpec((tq, width), lambda b, g, i: (b * nq + i, 0))
    cmp_spec = pl.BlockSpec((1, 1, k_cmp.shape[2], dh), lambda b, g, i: (b, g, 0, 0))
    kv_spec = lambda t: pl.BlockSpec((seq, dh), lambda b, g, i, t=t: (b, t * NSA_GROUPS + g))
    const = lambda shape: pl.BlockSpec(shape, lambda b, g, i: (0,) * len(shape))
    return pl.pallas_call(
        functools.partial(_nsa_kernel, seq=seq, tq=tq, kc_len=kc_len),
        grid=(batch, NSA_GROUPS, nq),
        in_specs=[
            pl.BlockSpec((tq, gw), lambda b, g, i: (b * nq + i, g)),
            rows(LANES), rows(LANES), rows(LANES), rows(LANES),
            cmp_spec, cmp_spec, kv_spec(2), kv_spec(3), kv_spec(4), kv_spec(5),
            const(overlap_t.shape), const(block_onehot.shape), const(window_bias.shape),
        ],
        out_specs=pl.BlockSpec((tq, gw), lambda b, g, i: (b * nq + i, g)),
        out_shape=jax.ShapeDtypeStruct((n, NSA_HEADS * dh), BF16),
        compiler_params=_params(3),
    )(q, gates, cos, sin_a, sin_b, k_cmp, v_cmp, kvr, kvr, kvr, kvr,
      overlap_t, block_onehot, window_bias)


def _rope_tables(positions):
    half = ROPE_DIM // 2
    inv = ROPE_THETA ** (-jnp.arange(0, ROPE_DIM, 2, dtype=F32) / ROPE_DIM)
    ang = positions.astype(F32).reshape(-1, 1) * inv
    cos, sin = jnp.cos(ang), jnp.sin(ang)
    n = ang.shape[0]
    ones = jnp.ones((n, LANES - ROPE_DIM), F32)
    zeros_h = jnp.zeros((n, half), F32)
    zeros_t = jnp.zeros((n, LANES - ROPE_DIM), F32)
    cos_t = jnp.concatenate([cos, cos, ones], axis=1)
    sin_a = jnp.concatenate([-sin, zeros_h, zeros_t], axis=1)
    sin_b = jnp.concatenate([zeros_h, sin, zeros_t], axis=1)
    return cos_t, sin_a, sin_b


def _overlap_table(seq):
    n_cmp = (seq - CMP_BLOCK) // CMP_STRIDE + 1
    n_sel = seq // SEL_BLOCK
    c_idx = jnp.arange(n_cmp)
    j_idx = jnp.arange(n_sel)
    ov = jnp.clip(jnp.minimum(c_idx[:, None] * CMP_STRIDE + CMP_BLOCK, (j_idx[None, :] + 1) * SEL_BLOCK)
                  - jnp.maximum(c_idx[:, None] * CMP_STRIDE, j_idx[None, :] * SEL_BLOCK), 0)
    ov = ov.astype(F32) / CMP_STRIDE
    return jnp.pad(ov, ((0, LANES - n_cmp), (0, LANES - n_sel)))


def _nsa_tables(seq):
    tq = LANES
    key = jnp.arange(seq)
    block_onehot = (key[:, None] // SEL_BLOCK == jnp.arange(LANES)[None, :]).astype(BF16)
    x = jnp.arange(WINDOW + tq)[None, :, None]
    t = (jnp.arange(WINDOW // tq + 1) * tq)[:, None, None] + jnp.arange(tq)[None, None, :]
    window_bias = jnp.where((x <= t) & (x > t - WINDOW), 0.0, NEG).astype(BF16)
    return _overlap_table(seq).T, block_onehot, window_bias


def _split_pad(w, main):
    extra = w[:, main:]
    return w[:, :main].astype(BF16), jnp.pad(extra, ((0, 0), (0, LANES - extra.shape[1]))).astype(BF16)


def kernel(x, positions, attn_norm, ffn_norm, final_norm, gdn_w_in, gdn_conv, gdn_a_log, gdn_dt_bias,
           gdn_out_norm, gdn_w_out, kv_norm, nsa_w_kv, cmp_pos_k, cmp_w1_k, cmp_w2_k, cmp_pos_v, cmp_w1_v,
           cmp_w2_v, nsa_w_q, nsa_w_o, ffn_w_gate, ffn_w_up, ffn_w_down, moe_router, moe_w_gate, moe_w_up,
           moe_w_down):
    batch, seq, d = x.shape
    depth = attn_norm.shape[0]
    n_a = depth // 2
    h = x.reshape(batch * seq, d)
    gdn_width = gdn_w_out.shape[1]
    nsa_width = nsa_w_o.shape[1]
    shared = None
    for l in range(depth):
        if l < n_a:
            w_main, w_gates = _split_pad(gdn_w_in[l], 4 * gdn_width)
            proj, gates = rms_matmul(h, attn_norm[l], w_main, w_gates, out_dtype=BF16)
            o = gdn_core(proj, gates, gdn_conv[l], gdn_a_log[l], gdn_dt_bias[l], gdn_out_norm[l],
                         batch=batch, seq=seq)
            h = matmul_residual(o, gdn_w_out[l].astype(BF16), h)
        else:
            cos, sin_a, sin_b, k_cmp, v_cmp, kvr, tables = shared
            w_main, w_gates = _split_pad(nsa_w_q[l - n_a], nsa_width)
            q, gates = rms_matmul(h, attn_norm[l], w_main, w_gates)
            o = nsa_attention(q, gates, cos, sin_a, sin_b, k_cmp, v_cmp, kvr, tables,
                              batch=batch, seq=seq)
            h = matmul_residual(o, nsa_w_o[l - n_a].astype(BF16), h)
        final_gain = final_norm if l == depth - 1 else None
        if l % 2 == 0:
            i = l // 2
            h = ffn_residual(h, ffn_norm[l], None, ffn_w_gate[i][None].astype(BF16),
                             ffn_w_up[i][None].astype(BF16), ffn_w_down[i][None].astype(BF16), final_gain)
        else:
            i = l // 2
            comb = route_tokens(h, ffn_norm[l], moe_router[i])
            h = ffn_residual(h, ffn_norm[l], comb, moe_w_gate[i].astype(BF16), moe_w_up[i].astype(BF16),
                             moe_w_down[i].astype(BF16), final_gain)
        if l == n_a - 1:
            cos, sin_a, sin_b = _rope_tables(positions)
            kv, _ = rms_matmul(h, kv_norm, nsa_w_kv.astype(BF16), jnp.zeros((d, LANES), BF16))
            kvr = kv_rope(kv, cos, sin_a, sin_b)
            k_cmp, v_cmp = nsa_compress(kv, cmp_pos_k, cmp_pos_v, cmp_w1_k.astype(BF16),
                                        cmp_w1_v.astype(BF16), cmp_w2_k.astype(BF16),
                                        cmp_w2_v.astype(BF16), batch=batch, seq=seq)
            shared = (cos, sin_a, sin_b, k_cmp, v_cmp, kvr, _nsa_tables(seq))
    return h.reshape(batch, seq, d)
```

```python
import functools
import math

import jax
import jax.numpy as jnp
from jax import lax
from jax.experimental import pallas as pl
from jax.experimental.pallas import tpu as pltpu

F32 = jnp.float32
BF16 = jnp.bfloat16

NORM_EPS = 1e-6
LANES = 128
GDN_HEADS = 16
GDN_CONV = 4
GDN_CHUNK = 64
GDN_GROUP = 256
GDN_GROUPS_IN_FLIGHT = 4
GDN_HEADS_PER_STEP = 4
NSA_HEADS = 16
NSA_GROUPS = 4
NSA_HPG = NSA_HEADS // NSA_GROUPS
NSA_N_BRANCH = 3
CMP_BLOCK = 32
CMP_STRIDE = 16
SEL_BLOCK = 64
SEL_TOPK = 16
WINDOW = 512
BIG = 1e9
NEG = -1e30
ROPE_THETA = 500000.0
ROPE_DIM = 32
TOP_K = 2

VMEM_LIMIT_BYTES = 56 * 1024 * 1024


def _params(n_axes):
    return pltpu.CompilerParams(dimension_semantics=("arbitrary",) * n_axes,
                                vmem_limit_bytes=VMEM_LIMIT_BYTES)


def _rms(x, gain):
    ms = jnp.mean(x * x, axis=-1, keepdims=True)
    return x * lax.rsqrt(ms + NORM_EPS) * gain


def _sigmoid(x):
    return 1.0 / (1.0 + jnp.exp(-x))


def _dot(a, b):
    return jnp.dot(a, b, preferred_element_type=F32)


def _dot_nt(a, b):
    return lax.dot_general(a, b, (((1,), (1,)), ((), ())), preferred_element_type=F32)


def _dot_tn(a, b):
    return lax.dot_general(a, b, (((0,), (0,)), ((), ())), preferred_element_type=F32)


def _pick_lane(x, idx):
    lane = lax.broadcasted_iota(jnp.int32, x.shape, 1)
    return jnp.sum(jnp.where(lane == idx, x, 0.0), axis=-1, keepdims=True)


def _rms_matmul_kernel(h_ref, g_ref, w_ref, we_ref, o_ref, oe_ref, xn_ref, *, row_chunk):
    j = pl.program_id(1)

    @pl.when(j == 0)
    def _():
        for r in range(0, h_ref.shape[0], row_chunk):
            xn_ref[r:r + row_chunk, :] = _rms(h_ref[r:r + row_chunk, :], g_ref[...]).astype(BF16)
        oe_ref[...] = _dot(xn_ref[...], we_ref[...])

    o_ref[...] = _dot(xn_ref[...], w_ref[...]).astype(o_ref.dtype)


def rms_matmul(h, gain, w, w_extra, *, tm=1024, tn=1024, out_dtype=F32):
    n, d = h.shape
    nm = w.shape[1]
    tm = min(tm, n)
    tn = min(tn, nm)
    return pl.pallas_call(
        functools.partial(_rms_matmul_kernel, row_chunk=256),
        grid=(n // tm, nm // tn),
        in_specs=[
            pl.BlockSpec((tm, d), lambda i, j: (i, 0)),
            pl.BlockSpec((1, d), lambda i, j: (0, 0)),
            pl.BlockSpec((d, tn), lambda i, j: (0, j)),
            pl.BlockSpec((d, LANES), lambda i, j: (0, 0)),
        ],
        out_specs=[
            pl.BlockSpec((tm, tn), lambda i, j: (i, j)),
            pl.BlockSpec((tm, LANES), lambda i, j: (i, 0)),
        ],
        out_shape=[jax.ShapeDtypeStruct((n, nm), out_dtype),
                   jax.ShapeDtypeStruct((n, LANES), F32)],
        scratch_shapes=[pltpu.VMEM((tm, d), BF16)],
        compiler_params=_params(2),
    )(h, gain.reshape(1, d), w, w_extra)


def _matmul_res_kernel(x_ref, w_ref, r_ref, o_ref):
    o_ref[...] = r_ref[...] + _dot(x_ref[...], w_ref[...])


def matmul_residual(x, w, res, *, tm=1024, tn=512):
    n, k = x.shape
    nm = w.shape[1]
    tm = min(tm, n)
    return pl.pallas_call(
        _matmul_res_kernel,
        grid=(n // tm, nm // tn),
        in_specs=[
            pl.BlockSpec((tm, k), lambda i, j: (i, 0)),
            pl.BlockSpec((k, tn), lambda i, j: (0, j)),
            pl.BlockSpec((tm, tn), lambda i, j: (i, j)),
        ],
        out_specs=pl.BlockSpec((tm, tn), lambda i, j: (i, j)),
        out_shape=jax.ShapeDtypeStruct((n, nm), F32),
        compiler_params=_params(2),
    )(x, w, res)


def _router_kernel(h_ref, g_ref, r_ref, o_ref, *, n_experts, row_chunk):
    for r in range(0, h_ref.shape[0], row_chunk):
        xn = _rms(h_ref[r:r + row_chunk, :], g_ref[...])
        logits = jnp.dot(xn, r_ref[...], preferred_element_type=F32,
                         precision=lax.Precision.HIGHEST)
        lane = lax.broadcasted_iota(jnp.int32, logits.shape, 1)
        valid = lane < n_experts
        z = jnp.where(valid, logits, NEG)
        z = z - jnp.max(z, axis=-1, keepdims=True)
        p = jnp.where(valid, jnp.exp(z), 0.0)
        p = p / jnp.sum(p, axis=-1, keepdims=True)
        p = jnp.where(valid, p, -1.0)
        p1 = jnp.max(p, axis=-1, keepdims=True)
        i1 = jnp.min(jnp.where(p == p1, lane, LANES), axis=-1, keepdims=True)
        pm = jnp.where(lane == i1, -1.0, p)
        p2 = jnp.max(pm, axis=-1, keepdims=True)
        i2 = jnp.min(jnp.where(pm == p2, lane, LANES), axis=-1, keepdims=True)
        den = p1 + p2
        o_ref[r:r + row_chunk, :] = (jnp.where(lane == i1, p1 / den, 0.0)
                                     + jnp.where(lane == i2, p2 / den, 0.0))


def route_tokens(h, gain, router, *, tm=1024):
    n, d = h.shape
    n_experts = router.shape[1]
    tm = min(tm, n)
    r_pad = jnp.pad(router, ((0, 0), (0, LANES - n_experts)))
    return pl.pallas_call(
        functools.partial(_router_kernel, n_experts=n_experts, row_chunk=256),
        grid=(n // tm,),
        in_specs=[
            pl.BlockSpec((tm, d), lambda i: (i, 0)),
            pl.BlockSpec((1, d), lambda i: (0, 0)),
            pl.BlockSpec((d, LANES), lambda i: (0, 0)),
        ],
        out_specs=pl.BlockSpec((tm, LANES), lambda i: (i, 0)),
        out_shape=jax.ShapeDtypeStruct((n, LANES), F32),
        compiler_params=_params(1),
    )(h, gain.reshape(1, d), r_pad)


def _ffn_kernel(h_ref, g_ref, c_ref, wg_ref, wu_ref, wd_ref, fg_ref, o_ref, xn_ref, *,
                n_e, n_f, use_comb, final_norm, row_chunk):
    e = pl.program_id(1)
    f = pl.program_id(2)

    @pl.when((e == 0) & (f == 0))
    def _():
        for r in range(0, h_ref.shape[0], row_chunk):
            x = h_ref[r:r + row_chunk, :]
            xn_ref[r:r + row_chunk, :] = _rms(x, g_ref[...]).astype(BF16)
            o_ref[r:r + row_chunk, :] = x

    xn = xn_ref[...]
    gate = _dot(xn, wg_ref[0])
    up = _dot(xn, wu_ref[0])
    act = gate * _sigmoid(gate) * up
    if use_comb:
        act = act * _pick_lane(c_ref[...], e)
    o_ref[...] += _dot(act.astype(BF16), wd_ref[0])

    if final_norm:
        @pl.when((e == n_e - 1) & (f == n_f - 1))
        def _():
            for r in range(0, h_ref.shape[0], row_chunk):
                o_ref[r:r + row_chunk, :] = _rms(o_ref[r:r + row_chunk, :], fg_ref[...])


def ffn_residual(h, gain, comb, w_gate, w_up, w_down, final_gain, *, tm=512, tf=512):
    n, d = h.shape
    n_e, _, ff = w_gate.shape
    tm = min(tm, n)
    use_comb = comb is not None
    final_norm = final_gain is not None
    if comb is None:
        comb = jnp.ones((n, LANES), F32)
    if final_gain is None:
        final_gain = jnp.ones((d,), F32)
    n_f = ff // tf
    return pl.pallas_call(
        functools.partial(_ffn_kernel, n_e=n_e, n_f=n_f, use_comb=use_comb,
                          final_norm=final_norm, row_chunk=256),
        grid=(n // tm, n_e, n_f),
        in_specs=[
            pl.BlockSpec((tm, d), lambda i, e, f: (i, 0)),
            pl.BlockSpec((1, d), lambda i, e, f: (0, 0)),
            pl.BlockSpec((tm, LANES), lambda i, e, f: (i, 0)),
            pl.BlockSpec((1, d, tf), lambda i, e, f: (e, 0, f)),
            pl.BlockSpec((1, d, tf), lambda i, e, f: (e, 0, f)),
            pl.BlockSpec((1, tf, d), lambda i, e, f: (e, f, 0)),
            pl.BlockSpec((1, d), lambda i, e, f: (0, 0)),
        ],
        out_specs=pl.BlockSpec((tm, d), lambda i, e, f: (i, 0)),
        out_shape=jax.ShapeDtypeStruct((n, d), F32),
        scratch_shapes=[pltpu.VMEM((tm, d), BF16)],
        compiler_params=_params(3),
    )(h, gain.reshape(1, d), comb, w_gate, w_up, w_down, final_gain.reshape(1, d))


def _moe_kernel(h_ref, g_ref, c_ref, ct_ref, wg_ref, wu_ref, wd_ref, fg_ref, o_ref,
                xn_ref, rc_ref, rr_ref, xs_ref, ye_ref, *, n_e, n_f, cap, final_norm, row_chunk):
    e = pl.program_id(1)
    f = pl.program_id(2)
    tm, d = h_ref.shape
    col_chunk = 512

    @pl.when((e == 0) & (f == 0))
    def _():
        for r in range(0, tm, row_chunk):
            x = h_ref[r:r + row_chunk, :]
            xn_ref[r:r + row_chunk, :] = _rms(x, g_ref[...]).astype(BF16)
            o_ref[r:r + row_chunk, :] = x
        ti = lax.broadcasted_iota(jnp.int32, (tm, tm), 0)
        tj = lax.broadcasted_iota(jnp.int32, (tm, tm), 1)
        routed_c = jnp.where(c_ref[...] > 0.0, 1.0, 0.0).astype(BF16)
        routed_r = jnp.where(ct_ref[...] > 0.0, 1.0, 0.0).astype(BF16)
        rc_ref[...] = _dot(jnp.where(tj < ti, 1.0, 0.0).astype(BF16), routed_c)
        rr_ref[...] = _dot(routed_r, jnp.where(ti < tj, 1.0, 0.0).astype(BF16))

    key_row = jnp.where(ct_ref[pl.ds(e, 1), :] > 0.0, rr_ref[pl.ds(e, 1), :], -1.0)
    w_row = ct_ref[pl.ds(e, 1), :]
    n_routed = jnp.sum(jnp.where(key_row >= 0.0, 1.0, 0.0)).astype(jnp.int32)
    n_rounds = (n_routed + cap - 1) // cap

    def gather_mask(rnd):
        slot = lax.broadcasted_iota(jnp.int32, (cap, tm), 0) + rnd * cap
        return key_row == slot.astype(F32)

    def expert(xs):
        gate = _dot(xs, wg_ref[0])
        up = _dot(xs, wu_ref[0])
        return _dot((gate * _sigmoid(gate) * up).astype(BF16), wd_ref[0])

    def scatter_add(rnd, y):
        picked = gather_mask(rnd)
        w_slot = jnp.sum(jnp.where(picked, w_row, 0.0), axis=-1, keepdims=True)
        yw = (y * w_slot).astype(BF16)
        key_col = jnp.where(_pick_lane(c_ref[...], e) > 0.0, _pick_lane(rc_ref[...], e), -1.0)
        slot = lax.broadcasted_iota(jnp.int32, (tm, cap), 1) + rnd * cap
        back = jnp.where(key_col == slot.astype(F32), 1.0, 0.0).astype(BF16)
        for c0 in range(0, d, col_chunk):
            o_ref[:, c0:c0 + col_chunk] += _dot(back, yw[:, c0:c0 + col_chunk])

    @pl.when(f == 0)
    def _():
        xs_ref[...] = _dot(jnp.where(gather_mask(0), 1.0, 0.0).astype(BF16), xn_ref[...]).astype(BF16)

    part = expert(xs_ref[...])

    @pl.when(f == 0)
    def _():
        ye_ref[...] = part

    @pl.when(f > 0)
    def _():
        ye_ref[...] += part

    @pl.when(f == n_f - 1)
    def _():
        scatter_add(0, ye_ref[...])

    def extra_round(rnd, carry):
        xs = _dot(jnp.where(gather_mask(rnd), 1.0, 0.0).astype(BF16), xn_ref[...]).astype(BF16)
        scatter_add(rnd, expert(xs))
        return carry

    lax.fori_loop(1, n_rounds, extra_round, 0)

    if final_norm:
        @pl.when((e == n_e - 1) & (f == n_f - 1))
        def _():
            for r in range(0, tm, row_chunk):
                o_ref[r:r + row_chunk, :] = _rms(o_ref[r:r + row_chunk, :], fg_ref[...])


def moe_residual(h, gain, comb, w_gate, w_up, w_down, final_gain, *, tm=1024, tf=512, cap=320):
    n, d = h.shape
    n_e, _, ff = w_gate.shape
    tm = min(tm, n)
    final_norm = final_gain is not None
    if final_gain is None:
        final_gain = jnp.ones((d,), F32)
    n_f = ff // tf
    once = pl.Buffered(1)
    return pl.pallas_call(
        functools.partial(_moe_kernel, n_e=n_e, n_f=n_f, cap=cap, final_norm=final_norm,
                          row_chunk=256),
        grid=(n // tm, n_e, n_f),
        in_specs=[
            pl.BlockSpec((tm, d), lambda i, e, f: (i, 0), pipeline_mode=once),
            pl.BlockSpec((1, d), lambda i, e, f: (0, 0)),
            pl.BlockSpec((tm, LANES), lambda i, e, f: (i, 0), pipeline_mode=once),
            pl.BlockSpec((LANES, tm), lambda i, e, f: (0, i), pipeline_mode=once),
            pl.BlockSpec((1, d, tf), lambda i, e, f: (e, 0, f)),
            pl.BlockSpec((1, d, tf), lambda i, e, f: (e, 0, f)),
            pl.BlockSpec((1, tf, d), lambda i, e, f: (e, f, 0)),
            pl.BlockSpec((1, d), lambda i, e, f: (0, 0)),
        ],
        out_specs=pl.BlockSpec((tm, d), lambda i, e, f: (i, 0)),
        out_shape=jax.ShapeDtypeStruct((n, d), F32),
        scratch_shapes=[pltpu.VMEM((tm, d), BF16), pltpu.VMEM((tm, LANES), F32),
                        pltpu.VMEM((LANES, tm), F32), pltpu.VMEM((cap, d), BF16),
                        pltpu.VMEM((cap, d), F32)],
        compiler_params=_params(3),
    )(h, gain.reshape(1, d), comb, comb.T, w_gate, w_up, w_down, final_gain.reshape(1, d))


def _tri_inv_many(lows, nilpotency):
    c = lows[0].shape[0]
    ii = lax.broadcasted_iota(jnp.int32, (c, c), 0)
    jj = lax.broadcasted_iota(jnp.int32, (c, c), 1)
    eye = jnp.where(ii == jj, 1.0, 0.0)
    lb = [low.astype(BF16) for low in lows]
    ps = [_dot(b, b).astype(BF16) for b in lb]
    ts = [eye - low for low in lows]
    levels = int(math.log2(nilpotency)) - 1
    for level in range(levels):
        tb = [t.astype(BF16) for t in ts]
        if level + 1 < levels:
            nxt = [_dot(p, p).astype(BF16) for p in ps]
        ts = [t + _dot(b, p) for t, b, p in zip(ts, tb, ps)]
        if level + 1 < levels:
            ps = nxt
    return ts


def _gdn_local_kernel(alog_ref, dtb_ref, q_ref, k_ref, v_ref, cq_ref, ck_ref, cv_ref,
                      a1_ref, a2_ref, b2_ref,
                      u_ref, w_ref, qd_ref, kd_ref, aqk_ref, gl_ref,
                      xp_ref, qs_ref, ks_ref, vs_ref, gc1_ref, gc2_ref, gl2_ref, be2_ref, *,
                      seq, row_chunk):
    hd = pl.program_id(1)
    dk = q_ref.shape[1]
    c = GDN_CHUNK
    grp = GDN_GROUP
    n_groups = seq // grp
    pad = 8

    def conv_silu(x_ref, w_ref, dst_ref, normalise, scale):
        xp_ref[0:pad, :] = jnp.zeros((pad, dk), F32)
        xp_ref[pad:pad + seq, :] = x_ref[...].astype(F32)
        w = w_ref[...]
        for r in range(0, seq, row_chunk):
            y = xp_ref[pad - 3 + r:pad - 3 + r + row_chunk, :] * w[0:1, :]
            for j in range(1, GDN_CONV):
                y = y + xp_ref[pad - 3 + j + r:pad - 3 + j + r + row_chunk, :] * w[j:j + 1, :]
            y = y * _sigmoid(y)
            if normalise:
                y = y * lax.rsqrt(jnp.sum(y * y, axis=-1, keepdims=True) + NORM_EPS)
            if scale != 1.0:
                y = y * scale
            dst_ref[r:r + row_chunk, :] = y

    conv_silu(q_ref, cq_ref, qs_ref, True, dk ** -0.5)
    conv_silu(k_ref, ck_ref, ks_ref, True, 1.0)
    conv_silu(v_ref, cv_ref, vs_ref, False, 1.0)

    a_coef = -jnp.exp(alog_ref[hd])
    dtb = dtb_ref[hd]

    def log_decay(a):
        x = a + dtb
        return a_coef * (jnp.maximum(x, 0.0) + jnp.log(1.0 + jnp.exp(-jnp.abs(x))))

    ii = lax.broadcasted_iota(jnp.int32, (grp, grp), 0)
    jj = lax.broadcasted_iota(jnp.int32, (grp, grp), 1)
    same = (ii // c) == (jj // c)
    hi = lax.Precision.HIGHEST
    g1 = log_decay(a1_ref[0, 0])
    g2 = log_decay(a2_ref[0, 0])
    ones_b = jnp.where(same, 1.0, 0.0)
    gc1_ref[...] = jnp.dot(g1, jnp.where(same & (ii <= jj), 1.0, 0.0),
                           preferred_element_type=F32, precision=hi)
    gc2_ref[...] = jnp.dot(jnp.where(same & (jj <= ii), 1.0, 0.0), g2,
                           preferred_element_type=F32, precision=hi)
    gl2_ref[...] = jnp.dot(ones_b, g2, preferred_element_type=F32, precision=hi)
    gl_ref[0, 0] = jnp.exp(jnp.dot(g1, ones_b, preferred_element_type=F32, precision=hi))
    be2_ref[...] = _sigmoid(b2_ref[0, 0])

    causal = same & (ii >= jj)
    strict = same & (ii > jj)
    n_par = GDN_GROUPS_IN_FLIGHT

    def groups(it, carry):
        gis = [it * n_par + t for t in range(n_par)]
        rows = [pl.ds(pl.multiple_of(gi * grp, grp), grp) for gi in gis]
        q = [qs_ref[r, :] for r in rows]
        k = [ks_ref[r, :] for r in rows]
        v = [vs_ref[r, :] for r in rows]
        g_row = [gc1_ref[pl.ds(gi, 1), :] for gi in gis]
        g_col = [_pick_lane(gc2_ref[...], gi) for gi in gis]
        gl_col = [_pick_lane(gl2_ref[...], gi) for gi in gis]
        b_col = [_pick_lane(be2_ref[...], gi) for gi in gis]
        kbf = [x.astype(BF16) for x in k]
        kb = [x * b for x, b in zip(k, b_col)]
        kk = [_dot_nt(x.astype(BF16), y) for x, y in zip(kb, kbf)]
        qk = [_dot_nt(x.astype(BF16), y) for x, y in zip(q, kbf)]
        decay = [jnp.where(causal, jnp.exp(jnp.where(causal, gc - gr, 0.0)), 0.0)
                 for gc, gr in zip(g_col, g_row)]
        t_inv = _tri_inv_many([jnp.where(strict, x * d, 0.0) for x, d in zip(kk, decay)], c)
        e_col = [jnp.exp(gc) for gc in g_col]
        rhs = [jnp.concatenate([x * b, y * e], axis=1).astype(BF16)
               for x, b, y, e in zip(v, b_col, kb, e_col)]
        uw = [_dot(t.astype(BF16), r) for t, r in zip(t_inv, rhs)]
        for t in range(n_par):
            u_ref[rows[t], :] = uw[t][:, :dk].astype(u_ref.dtype)
            w_ref[rows[t], :] = uw[t][:, dk:].astype(w_ref.dtype)
            a_qk = qk[t] * decay[t]
            a_c = a_qk[:, 0:c]
            for blk in range(1, grp // c):
                a_c = a_c + a_qk[:, blk * c:(blk + 1) * c]
            aqk_ref[0, 0, rows[t], :] = a_c.astype(aqk_ref.dtype)
            qd_ref[rows[t], :] = (q[t] * e_col[t]).astype(qd_ref.dtype)
            kd_ref[rows[t], :] = (k[t] * jnp.exp(gl_col[t] - g_col[t])).astype(kd_ref.dtype)
        return carry

    lax.fori_loop(0, n_groups // n_par, groups, 0)


def _gdn_scan_kernel(gl_ref, u_ref, w_ref, qd_ref, kd_ref, aqk_ref, z_ref, on_ref, o_ref, *,
                     seq, heads_per_step):
    b = pl.program_id(0)
    hb = pl.program_id(1)
    c = GDN_CHUNK
    dk = on_ref.shape[1]

    def step(ci, states):
        rows = pl.ds(pl.multiple_of(ci * c, c), c)
        hs = range(heads_per_step)
        cols = [slice(hh * dk, (hh + 1) * dk) for hh in hs]
        ws_qs = [_dot(jnp.concatenate([w_ref[rows, cols[hh]], qd_ref[rows, cols[hh]]], axis=0),
                      states[hh].astype(BF16)) for hh in hs]
        v_new = [(u_ref[rows, cols[hh]].astype(F32) - ws_qs[hh][:c]).astype(BF16) for hh in hs]
        intra = [_dot(aqk_ref[0, hh, rows, :], v_new[hh]) for hh in hs]
        outer = [_dot_tn(kd_ref[rows, cols[hh]], v_new[hh]) for hh in hs]
        new_states = []
        for hh in hs:
            decay = gl_ref[b, hb * heads_per_step + hh, ci]
            new_states.append(states[hh] * decay + outer[hh])
            z = z_ref[rows, cols[hh]].astype(F32)
            o = ws_qs[hh][c:] + intra[hh]
            o_ref[rows, cols[hh]] = (_rms(o, on_ref[...]) * (z * _sigmoid(z))).astype(o_ref.dtype)
        return tuple(new_states)

    lax.fori_loop(0, seq // c, step,
                  tuple(jnp.zeros((dk, dk), F32) for _ in range(heads_per_step)))


def gdn_core(proj, gates, conv_w, a_log, dt_bias, out_norm, *, batch, seq):
    n = proj.shape[0]
    nh = GDN_HEADS
    dk = proj.shape[1] // (4 * nh)
    c = GDN_CHUNK
    grp = GDN_GROUP
    n_groups = seq // grp
    hps = GDN_HEADS_PER_STEP
    bl = gates[:, :nh].reshape(batch, seq, nh).transpose(0, 2, 1)
    al = gates[:, nh:2 * nh].reshape(batch, seq, nh).transpose(0, 2, 1)
    a1 = al.reshape(batch, nh, n_groups, grp)
    a2 = a1.transpose(0, 1, 3, 2)
    b2 = bl.reshape(batch, nh, n_groups, grp).transpose(0, 1, 3, 2)
    smem = pl.BlockSpec(memory_space=pltpu.SMEM)
    col = lambda off: pl.BlockSpec((seq, dk), lambda b, h, off=off: (b, off * nh + h))
    cw = lambda off: pl.BlockSpec((GDN_CONV, dk), lambda b, h, off=off: (0, off * nh + h))
    g1 = pl.BlockSpec((1, 1, n_groups, grp), lambda b, h: (b, h, 0, 0))
    g2 = pl.BlockSpec((1, 1, grp, n_groups), lambda b, h: (b, h, 0, 0))
    head_out = pl.BlockSpec((seq, dk), lambda b, h: (b, h))
    wide = jax.ShapeDtypeStruct((n, nh * dk), BF16)
    u, w, qd, kd, aqk, gl = pl.pallas_call(
        functools.partial(_gdn_local_kernel, seq=seq, row_chunk=256),
        grid=(batch, nh),
        in_specs=[smem, smem, col(0), col(1), col(2), cw(0), cw(1), cw(2), g1, g2, g2],
        out_specs=[head_out, head_out, head_out, head_out,
                   pl.BlockSpec((1, 1, seq, c), lambda b, h: (b, h, 0, 0)), g1],
        out_shape=[wide, wide, wide, wide,
                   jax.ShapeDtypeStruct((batch, nh, seq, c), BF16),
                   jax.ShapeDtypeStruct((batch, nh, n_groups, grp), F32)],
        scratch_shapes=[
            pltpu.VMEM((seq + 8, dk), F32),
            pltpu.VMEM((seq, dk), F32), pltpu.VMEM((seq, dk), F32), pltpu.VMEM((seq, dk), F32),
            pltpu.VMEM((n_groups, grp), F32), pltpu.VMEM((grp, n_groups), F32),
            pltpu.VMEM((grp, n_groups), F32), pltpu.VMEM((grp, n_groups), F32),
        ],
        compiler_params=_params(2),
    )(a_log, dt_bias, proj, proj, proj, conv_w, conv_w, conv_w, a1, a2, b2)
    chunk_decay = gl[..., c - 1::c].reshape(batch, nh, seq // c)
    heads = pl.BlockSpec((seq, hps * dk), lambda b, h: (b, h))
    return pl.pallas_call(
        functools.partial(_gdn_scan_kernel, seq=seq, heads_per_step=hps),
        grid=(batch, nh // hps),
        in_specs=[smem, heads, heads, heads, heads,
                  pl.BlockSpec((1, hps, seq, c), lambda b, h: (b, h, 0, 0)),
                  pl.BlockSpec((seq, hps * dk), lambda b, h: (b, 3 * (nh // hps) + h)),
                  pl.BlockSpec((1, dk), lambda b, h: (0, 0))],
        out_specs=heads,
        out_shape=wide,
        compiler_params=_params(2),
    )(chunk_decay, u, w, qd, kd, aqk, proj, out_norm.reshape(1, dk))


def _rope(x, cos, sin_a, sin_b):
    half = ROPE_DIM // 2
    return (x * cos + pltpu.roll(x, LANES - half, 1) * sin_a + pltpu.roll(x, half, 1) * sin_b)


def _kv_rope_kernel(x_ref, cos_ref, sa_ref, sb_ref, o_ref):
    j = pl.program_id(1)
    tensor = j // NSA_GROUPS
    x = x_ref[...]
    roped = _rope(x, cos_ref[...], sa_ref[...], sb_ref[...])
    o_ref[...] = jnp.where((tensor == 2) | (tensor == 4), roped, x).astype(o_ref.dtype)


def kv_rope(kv, cos, sin_a, sin_b, *, tm=1024):
    n, width = kv.shape
    tm = min(tm, n)
    tab = pl.BlockSpec((tm, LANES), lambda i, j: (i, 0))
    return pl.pallas_call(
        _kv_rope_kernel,
        grid=(n // tm, width // LANES),
        in_specs=[pl.BlockSpec((tm, LANES), lambda i, j: (i, j)), tab, tab, tab],
        out_specs=pl.BlockSpec((tm, LANES), lambda i, j: (i, j)),
        out_shape=jax.ShapeDtypeStruct((n, width), BF16),
        compiler_params=_params(2),
    )(kv, cos, sin_a, sin_b)


def _compress_kernel(kc_ref, vc_ref, pk_ref, pv_ref, w1k_ref, w1v_ref, w2k_ref, w2v_ref,
                     ok_ref, ov_ref, tp_ref, *, seq):
    dh = kc_ref.shape[1]
    n_rows = ok_ref.shape[2]
    hidden = w1k_ref.shape[1]

    def run(t_ref, pos_ref, w1_ref, w2_ref, out_ref):
        tp_ref[0:seq, :] = t_ref[...]
        tp_ref[seq:seq + CMP_STRIDE, :] = jnp.zeros((CMP_STRIDE, dh), F32)
        acc = jnp.zeros((n_rows, hidden), F32)
        for l in range(CMP_BLOCK):
            x = tp_ref[pl.ds(l, n_rows, stride=CMP_STRIDE), :] + pos_ref[l:l + 1, :]
            acc = acc + _dot(x.astype(BF16), w1_ref[l * dh:(l + 1) * dh, :])
        hid = acc * _sigmoid(acc)
        out_ref[0, 0] = _dot(hid.astype(BF16), w2_ref[...]).astype(out_ref.dtype)

    run(kc_ref, pk_ref, w1k_ref, w2k_ref, ok_ref)
    run(vc_ref, pv_ref, w1v_ref, w2v_ref, ov_ref)


def nsa_compress(kv, pos_k, pos_v, w1k, w1v, w2k, w2v, *, batch, seq):
    dh = pos_k.shape[1]
    n_rows = seq // CMP_STRIDE
    hidden = w1k.shape[1]
    full = lambda shape: pl.BlockSpec(shape, lambda b, g: (0,) * len(shape))
    out = pl.BlockSpec((1, 1, n_rows, dh), lambda b, g: (b, g, 0, 0))
    return pl.pallas_call(
        functools.partial(_compress_kernel, seq=seq),
        grid=(batch, NSA_GROUPS),
        in_specs=[
            pl.BlockSpec((seq, dh), lambda b, g: (b, g)),
            pl.BlockSpec((seq, dh), lambda b, g: (b, NSA_GROUPS + g)),
            full((CMP_BLOCK, dh)), full((CMP_BLOCK, dh)),
            full((CMP_BLOCK * dh, hidden)), full((CMP_BLOCK * dh, hidden)),
            full((hidden, dh)), full((hidden, dh)),
        ],
        out_specs=[out, out],
        out_shape=[jax.ShapeDtypeStruct((batch, NSA_GROUPS, n_rows, dh), BF16)] * 2,
        scratch_shapes=[pltpu.VMEM((seq + CMP_STRIDE, dh), F32)],
        compiler_params=_params(2),
    )(kv, kv, pos_k, pos_v, w1k, w1v, w2k, w2v)


def _masked_softmax(s, mask):
    sm = jnp.where(mask, s, NEG)
    m = jnp.max(sm, axis=-1, keepdims=True)
    p = jnp.where(mask, jnp.exp(sm - m), 0.0)
    return p / jnp.maximum(jnp.sum(p, axis=-1, keepdims=True), 1e-30)


def _nsa_kernel(q_ref, gate_ref, cos_ref, sa_ref, sb_ref, kc_ref, vc_ref, ks_ref, vs_ref,
                kw_ref, vw_ref, ovt_ref, eb_ref, wb_ref, o_ref, *, seq, tq, kc_len):
    g = pl.program_id(1)
    n = pl.program_id(2)
    t0 = n * tq
    dh = LANES
    hpg = NSA_HPG
    m_rows = hpg * tq
    n_sel = seq // SEL_BLOCK
    topk = min(SEL_TOPK, n_sel)
    scale = dh ** -0.5

    stack = lambda x: jnp.concatenate([x] * hpg, axis=0)
    q = q_ref[...]
    qs = jnp.concatenate([q[:, h * dh:(h + 1) * dh] for h in range(hpg)], axis=0) * scale
    qr = _rope(qs, stack(cos_ref[...]), stack(sa_ref[...]), stack(sb_ref[...]))
    qb = qs.astype(BF16)
    qrb = qr.astype(BF16)
    row = lax.broadcasted_iota(jnp.int32, (m_rows, 1), 0)
    tq_col = t0 + (row & (tq - 1))

    s_c = _dot_nt(qb, kc_ref[0, 0])
    lane_c = lax.broadcasted_iota(jnp.int32, s_c.shape, 1)
    p_c = _masked_softmax(s_c, lane_c * CMP_STRIDE + (CMP_BLOCK - 1) <= tq_col)
    o_c = _dot(p_c.astype(BF16), vc_ref[0, 0])

    p_sum = p_c[0:tq]
    for h in range(1, hpg):
        p_sum = p_sum + p_c[h * tq:(h + 1) * tq]
    imp_t = lax.dot_general(ovt_ref[...], p_sum, (((1,), (1,)), ((), ())),
                            preferred_element_type=F32, precision=lax.Precision.HIGHEST)
    nsp = -(-n_sel // 8) * 8
    imp_t = imp_t[0:nsp]
    blk = lax.broadcasted_iota(jnp.int32, (nsp, tq), 0)
    t1 = t0 + lax.broadcasted_iota(jnp.int32, (nsp, tq), 1)
    cur = t1 // SEL_BLOCK
    forced = (blk == 0) | (blk == cur) | (blk == cur - 1)
    visible = blk * SEL_BLOCK <= t1
    score = jnp.where(forced, BIG, jnp.where(visible, imp_t, -BIG))
    score = jnp.where(blk < n_sel, score, -3e38)
    rank = jnp.zeros((nsp, tq), F32)
    for j2 in range(n_sel):
        other = score[j2:j2 + 1, :]
        rank = rank + jnp.where(other > score, 1.0,
                                jnp.where((other == score) & (blk > j2), 1.0, 0.0))
    chosen = (rank < topk) & visible & (blk < n_sel)
    bias_t = jnp.where(chosen, 0.0, NEG)
    if nsp < LANES:
        bias_t = jnp.concatenate([bias_t, jnp.full((LANES - nsp, tq), NEG, F32)], axis=0)
    sel_bias = stack(bias_t.T).astype(BF16)

    lhs_sel = jnp.concatenate([qrb, sel_bias], axis=1)

    def sel_step(ci, carry, diagonal):
        m_i, l_i, acc = carry
        k0 = pl.multiple_of(ci * kc_len, kc_len)
        keys = jnp.concatenate([ks_ref[pl.ds(k0, kc_len), :], eb_ref[pl.ds(k0, kc_len), :]], axis=1)
        s = _dot_nt(lhs_sel, keys)
        if diagonal:
            kpos = k0 + lax.broadcasted_iota(jnp.int32, (m_rows, kc_len), 1)
            s = jnp.where(kpos <= tq_col, s, NEG)
        m_new = jnp.maximum(m_i, jnp.max(s, axis=-1, keepdims=True))
        alpha = jnp.exp(m_i - m_new)
        p = jnp.exp(s - m_new)
        l_new = alpha * l_i + jnp.sum(p, axis=-1, keepdims=True)
        acc_new = alpha * acc + _dot(p.astype(BF16), vs_ref[pl.ds(k0, kc_len), :])
        return m_new, l_new, acc_new

    n_kc = (t0 + tq + kc_len - 1) // kc_len
    init = (jnp.full((m_rows, 1), NEG, F32), jnp.zeros((m_rows, 1), F32),
            jnp.zeros((m_rows, dh), F32))
    carry = lax.fori_loop(0, n_kc - 1, functools.partial(sel_step, diagonal=False), init)
    _, l_s, acc_s = sel_step(n_kc - 1, carry, True)
    o_s = acc_s / jnp.maximum(l_s, 1e-30)

    wk = WINDOW + tq
    start = pl.multiple_of(jnp.maximum(t0 - WINDOW, 0), LANES)
    case = jnp.minimum(n, WINDOW // tq)
    row_id = lax.broadcasted_iota(jnp.int32, (m_rows, LANES), 0) & (tq - 1)
    lane_id = lax.broadcasted_iota(jnp.int32, (m_rows, LANES), 1)
    lhs_w = jnp.concatenate([qrb, jnp.where(row_id == lane_id, 1.0, 0.0).astype(BF16)], axis=1)
    keys_w = jnp.concatenate([kw_ref[pl.ds(start, wk), :], wb_ref[case]], axis=1)
    s_w = _dot_nt(lhs_w, keys_w)
    p_w = jnp.exp(s_w - jnp.max(s_w, axis=-1, keepdims=True))
    p_w = p_w / jnp.maximum(jnp.sum(p_w, axis=-1, keepdims=True), 1e-30)
    o_w = _dot(p_w.astype(BF16), vw_ref[pl.ds(start, wk), :])

    gates = _sigmoid(gate_ref[...])
    for h in range(hpg):
        base = (g * hpg + h) * NSA_N_BRANCH
        rows = slice(h * tq, (h + 1) * tq)
        out = (_pick_lane(gates, base) * o_c[rows] + _pick_lane(gates, base + 1) * o_s[rows]
               + _pick_lane(gates, base + 2) * o_w[rows])
        o_ref[:, h * dh:(h + 1) * dh] = out.astype(o_ref.dtype)


def nsa_attention(q, gates, cos, sin_a, sin_b, k_cmp, v_cmp, kvr, tables, *, batch, seq, kc_len=512):
    overlap_t, block_onehot, window_bias = tables
    n = q.shape[0]
    dh = LANES
    tq = LANES
    gw = NSA_HPG * dh
    nq = seq // tq
    kc_len = min(kc_len, seq)
    rows = lambda width: pl.BlockSpec((tq, width), lambda b, g, i: (b * nq + i, 0))
    cmp_spec = pl.BlockSpec((1, 1, k_cmp.shape[2], dh), lambda b, g, i: (b, g, 0, 0))
    kv_spec = lambda t: pl.BlockSpec((seq, dh), lambda b, g, i, t=t: (b, t * NSA_GROUPS + g))
    const = lambda shape: pl.BlockSpec(shape, lambda b, g, i: (0,) * len(shape))
    return pl.pallas_call(
        functools.partial(_nsa_kernel, seq=seq, tq=tq, kc_len=kc_len),
        grid=(batch, NSA_GROUPS, nq),
        in_specs=[
            pl.BlockSpec((tq, gw), lambda b, g, i: (b * nq + i, g)),
            rows(LANES), rows(LANES), rows(LANES), rows(LANES),
            cmp_spec, cmp_spec, kv_spec(2), kv_spec(3), kv_spec(4), kv_spec(5),
            const(overlap_t.shape), const(block_onehot.shape), const(window_bias.shape),
        ],
        out_specs=pl.BlockSpec((tq, gw), lambda b, g, i: (b * nq + i, g)),
        out_shape=jax.ShapeDtypeStruct((n, NSA_HEADS * dh), BF16),
        compiler_params=_params(3),
    )(q, gates, cos, sin_a, sin_b, k_cmp, v_cmp, kvr, kvr, kvr, kvr,
      overlap_t, block_onehot, window_bias)


def _rope_tables(positions):
    half = ROPE_DIM // 2
    inv = ROPE_THETA ** (-jnp.arange(0, ROPE_DIM, 2, dtype=F32) / ROPE_DIM)
    ang = positions.astype(F32).reshape(-1, 1) * inv
    cos, sin = jnp.cos(ang), jnp.sin(ang)
    n = ang.shape[0]
    ones = jnp.ones((n, LANES - ROPE_DIM), F32)
    zeros_h = jnp.zeros((n, half), F32)
    zeros_t = jnp.zeros((n, LANES - ROPE_DIM), F32)
    cos_t = jnp.concatenate([cos, cos, ones], axis=1)
    sin_a = jnp.concatenate([-sin, zeros_h, zeros_t], axis=1)
    sin_b = jnp.concatenate([zeros_h, sin, zeros_t], axis=1)
    return cos_t, sin_a, sin_b


def _overlap_table(seq):
    n_cmp = (seq - CMP_BLOCK) // CMP_STRIDE + 1
    n_sel = seq // SEL_BLOCK
    c_idx = jnp.arange(n_cmp)
    j_idx = jnp.arange(n_sel)
    ov = jnp.clip(jnp.minimum(c_idx[:, None] * CMP_STRIDE + CMP_BLOCK, (j_idx[None, :] + 1) * SEL_BLOCK)
                  - jnp.maximum(c_idx[:, None] * CMP_STRIDE, j_idx[None, :] * SEL_BLOCK), 0)
    ov = ov.astype(F32) / CMP_STRIDE
    return jnp.pad(ov, ((0, LANES - n_cmp), (0, LANES - n_sel)))


def _nsa_tables(seq):
    tq = LANES
    key = jnp.arange(seq)
    block_onehot = (key[:, None] // SEL_BLOCK == jnp.arange(LANES)[None, :]).astype(BF16)
    x = jnp.arange(WINDOW + tq)[None, :, None]
    t = (jnp.arange(WINDOW // tq + 1) * tq)[:, None, None] + jnp.arange(tq)[None, None, :]
    window_bias = jnp.where((x <= t) & (x > t - WINDOW), 0.0, NEG).astype(BF16)
    return _overlap_table(seq).T, block_onehot, window_bias


def _split_pad(w, main):
    extra = w[:, main:]
    return w[:, :main].astype(BF16), jnp.pad(extra, ((0, 0), (0, LANES - extra.shape[1]))).astype(BF16)


def kernel(x, positions, attn_norm, ffn_norm, final_norm, gdn_w_in, gdn_conv, gdn_a_log, gdn_dt_bias,
           gdn_out_norm, gdn_w_out, kv_norm, nsa_w_kv, cmp_pos_k, cmp_w1_k, cmp_w2_k, cmp_pos_v, cmp_w1_v,
           cmp_w2_v, nsa_w_q, nsa_w_o, ffn_w_gate, ffn_w_up, ffn_w_down, moe_router, moe_w_gate, moe_w_up,
           moe_w_down):
    batch, seq, d = x.shape
    depth = attn_norm.shape[0]
    n_a = depth // 2
    h = x.reshape(batch * seq, d)
    gdn_width = gdn_w_out.shape[1]
    nsa_width = nsa_w_o.shape[1]
    shared = None
    for l in range(depth):
        if l < n_a:
            w_main, w_gates = _split_pad(gdn_w_in[l], 4 * gdn_width)
            proj, gates = rms_matmul(h, attn_norm[l], w_main, w_gates, out_dtype=BF16)
            o = gdn_core(proj, gates, gdn_conv[l], gdn_a_log[l], gdn_dt_bias[l], gdn_out_norm[l],
                         batch=batch, seq=seq)
            h = matmul_residual(o, gdn_w_out[l].astype(BF16), h)
        else:
            cos, sin_a, sin_b, k_cmp, v_cmp, kvr, tables = shared
            w_main, w_gates = _split_pad(nsa_w_q[l - n_a], nsa_width)
            q, gates = rms_matmul(h, attn_norm[l], w_main, w_gates)
            o = nsa_attention(q, gates, cos, sin_a, sin_b, k_cmp, v_cmp, kvr, tables,
                              batch=batch, seq=seq)
            h = matmul_residual(o, nsa_w_o[l - n_a].astype(BF16), h)
        final_gain = final_norm if l == depth - 1 else None
        if l % 2 == 0:
            i = l // 2
            h = ffn_residual(h, ffn_norm[l], None, ffn_w_gate[i][None].astype(BF16),
                             ffn_w_up[i][None].astype(BF16), ffn_w_down[i][None].astype(BF16), final_gain)
        else:
            i = l // 2
            comb = route_tokens(h, ffn_norm[l], moe_router[i])
            h = moe_residual(h, ffn_norm[l], comb, moe_w_gate[i].astype(BF16), moe_w_up[i].astype(BF16),
                             moe_w_down[i].astype(BF16), final_gain)
        if l == n_a - 1:
            cos, sin_a, sin_b = _rope_tables(positions)
            kv, _ = rms_matmul(h, kv_norm, nsa_w_kv.astype(BF16), jnp.zeros((d, LANES), BF16))
            kvr = kv_rope(kv, cos, sin_a, sin_b)
            k_cmp, v_cmp = nsa_compress(kv, cmp_pos_k, cmp_pos_v, cmp_w1_k.astype(BF16),
                                        cmp_w1_v.astype(BF16), cmp_w2_k.astype(BF16),
                                        cmp_w2_v.astype(BF16), batch=batch, seq=seq)
            shared = (cos, sin_a, sin_b, k_cmp, v_cmp, kvr, _nsa_tables(seq))
    return h.reshape(batch, seq, d)
```

```python
import functools
import math

import jax
import jax.numpy as jnp
from jax import lax
from jax.experimental import pallas as pl
from jax.experimental.pallas import tpu as pltpu

F32 = jnp.float32
BF16 = jnp.bfloat16

NORM_EPS = 1e-6
LANES = 128
GDN_HEADS = 16
GDN_CONV = 4
GDN_CHUNK = 64
GDN_GROUP = 256
GDN_GROUPS_IN_FLIGHT = 4
GDN_HEADS_PER_STEP = 4
NSA_HEADS = 16
NSA_GROUPS = 4
NSA_HPG = NSA_HEADS // NSA_GROUPS
NSA_N_BRANCH = 3
CMP_BLOCK = 32
CMP_STRIDE = 16
SEL_BLOCK = 64
SEL_TOPK = 16
WINDOW = 512
BIG = 1e9
NEG = -1e30
ROPE_THETA = 500000.0
ROPE_DIM = 32
TOP_K = 2

VMEM_LIMIT_BYTES = 56 * 1024 * 1024


def _params(n_axes):
    return pltpu.CompilerParams(dimension_semantics=("arbitrary",) * n_axes,
                                vmem_limit_bytes=VMEM_LIMIT_BYTES)


def _rms(x, gain):
    ms = jnp.mean(x * x, axis=-1, keepdims=True)
    return x * lax.rsqrt(ms + NORM_EPS) * gain


def _sigmoid(x):
    return 1.0 / (1.0 + jnp.exp(-x))


def _dot(a, b):
    return jnp.dot(a, b, preferred_element_type=F32)


def _dot_nt(a, b):
    return lax.dot_general(a, b, (((1,), (1,)), ((), ())), preferred_element_type=F32)


def _dot_tn(a, b):
    return lax.dot_general(a, b, (((0,), (0,)), ((), ())), preferred_element_type=F32)


def _pick_lane(x, idx):
    lane = lax.broadcasted_iota(jnp.int32, x.shape, 1)
    return jnp.sum(jnp.where(lane == idx, x, 0.0), axis=-1, keepdims=True)


def _rms_matmul_kernel(h_ref, g_ref, w_ref, we_ref, o_ref, oe_ref, xn_ref, *, row_chunk):
    j = pl.program_id(1)

    @pl.when(j == 0)
    def _():
        for r in range(0, h_ref.shape[0], row_chunk):
            xn_ref[r:r + row_chunk, :] = _rms(h_ref[r:r + row_chunk, :], g_ref[...]).astype(BF16)
        oe_ref[...] = _dot(xn_ref[...], we_ref[...])

    o_ref[...] = _dot(xn_ref[...], w_ref[...]).astype(o_ref.dtype)


def rms_matmul(h, gain, w, w_extra, *, tm=1024, tn=1024, out_dtype=F32):
    n, d = h.shape
    nm = w.shape[1]
    tm = min(tm, n)
    tn = min(tn, nm)
    return pl.pallas_call(
        functools.partial(_rms_matmul_kernel, row_chunk=256),
        grid=(n // tm, nm // tn),
        in_specs=[
            pl.BlockSpec((tm, d), lambda i, j: (i, 0)),
            pl.BlockSpec((1, d), lambda i, j: (0, 0)),
            pl.BlockSpec((d, tn), lambda i, j: (0, j)),
            pl.BlockSpec((d, LANES), lambda i, j: (0, 0)),
        ],
        out_specs=[
            pl.BlockSpec((tm, tn), lambda i, j: (i, j)),
            pl.BlockSpec((tm, LANES), lambda i, j: (i, 0)),
        ],
        out_shape=[jax.ShapeDtypeStruct((n, nm), out_dtype),
                   jax.ShapeDtypeStruct((n, LANES), F32)],
        scratch_shapes=[pltpu.VMEM((tm, d), BF16)],
        compiler_params=_params(2),
    )(h, gain.reshape(1, d), w, w_extra)


def _matmul_res_kernel(x_ref, w_ref, r_ref, o_ref):
    o_ref[...] = r_ref[...] + _dot(x_ref[...], w_ref[...])


def matmul_residual(x, w, res, *, tm=1024, tn=512):
    n, k = x.shape
    nm = w.shape[1]
    tm = min(tm, n)
    return pl.pallas_call(
        _matmul_res_kernel,
        grid=(n // tm, nm // tn),
        in_specs=[
            pl.BlockSpec((tm, k), lambda i, j: (i, 0)),
            pl.BlockSpec((k, tn), lambda i, j: (0, j)),
            pl.BlockSpec((tm, tn), lambda i, j: (i, j)),
        ],
        out_specs=pl.BlockSpec((tm, tn), lambda i, j: (i, j)),
        out_shape=jax.ShapeDtypeStruct((n, nm), F32),
        compiler_params=_params(2),
    )(x, w, res)


def _router_kernel(h_ref, g_ref, r_ref, o_ref, *, n_experts, row_chunk):
    for r in range(0, h_ref.shape[0], row_chunk):
        xn = _rms(h_ref[r:r + row_chunk, :], g_ref[...])
        logits = jnp.dot(xn, r_ref[...], preferred_element_type=F32,
                         precision=lax.Precision.HIGHEST)
        lane = lax.broadcasted_iota(jnp.int32, logits.shape, 1)
        valid = lane < n_experts
        z = jnp.where(valid, logits, NEG)
        z = z - jnp.max(z, axis=-1, keepdims=True)
        p = jnp.where(valid, jnp.exp(z), 0.0)
        p = p / jnp.sum(p, axis=-1, keepdims=True)
        p = jnp.where(valid, p, -1.0)
        p1 = jnp.max(p, axis=-1, keepdims=True)
        i1 = jnp.min(jnp.where(p == p1, lane, LANES), axis=-1, keepdims=True)
        pm = jnp.where(lane == i1, -1.0, p)
        p2 = jnp.max(pm, axis=-1, keepdims=True)
        i2 = jnp.min(jnp.where(pm == p2, lane, LANES), axis=-1, keepdims=True)
        den = p1 + p2
        o_ref[r:r + row_chunk, :] = (jnp.where(lane == i1, p1 / den, 0.0)
                                     + jnp.where(lane == i2, p2 / den, 0.0))


def route_tokens(h, gain, router, *, tm=1024):
    n, d = h.shape
    n_experts = router.shape[1]
    tm = min(tm, n)
    r_pad = jnp.pad(router, ((0, 0), (0, LANES - n_experts)))
    return pl.pallas_call(
        functools.partial(_router_kernel, n_experts=n_experts, row_chunk=256),
        grid=(n // tm,),
        in_specs=[
            pl.BlockSpec((tm, d), lambda i: (i, 0)),
            pl.BlockSpec((1, d), lambda i: (0, 0)),
            pl.BlockSpec((d, LANES), lambda i: (0, 0)),
        ],
        out_specs=pl.BlockSpec((tm, LANES), lambda i: (i, 0)),
        out_shape=jax.ShapeDtypeStruct((n, LANES), F32),
        compiler_params=_params(1),
    )(h, gain.reshape(1, d), r_pad)


def _ffn_kernel(h_ref, g_ref, c_ref, wg_ref, wu_ref, wd_ref, fg_ref, o_ref, xn_ref, *,
                n_e, n_f, use_comb, final_norm, row_chunk):
    e = pl.program_id(1)
    f = pl.program_id(2)

    @pl.when((e == 0) & (f == 0))
    def _():
        for r in range(0, h_ref.shape[0], row_chunk):
            x = h_ref[r:r + row_chunk, :]
            xn_ref[r:r + row_chunk, :] = _rms(x, g_ref[...]).astype(BF16)
            o_ref[r:r + row_chunk, :] = x

    xn = xn_ref[...]
    gate = _dot(xn, wg_ref[0, 0])
    up = _dot(xn, wu_ref[0, 0])
    act = gate * _sigmoid(gate) * up
    if use_comb:
        act = act * _pick_lane(c_ref[...], e)
    o_ref[...] += _dot(act.astype(BF16), wd_ref[0])

    if final_norm:
        @pl.when((e == n_e - 1) & (f == n_f - 1))
        def _():
            for r in range(0, h_ref.shape[0], row_chunk):
                o_ref[r:r + row_chunk, :] = _rms(o_ref[r:r + row_chunk, :], fg_ref[...])


def _tile_columns(w, tf):
    n_e, d, ff = w.shape
    return w.reshape(n_e, d, ff // tf, tf).transpose(0, 2, 1, 3).astype(BF16)


def ffn_residual(h, gain, comb, w_gate, w_up, w_down, final_gain, *, tm=512, tf=512):
    n, d = h.shape
    n_e, _, ff = w_gate.shape
    w_gate, w_up, w_down = _tile_columns(w_gate, tf), _tile_columns(w_up, tf), w_down.astype(BF16)
    tm = min(tm, n)
    use_comb = comb is not None
    final_norm = final_gain is not None
    if comb is None:
        comb = jnp.ones((n, LANES), F32)
    if final_gain is None:
        final_gain = jnp.ones((d,), F32)
    n_f = ff // tf
    return pl.pallas_call(
        functools.partial(_ffn_kernel, n_e=n_e, n_f=n_f, use_comb=use_comb,
                          final_norm=final_norm, row_chunk=256),
        grid=(n // tm, n_e, n_f),
        in_specs=[
            pl.BlockSpec((tm, d), lambda i, e, f: (i, 0)),
            pl.BlockSpec((1, d), lambda i, e, f: (0, 0)),
            pl.BlockSpec((tm, LANES), lambda i, e, f: (i, 0)),
            pl.BlockSpec((1, 1, d, tf), lambda i, e, f: (e, f, 0, 0)),
            pl.BlockSpec((1, 1, d, tf), lambda i, e, f: (e, f, 0, 0)),
            pl.BlockSpec((1, tf, d), lambda i, e, f: (e, f, 0)),
            pl.BlockSpec((1, d), lambda i, e, f: (0, 0)),
        ],
        out_specs=pl.BlockSpec((tm, d), lambda i, e, f: (i, 0)),
        out_shape=jax.ShapeDtypeStruct((n, d), F32),
        scratch_shapes=[pltpu.VMEM((tm, d), BF16)],
        compiler_params=_params(3),
    )(h, gain.reshape(1, d), comb, w_gate, w_up, w_down, final_gain.reshape(1, d))


def _moe_kernel(h_ref, g_ref, c_ref, ct_ref, wg_ref, wu_ref, wd_ref, fg_ref, o_ref,
                xn_ref, rc_ref, rr_ref, xs_ref, ye_ref, *, n_e, n_f, cap, final_norm, row_chunk):
    e = pl.program_id(1)
    f = pl.program_id(2)
    tm, d = h_ref.shape
    col_chunk = 512

    @pl.when((e == 0) & (f == 0))
    def _():
        for r in range(0, tm, row_chunk):
            x = h_ref[r:r + row_chunk, :]
            xn_ref[r:r + row_chunk, :] = _rms(x, g_ref[...]).astype(BF16)
            o_ref[r:r + row_chunk, :] = x
        ti = lax.broadcasted_iota(jnp.int32, (tm, tm), 0)
        tj = lax.broadcasted_iota(jnp.int32, (tm, tm), 1)
        routed_c = jnp.where(c_ref[...] > 0.0, 1.0, 0.0).astype(BF16)
        routed_r = jnp.where(ct_ref[...] > 0.0, 1.0, 0.0).astype(BF16)
        rc_ref[...] = _dot(jnp.where(tj < ti, 1.0, 0.0).astype(BF16), routed_c)
        rr_ref[...] = _dot(routed_r, jnp.where(ti < tj, 1.0, 0.0).astype(BF16))

    key_row = jnp.where(ct_ref[pl.ds(e, 1), :] > 0.0, rr_ref[pl.ds(e, 1), :], -1.0)
    w_row = ct_ref[pl.ds(e, 1), :]
    n_routed = jnp.sum(jnp.where(key_row >= 0.0, 1.0, 0.0)).astype(jnp.int32)
    n_rounds = (n_routed + cap - 1) // cap

    def gather_mask(rnd):
        slot = lax.broadcasted_iota(jnp.int32, (cap, tm), 0) + rnd * cap
        return key_row == slot.astype(F32)

    def expert(xs):
        gate = _dot(xs, wg_ref[0, 0])
        up = _dot(xs, wu_ref[0, 0])
        return _dot((gate * _sigmoid(gate) * up).astype(BF16), wd_ref[0])

    def scatter_add(rnd, y):
        picked = gather_mask(rnd)
        w_slot = jnp.sum(jnp.where(picked, w_row, 0.0), axis=-1, keepdims=True)
        yw = (y * w_slot).astype(BF16)
        key_col = jnp.where(_pick_lane(c_ref[...], e) > 0.0, _pick_lane(rc_ref[...], e), -1.0)
        slot = lax.broadcasted_iota(jnp.int32, (tm, cap), 1) + rnd * cap
        back = jnp.where(key_col == slot.astype(F32), 1.0, 0.0).astype(BF16)
        for c0 in range(0, d, col_chunk):
            o_ref[:, c0:c0 + col_chunk] += _dot(back, yw[:, c0:c0 + col_chunk])

    @pl.when(f == 0)
    def _():
        xs_ref[...] = _dot(jnp.where(gather_mask(0), 1.0, 0.0).astype(BF16), xn_ref[...]).astype(BF16)

    part = expert(xs_ref[...])

    @pl.when(f == 0)
    def _():
        ye_ref[...] = part

    @pl.when(f > 0)
    def _():
        ye_ref[...] += part

    @pl.when(f == n_f - 1)
    def _():
        scatter_add(0, ye_ref[...])

    def extra_round(rnd, carry):
        xs = _dot(jnp.where(gather_mask(rnd), 1.0, 0.0).astype(BF16), xn_ref[...]).astype(BF16)
        scatter_add(rnd, expert(xs))
        return carry

    lax.fori_loop(1, n_rounds, extra_round, 0)

    if final_norm:
        @pl.when((e == n_e - 1) & (f == n_f - 1))
        def _():
            for r in range(0, tm, row_chunk):
                o_ref[r:r + row_chunk, :] = _rms(o_ref[r:r + row_chunk, :], fg_ref[...])


def moe_residual(h, gain, comb, w_gate, w_up, w_down, final_gain, *, tm=1024, tf=512, cap=320):
    n, d = h.shape
    n_e, _, ff = w_gate.shape
    w_gate, w_up, w_down = _tile_columns(w_gate, tf), _tile_columns(w_up, tf), w_down.astype(BF16)
    tm = min(tm, n)
    final_norm = final_gain is not None
    if final_gain is None:
        final_gain = jnp.ones((d,), F32)
    n_f = ff // tf
    once = pl.Buffered(1)
    return pl.pallas_call(
        functools.partial(_moe_kernel, n_e=n_e, n_f=n_f, cap=cap, final_norm=final_norm,
                          row_chunk=256),
        grid=(n // tm, n_e, n_f),
        in_specs=[
            pl.BlockSpec((tm, d), lambda i, e, f: (i, 0), pipeline_mode=once),
            pl.BlockSpec((1, d), lambda i, e, f: (0, 0)),
            pl.BlockSpec((tm, LANES), lambda i, e, f: (i, 0), pipeline_mode=once),
            pl.BlockSpec((LANES, tm), lambda i, e, f: (0, i), pipeline_mode=once),
            pl.BlockSpec((1, 1, d, tf), lambda i, e, f: (e, f, 0, 0)),
            pl.BlockSpec((1, 1, d, tf), lambda i, e, f: (e, f, 0, 0)),
            pl.BlockSpec((1, tf, d), lambda i, e, f: (e, f, 0)),
            pl.BlockSpec((1, d), lambda i, e, f: (0, 0)),
        ],
        out_specs=pl.BlockSpec((tm, d), lambda i, e, f: (i, 0)),
        out_shape=jax.ShapeDtypeStruct((n, d), F32),
        scratch_shapes=[pltpu.VMEM((tm, d), BF16), pltpu.VMEM((tm, LANES), F32),
                        pltpu.VMEM((LANES, tm), F32), pltpu.VMEM((cap, d), BF16),
                        pltpu.VMEM((cap, d), F32)],
        compiler_params=_params(3),
    )(h, gain.reshape(1, d), comb, comb.T, w_gate, w_up, w_down, final_gain.reshape(1, d))


def _tri_inv_many(lows, nilpotency):
    c = lows[0].shape[0]
    ii = lax.broadcasted_iota(jnp.int32, (c, c), 0)
    jj = lax.broadcasted_iota(jnp.int32, (c, c), 1)
    eye = jnp.where(ii == jj, 1.0, 0.0)
    lb = [low.astype(BF16) for low in lows]
    ps = [_dot(b, b).astype(BF16) for b in lb]
    ts = [eye - low for low in lows]
    levels = int(math.log2(nilpotency)) - 1
    for level in range(levels):
        tb = [t.astype(BF16) for t in ts]
        if level + 1 < levels:
            nxt = [_dot(p, p).astype(BF16) for p in ps]
        ts = [t + _dot(b, p) for t, b, p in zip(ts, tb, ps)]
        if level + 1 < levels:
            ps = nxt
    return ts


def _gdn_local_kernel(alog_ref, dtb_ref, q_ref, k_ref, v_ref, cq_ref, ck_ref, cv_ref,
                      a1_ref, a2_ref, b2_ref,
                      u_ref, w_ref, qd_ref, kd_ref, aqk_ref, gl_ref,
                      xp_ref, qs_ref, ks_ref, vs_ref, gc1_ref, gc2_ref, gl2_ref, be2_ref, *,
                      seq, row_chunk):
    hd = pl.program_id(1)
    dk = q_ref.shape[1]
    c = GDN_CHUNK
    grp = GDN_GROUP
    n_groups = seq // grp
    pad = 8

    def conv_silu(x_ref, w_ref, dst_ref, normalise, scale):
        xp_ref[0:pad, :] = jnp.zeros((pad, dk), F32)
        xp_ref[pad:pad + seq, :] = x_ref[...].astype(F32)
        w = w_ref[...]
        for r in range(0, seq, row_chunk):
            y = xp_ref[pad - 3 + r:pad - 3 + r + row_chunk, :] * w[0:1, :]
            for j in range(1, GDN_CONV):
                y = y + xp_ref[pad - 3 + j + r:pad - 3 + j + r + row_chunk, :] * w[j:j + 1, :]
            y = y * _sigmoid(y)
            if normalise:
                y = y * lax.rsqrt(jnp.sum(y * y, axis=-1, keepdims=True) + NORM_EPS)
            if scale != 1.0:
                y = y * scale
            dst_ref[r:r + row_chunk, :] = y

    conv_silu(q_ref, cq_ref, qs_ref, True, dk ** -0.5)
    conv_silu(k_ref, ck_ref, ks_ref, True, 1.0)
    conv_silu(v_ref, cv_ref, vs_ref, False, 1.0)

    a_coef = -jnp.exp(alog_ref[hd])
    dtb = dtb_ref[hd]

    def log_decay(a):
        x = a + dtb
        return a_coef * (jnp.maximum(x, 0.0) + jnp.log(1.0 + jnp.exp(-jnp.abs(x))))

    ii = lax.broadcasted_iota(jnp.int32, (grp, grp), 0)
    jj = lax.broadcasted_iota(jnp.int32, (grp, grp), 1)
    same = (ii // c) == (jj // c)
    hi = lax.Precision.HIGHEST
    g1 = log_decay(a1_ref[0, 0])
    g2 = log_decay(a2_ref[0, 0])
    ones_b = jnp.where(same, 1.0, 0.0)
    gc1_ref[...] = jnp.dot(g1, jnp.where(same & (ii <= jj), 1.0, 0.0),
                           preferred_element_type=F32, precision=hi)
    gc2_ref[...] = jnp.dot(jnp.where(same & (jj <= ii), 1.0, 0.0), g2,
                           preferred_element_type=F32, precision=hi)
    gl2_ref[...] = jnp.dot(ones_b, g2, preferred_element_type=F32, precision=hi)
    gl_ref[0, 0] = jnp.exp(jnp.dot(g1, ones_b, preferred_element_type=F32, precision=hi))
    be2_ref[...] = _sigmoid(b2_ref[0, 0])

    causal = same & (ii >= jj)
    strict = same & (ii > jj)
    n_par = GDN_GROUPS_IN_FLIGHT

    def groups(it, carry):
        gis = [it * n_par + t for t in range(n_par)]
        rows = [pl.ds(pl.multiple_of(gi * grp, grp), grp) for gi in gis]
        q = [qs_ref[r, :] for r in rows]
        k = [ks_ref[r, :] for r in rows]
        v = [vs_ref[r, :] for r in rows]
        g_row = [gc1_ref[pl.ds(gi, 1), :] for gi in gis]
        g_col = [_pick_lane(gc2_ref[...], gi) for gi in gis]
        gl_col = [_pick_lane(gl2_ref[...], gi) for gi in gis]
        b_col = [_pick_lane(be2_ref[...], gi) for gi in gis]
        kbf = [x.astype(BF16) for x in k]
        kb = [x * b for x, b in zip(k, b_col)]
        kk = [_dot_nt(x.astype(BF16), y) for x, y in zip(kb, kbf)]
        qk = [_dot_nt(x.astype(BF16), y) for x, y in zip(q, kbf)]
        decay = [jnp.where(causal, jnp.exp(jnp.where(causal, gc - gr, 0.0)), 0.0)
                 for gc, gr in zip(g_col, g_row)]
        t_inv = _tri_inv_many([jnp.where(strict, x * d, 0.0) for x, d in zip(kk, decay)], c)
        e_col = [jnp.exp(gc) for gc in g_col]
        rhs = [jnp.concatenate([x * b, y * e], axis=1).astype(BF16)
               for x, b, y, e in zip(v, b_col, kb, e_col)]
        uw = [_dot(t.astype(BF16), r) for t, r in zip(t_inv, rhs)]
        for t in range(n_par):
            u_ref[rows[t], :] = uw[t][:, :dk].astype(u_ref.dtype)
            w_ref[rows[t], :] = uw[t][:, dk:].astype(w_ref.dtype)
            a_qk = qk[t] * decay[t]
            a_c = a_qk[:, 0:c]
            for blk in range(1, grp // c):
                a_c = a_c + a_qk[:, blk * c:(blk + 1) * c]
            aqk_ref[0, 0, rows[t], :] = a_c.astype(aqk_ref.dtype)
            qd_ref[rows[t], :] = (q[t] * e_col[t]).astype(qd_ref.dtype)
            kd_ref[rows[t], :] = (k[t] * jnp.exp(gl_col[t] - g_col[t])).astype(kd_ref.dtype)
        return carry

    lax.fori_loop(0, n_groups // n_par, groups, 0)


def _gdn_scan_kernel(gl_ref, u_ref, w_ref, qd_ref, kd_ref, aqk_ref, z_ref, on_ref, o_ref, *,
                     seq, heads_per_step):
    b = pl.program_id(0)
    hb = pl.program_id(1)
    c = GDN_CHUNK
    dk = on_ref.shape[1]

    def step(ci, states):
        rows = pl.ds(pl.multiple_of(ci * c, c), c)
        hs = range(heads_per_step)
        cols = [slice(hh * dk, (hh + 1) * dk) for hh in hs]
        ws_qs = [_dot(jnp.concatenate([w_ref[rows, cols[hh]], qd_ref[rows, cols[hh]]], axis=0),
                      states[hh].astype(BF16)) for hh in hs]
        v_new = [(u_ref[rows, cols[hh]].astype(F32) - ws_qs[hh][:c]).astype(BF16) for hh in hs]
        intra = [_dot(aqk_ref[0, hh, rows, :], v_new[hh]) for hh in hs]
        outer = [_dot_tn(kd_ref[rows, cols[hh]], v_new[hh]) for hh in hs]
        new_states = []
        for hh in hs:
            decay = gl_ref[b, hb * heads_per_step + hh, ci]
            new_states.append(states[hh] * decay + outer[hh])
            z = z_ref[rows, cols[hh]].astype(F32)
            o = ws_qs[hh][c:] + intra[hh]
            o_ref[rows, cols[hh]] = (_rms(o, on_ref[...]) * (z * _sigmoid(z))).astype(o_ref.dtype)
        return tuple(new_states)

    lax.fori_loop(0, seq // c, step,
                  tuple(jnp.zeros((dk, dk), F32) for _ in range(heads_per_step)))


def gdn_core(proj, gates, conv_w, a_log, dt_bias, out_norm, *, batch, seq):
    n = proj.shape[0]
    nh = GDN_HEADS
    dk = proj.shape[1] // (4 * nh)
    c = GDN_CHUNK
    grp = GDN_GROUP
    n_groups = seq // grp
    hps = GDN_HEADS_PER_STEP
    bl = gates[:, :nh].reshape(batch, seq, nh).transpose(0, 2, 1)
    al = gates[:, nh:2 * nh].reshape(batch, seq, nh).transpose(0, 2, 1)
    a1 = al.reshape(batch, nh, n_groups, grp)
    a2 = a1.transpose(0, 1, 3, 2)
    b2 = bl.reshape(batch, nh, n_groups, grp).transpose(0, 1, 3, 2)
    smem = pl.BlockSpec(memory_space=pltpu.SMEM)
    col = lambda off: pl.BlockSpec((seq, dk), lambda b, h, off=off: (b, off * nh + h))
    cw = lambda off: pl.BlockSpec((GDN_CONV, dk), lambda b, h, off=off: (0, off * nh + h))
    g1 = pl.BlockSpec((1, 1, n_groups, grp), lambda b, h: (b, h, 0, 0))
    g2 = pl.BlockSpec((1, 1, grp, n_groups), lambda b, h: (b, h, 0, 0))
    head_out = pl.BlockSpec((seq, dk), lambda b, h: (b, h))
    wide = jax.ShapeDtypeStruct((n, nh * dk), BF16)
    u, w, qd, kd, aqk, gl = pl.pallas_call(
        functools.partial(_gdn_local_kernel, seq=seq, row_chunk=256),
        grid=(batch, nh),
        in_specs=[smem, smem, col(0), col(1), col(2), cw(0), cw(1), cw(2), g1, g2, g2],
        out_specs=[head_out, head_out, head_out, head_out,
                   pl.BlockSpec((1, 1, seq, c), lambda b, h: (b, h, 0, 0)), g1],
        out_shape=[wide, wide, wide, wide,
                   jax.ShapeDtypeStruct((batch, nh, seq, c), BF16),
                   jax.ShapeDtypeStruct((batch, nh, n_groups, grp), F32)],
        scratch_shapes=[
            pltpu.VMEM((seq + 8, dk), F32),
            pltpu.VMEM((seq, dk), F32), pltpu.VMEM((seq, dk), F32), pltpu.VMEM((seq, dk), F32),
            pltpu.VMEM((n_groups, grp), F32), pltpu.VMEM((grp, n_groups), F32),
            pltpu.VMEM((grp, n_groups), F32), pltpu.VMEM((grp, n_groups), F32),
        ],
        compiler_params=_params(2),
    )(a_log, dt_bias, proj, proj, proj, conv_w, conv_w, conv_w, a1, a2, b2)
    chunk_decay = gl[..., c - 1::c].reshape(batch, nh, seq // c)
    heads = pl.BlockSpec((seq, hps * dk), lambda b, h: (b, h))
    return pl.pallas_call(
        functools.partial(_gdn_scan_kernel, seq=seq, heads_per_step=hps),
        grid=(batch, nh // hps),
        in_specs=[smem, heads, heads, heads, heads,
                  pl.BlockSpec((1, hps, seq, c), lambda b, h: (b, h, 0, 0)),
                  pl.BlockSpec((seq, hps * dk), lambda b, h: (b, 3 * (nh // hps) + h)),
                  pl.BlockSpec((1, dk), lambda b, h: (0, 0))],
        out_specs=heads,
        out_shape=wide,
        compiler_params=_params(2),
    )(chunk_decay, u, w, qd, kd, aqk, proj, out_norm.reshape(1, dk))


def _rope(x, cos, sin_a, sin_b):
    half = ROPE_DIM // 2
    return (x * cos + pltpu.roll(x, LANES - half, 1) * sin_a + pltpu.roll(x, half, 1) * sin_b)


def _kv_rope_kernel(x_ref, cos_ref, sa_ref, sb_ref, o_ref):
    j = pl.program_id(1)
    tensor = j // NSA_GROUPS
    x = x_ref[...]
    roped = _rope(x, cos_ref[...], sa_ref[...], sb_ref[...])
    o_ref[...] = jnp.where((tensor == 2) | (tensor == 4), roped, x).astype(o_ref.dtype)


def kv_rope(kv, cos, sin_a, sin_b, *, tm=1024):
    n, width = kv.shape
    tm = min(tm, n)
    tab = pl.BlockSpec((tm, LANES), lambda i, j: (i, 0))
    return pl.pallas_call(
        _kv_rope_kernel,
        grid=(n // tm, width // LANES),
        in_specs=[pl.BlockSpec((tm, LANES), lambda i, j: (i, j)), tab, tab, tab],
        out_specs=pl.BlockSpec((tm, LANES), lambda i, j: (i, j)),
        out_shape=jax.ShapeDtypeStruct((n, width), BF16),
        compiler_params=_params(2),
    )(kv, cos, sin_a, sin_b)


def _compress_kernel(kc_ref, vc_ref, pk_ref, pv_ref, w1k_ref, w1v_ref, w2k_ref, w2v_ref,
                     ok_ref, ov_ref, tp_ref, *, seq):
    dh = kc_ref.shape[1]
    n_rows = ok_ref.shape[2]
    hidden = w1k_ref.shape[1]

    def run(t_ref, pos_ref, w1_ref, w2_ref, out_ref):
        tp_ref[0:seq, :] = t_ref[...]
        tp_ref[seq:seq + CMP_STRIDE, :] = jnp.zeros((CMP_STRIDE, dh), F32)
        acc = jnp.zeros((n_rows, hidden), F32)
        for l in range(CMP_BLOCK):
            x = tp_ref[pl.ds(l, n_rows, stride=CMP_STRIDE), :] + pos_ref[l:l + 1, :]
            acc = acc + _dot(x.astype(BF16), w1_ref[l * dh:(l + 1) * dh, :])
        hid = acc * _sigmoid(acc)
        out_ref[0, 0] = _dot(hid.astype(BF16), w2_ref[...]).astype(out_ref.dtype)

    run(kc_ref, pk_ref, w1k_ref, w2k_ref, ok_ref)
    run(vc_ref, pv_ref, w1v_ref, w2v_ref, ov_ref)


def nsa_compress(kv, pos_k, pos_v, w1k, w1v, w2k, w2v, *, batch, seq):
    dh = pos_k.shape[1]
    n_rows = seq // CMP_STRIDE
    hidden = w1k.shape[1]
    full = lambda shape: pl.BlockSpec(shape, lambda b, g: (0,) * len(shape))
    out = pl.BlockSpec((1, 1, n_rows, dh), lambda b, g: (b, g, 0, 0))
    return pl.pallas_call(
        functools.partial(_compress_kernel, seq=seq),
        grid=(batch, NSA_GROUPS),
        in_specs=[
            pl.BlockSpec((seq, dh), lambda b, g: (b, g)),
            pl.BlockSpec((seq, dh), lambda b, g: (b, NSA_GROUPS + g)),
            full((CMP_BLOCK, dh)), full((CMP_BLOCK, dh)),
            full((CMP_BLOCK * dh, hidden)), full((CMP_BLOCK * dh, hidden)),
            full((hidden, dh)), full((hidden, dh)),
        ],
        out_specs=[out, out],
        out_shape=[jax.ShapeDtypeStruct((batch, NSA_GROUPS, n_rows, dh), BF16)] * 2,
        scratch_shapes=[pltpu.VMEM((seq + CMP_STRIDE, dh), F32)],
        compiler_params=_params(2),
    )(kv, kv, pos_k, pos_v, w1k, w1v, w2k, w2v)


def _masked_softmax(s, mask):
    sm = jnp.where(mask, s, NEG)
    m = jnp.max(sm, axis=-1, keepdims=True)
    p = jnp.where(mask, jnp.exp(sm - m), 0.0)
    return p / jnp.maximum(jnp.sum(p, axis=-1, keepdims=True), 1e-30)


def _nsa_kernel(q_ref, gate_ref, cos_ref, sa_ref, sb_ref, kc_ref, vc_ref, ks_ref, vs_ref,
                kw_ref, vw_ref, ovt_ref, eb_ref, wb_ref, o_ref, *, seq, tq, kc_len):
    g = pl.program_id(1)
    n = pl.program_id(2)
    t0 = n * tq
    dh = LANES
    hpg = NSA_HPG
    m_rows = hpg * tq
    n_sel = seq // SEL_BLOCK
    topk = min(SEL_TOPK, n_sel)
    scale = dh ** -0.5

    stack = lambda x: jnp.concatenate([x] * hpg, axis=0)
    q = q_ref[...]
    qs = jnp.concatenate([q[:, h * dh:(h + 1) * dh] for h in range(hpg)], axis=0) * scale
    qr = _rope(qs, stack(cos_ref[...]), stack(sa_ref[...]), stack(sb_ref[...]))
    qb = qs.astype(BF16)
    qrb = qr.astype(BF16)
    row = lax.broadcasted_iota(jnp.int32, (m_rows, 1), 0)
    tq_col = t0 + (row & (tq - 1))

    s_c = _dot_nt(qb, kc_ref[0, 0])
    lane_c = lax.broadcasted_iota(jnp.int32, s_c.shape, 1)
    p_c = _masked_softmax(s_c, lane_c * CMP_STRIDE + (CMP_BLOCK - 1) <= tq_col)
    o_c = _dot(p_c.astype(BF16), vc_ref[0, 0])

    p_sum = p_c[0:tq]
    for h in range(1, hpg):
        p_sum = p_sum + p_c[h * tq:(h + 1) * tq]
    imp_t = lax.dot_general(ovt_ref[...], p_sum, (((1,), (1,)), ((), ())),
                            preferred_element_type=F32, precision=lax.Precision.HIGHEST)
    nsp = -(-n_sel // 8) * 8
    imp_t = imp_t[0:nsp]
    blk = lax.broadcasted_iota(jnp.int32, (nsp, tq), 0)
    t1 = t0 + lax.broadcasted_iota(jnp.int32, (nsp, tq), 1)
    cur = t1 // SEL_BLOCK
    forced = (blk == 0) | (blk == cur) | (blk == cur - 1)
    visible = blk * SEL_BLOCK <= t1
    score = jnp.where(forced, BIG, jnp.where(visible, imp_t, -BIG))
    score = jnp.where(blk < n_sel, score, -3e38)
    rank = jnp.zeros((nsp, tq), F32)
    for j2 in range(n_sel):
        other = score[j2:j2 + 1, :]
        rank = rank + jnp.where(other > score, 1.0,
                                jnp.where((other == score) & (blk > j2), 1.0, 0.0))
    chosen = (rank < topk) & visible & (blk < n_sel)
    bias_t = jnp.where(chosen, 0.0, NEG)
    if nsp < LANES:
        bias_t = jnp.concatenate([bias_t, jnp.full((LANES - nsp, tq), NEG, F32)], axis=0)
    sel_bias = stack(bias_t.T).astype(BF16)

    lhs_sel = jnp.concatenate([qrb, sel_bias], axis=1)

    def sel_step(ci, carry, diagonal):
        m_i, l_i, acc = carry
        k0 = pl.multiple_of(ci * kc_len, kc_len)
        keys = jnp.concatenate([ks_ref[pl.ds(k0, kc_len), :], eb_ref[pl.ds(k0, kc_len), :]], axis=1)
        s = _dot_nt(lhs_sel, keys)
        if diagonal:
            kpos = k0 + lax.broadcasted_iota(jnp.int32, (m_rows, kc_len), 1)
            s = jnp.where(kpos <= tq_col, s, NEG)
        m_new = jnp.maximum(m_i, jnp.max(s, axis=-1, keepdims=True))
        alpha = jnp.exp(m_i - m_new)
        p = jnp.exp(s - m_new)
        l_new = alpha * l_i + jnp.sum(p, axis=-1, keepdims=True)
        acc_new = alpha * acc + _dot(p.astype(BF16), vs_ref[pl.ds(k0, kc_len), :])
        return m_new, l_new, acc_new

    n_kc = (t0 + tq + kc_len - 1) // kc_len
    init = (jnp.full((m_rows, 1), NEG, F32), jnp.zeros((m_rows, 1), F32),
            jnp.zeros((m_rows, dh), F32))
    carry = lax.fori_loop(0, n_kc - 1, functools.partial(sel_step, diagonal=False), init)
    _, l_s, acc_s = sel_step(n_kc - 1, carry, True)
    o_s = acc_s / jnp.maximum(l_s, 1e-30)

    wk = WINDOW + tq
    start = pl.multiple_of(jnp.maximum(t0 - WINDOW, 0), LANES)
    case = jnp.minimum(n, WINDOW // tq)
    row_id = lax.broadcasted_iota(jnp.int32, (m_rows, LANES), 0) & (tq - 1)
    lane_id = lax.broadcasted_iota(jnp.int32, (m_rows, LANES), 1)
    lhs_w = jnp.concatenate([qrb, jnp.where(row_id == lane_id, 1.0, 0.0).astype(BF16)], axis=1)
    keys_w = jnp.concatenate([kw_ref[pl.ds(start, wk), :], wb_ref[case]], axis=1)
    s_w = _dot_nt(lhs_w, keys_w)
    p_w = jnp.exp(s_w - jnp.max(s_w, axis=-1, keepdims=True))
    p_w = p_w / jnp.maximum(jnp.sum(p_w, axis=-1, keepdims=True), 1e-30)
    o_w = _dot(p_w.astype(BF16), vw_ref[pl.ds(start, wk), :])

    gates = _sigmoid(gate_ref[...])
    for h in range(hpg):
        base = (g * hpg + h) * NSA_N_BRANCH
        rows = slice(h * tq, (h + 1) * tq)
        out = (_pick_lane(gates, base) * o_c[rows] + _pick_lane(gates, base + 1) * o_s[rows]
               + _pick_lane(gates, base + 2) * o_w[rows])
        o_ref[:, h * dh:(h + 1) * dh] = out.astype(o_ref.dtype)


def nsa_attention(q, gates, cos, sin_a, sin_b, k_cmp, v_cmp, kvr, tables, *, batch, seq, kc_len=512):
    overlap_t, block_onehot, window_bias = tables
    n = q.shape[0]
    dh = LANES
    tq = LANES
    gw = NSA_HPG * dh
    nq = seq // tq
    kc_len = min(kc_len, seq)
    rows = lambda width: pl.BlockSpec((tq, width), lambda b, g, i: (b * nq + i, 0))
    cmp_spec = pl.BlockSpec((1, 1, k_cmp.shape[2], dh), lambda b, g, i: (b, g, 0, 0))
    kv_spec = lambda t: pl.BlockSpec((seq, dh), lambda b, g, i, t=t: (b, t * NSA_GROUPS + g))
    const = lambda shape: pl.BlockSpec(shape, lambda b, g, i: (0,) * len(shape))
    return pl.pallas_call(
        functools.partial(_nsa_kernel, seq=seq, tq=tq, kc_len=kc_len),
        grid=(batch, NSA_GROUPS, nq),
        in_specs=[
            pl.BlockSpec((tq, gw), lambda b, g, i: (b * nq + i, g)),
            rows(LANES), rows(LANES), rows(LANES), rows(LANES),
            cmp_spec, cmp_spec, kv_spec(2), kv_spec(3), kv_spec(4), kv_spec(5),
            const(overlap_t.shape), const(block_onehot.shape), const(window_bias.shape),
        ],
        out_specs=pl.BlockSpec((tq, gw), lambda b, g, i: (b * nq + i, g)),
        out_shape=jax.ShapeDtypeStruct((n, NSA_HEADS * dh), BF16),
        compiler_params=_params(3),
    )(q, gates, cos, sin_a, sin_b, k_cmp, v_cmp, kvr, kvr, kvr, kvr,
      overlap_t, block_onehot, window_bias)


def _rope_tables(positions):
    half = ROPE_DIM // 2
    inv = ROPE_THETA ** (-jnp.arange(0, ROPE_DIM, 2, dtype=F32) / ROPE_DIM)
    ang = positions.astype(F32).reshape(-1, 1) * inv
    cos, sin = jnp.cos(ang), jnp.sin(ang)
    n = ang.shape[0]
    ones = jnp.ones((n, LANES - ROPE_DIM), F32)
    zeros_h = jnp.zeros((n, half), F32)
    zeros_t = jnp.zeros((n, LANES - ROPE_DIM), F32)
    cos_t = jnp.concatenate([cos, cos, ones], axis=1)
    sin_a = jnp.concatenate([-sin, zeros_h, zeros_t], axis=1)
    sin_b = jnp.concatenate([zeros_h, sin, zeros_t], axis=1)
    return cos_t, sin_a, sin_b


def _overlap_table(seq):
    n_cmp = (seq - CMP_BLOCK) // CMP_STRIDE + 1
    n_sel = seq // SEL_BLOCK
    c_idx = jnp.arange(n_cmp)
    j_idx = jnp.arange(n_sel)
    ov = jnp.clip(jnp.minimum(c_idx[:, None] * CMP_STRIDE + CMP_BLOCK, (j_idx[None, :] + 1) * SEL_BLOCK)
                  - jnp.maximum(c_idx[:, None] * CMP_STRIDE, j_idx[None, :] * SEL_BLOCK), 0)
    ov = ov.astype(F32) / CMP_STRIDE
    return jnp.pad(ov, ((0, LANES - n_cmp), (0, LANES - n_sel)))


def _nsa_tables(seq):
    tq = LANES
    key = jnp.arange(seq)
    block_onehot = (key[:, None] // SEL_BLOCK == jnp.arange(LANES)[None, :]).astype(BF16)
    x = jnp.arange(WINDOW + tq)[None, :, None]
    t = (jnp.arange(WINDOW // tq + 1) * tq)[:, None, None] + jnp.arange(tq)[None, None, :]
    window_bias = jnp.where((x <= t) & (x > t - WINDOW), 0.0, NEG).astype(BF16)
    return _overlap_table(seq).T, block_onehot, window_bias


def _split_pad(w, main):
    extra = w[:, main:]
    return w[:, :main].astype(BF16), jnp.pad(extra, ((0, 0), (0, LANES - extra.shape[1]))).astype(BF16)


def kernel(x, positions, attn_norm, ffn_norm, final_norm, gdn_w_in, gdn_conv, gdn_a_log, gdn_dt_bias,
           gdn_out_norm, gdn_w_out, kv_norm, nsa_w_kv, cmp_pos_k, cmp_w1_k, cmp_w2_k, cmp_pos_v, cmp_w1_v,
           cmp_w2_v, nsa_w_q, nsa_w_o, ffn_w_gate, ffn_w_up, ffn_w_down, moe_router, moe_w_gate, moe_w_up,
           moe_w_down):
    batch, seq, d = x.shape
    depth = attn_norm.shape[0]
    n_a = depth // 2
    h = x.reshape(batch * seq, d)
    gdn_width = gdn_w_out.shape[1]
    nsa_width = nsa_w_o.shape[1]
    shared = None
    for l in range(depth):
        if l < n_a:
            w_main, w_gates = _split_pad(gdn_w_in[l], 4 * gdn_width)
            proj, gates = rms_matmul(h, attn_norm[l], w_main, w_gates, out_dtype=BF16)
            o = gdn_core(proj, gates, gdn_conv[l], gdn_a_log[l], gdn_dt_bias[l], gdn_out_norm[l],
                         batch=batch, seq=seq)
            h = matmul_residual(o, gdn_w_out[l].astype(BF16), h)
        else:
            cos, sin_a, sin_b, k_cmp, v_cmp, kvr, tables = shared
            w_main, w_gates = _split_pad(nsa_w_q[l - n_a], nsa_width)
            q, gates = rms_matmul(h, attn_norm[l], w_main, w_gates)
            o = nsa_attention(q, gates, cos, sin_a, sin_b, k_cmp, v_cmp, kvr, tables,
                              batch=batch, seq=seq)
            h = matmul_residual(o, nsa_w_o[l - n_a].astype(BF16), h)
        final_gain = final_norm if l == depth - 1 else None
        if l % 2 == 0:
            i = l // 2
            h = ffn_residual(h, ffn_norm[l], None, ffn_w_gate[i][None], ffn_w_up[i][None],
                             ffn_w_down[i][None], final_gain)
        else:
            i = l // 2
            comb = route_tokens(h, ffn_norm[l], moe_router[i])
            h = moe_residual(h, ffn_norm[l], comb, moe_w_gate[i], moe_w_up[i], moe_w_down[i], final_gain)
        if l == n_a - 1:
            cos, sin_a, sin_b = _rope_tables(positions)
            kv, _ = rms_matmul(h, kv_norm, nsa_w_kv.astype(BF16), jnp.zeros((d, LANES), BF16))
            kvr = kv_rope(kv, cos, sin_a, sin_b)
            k_cmp, v_cmp = nsa_compress(kv, cmp_pos_k, cmp_pos_v, cmp_w1_k.astype(BF16),
                                        cmp_w1_v.astype(BF16), cmp_w2_k.astype(BF16),
                                        cmp_w2_v.astype(BF16), batch=batch, seq=seq)
            shared = (cos, sin_a, sin_b, k_cmp, v_cmp, kvr, _nsa_tables(seq))
    return h.reshape(batch, seq, d)
```

```python
import functools
import math

import jax
import jax.numpy as jnp
from jax import lax
from jax.experimental import pallas as pl
from jax.experimental.pallas import tpu as pltpu

F32 = jnp.float32
BF16 = jnp.bfloat16

NORM_EPS = 1e-6
LANES = 128
GDN_HEADS = 16
GDN_CONV = 4
GDN_CHUNK = 64
GDN_GROUP = 256
GDN_GROUPS_IN_FLIGHT = 4
GDN_HEADS_PER_STEP = 4
NSA_HEADS = 16
NSA_GROUPS = 4
NSA_HPG = NSA_HEADS // NSA_GROUPS
NSA_N_BRANCH = 3
CMP_BLOCK = 32
CMP_STRIDE = 16
SEL_BLOCK = 64
SEL_TOPK = 16
WINDOW = 512
BIG = 1e9
NEG = -1e30
ROPE_THETA = 500000.0
ROPE_DIM = 32
TOP_K = 2

VMEM_LIMIT_BYTES = 56 * 1024 * 1024


def _params(n_axes):
    return pltpu.CompilerParams(dimension_semantics=("arbitrary",) * n_axes,
                                vmem_limit_bytes=VMEM_LIMIT_BYTES)


def _rms(x, gain):
    ms = jnp.mean(x * x, axis=-1, keepdims=True)
    return x * lax.rsqrt(ms + NORM_EPS) * gain


def _sigmoid(x):
    return 1.0 / (1.0 + jnp.exp(-x))


def _dot(a, b):
    return jnp.dot(a, b, preferred_element_type=F32)


def _dot_nt(a, b):
    return lax.dot_general(a, b, (((1,), (1,)), ((), ())), preferred_element_type=F32)


def _dot_tn(a, b):
    return lax.dot_general(a, b, (((0,), (0,)), ((), ())), preferred_element_type=F32)


def _pick_lane(x, idx):
    lane = lax.broadcasted_iota(jnp.int32, x.shape, 1)
    return jnp.sum(jnp.where(lane == idx, x, 0.0), axis=-1, keepdims=True)


def _rms_matmul_kernel(h_ref, g_ref, w_ref, we_ref, o_ref, oe_ref, xn_ref, *, row_chunk):
    j = pl.program_id(1)

    @pl.when(j == 0)
    def _():
        for r in range(0, h_ref.shape[0], row_chunk):
            xn_ref[r:r + row_chunk, :] = _rms(h_ref[r:r + row_chunk, :], g_ref[...]).astype(BF16)
        oe_ref[...] = _dot(xn_ref[...], we_ref[...])

    o_ref[...] = _dot(xn_ref[...], w_ref[...]).astype(o_ref.dtype)


def rms_matmul(h, gain, w, w_extra, *, tm=1024, tn=1024, out_dtype=F32):
    n, d = h.shape
    nm = w.shape[1]
    tm = min(tm, n)
    tn = min(tn, nm)
    return pl.pallas_call(
        functools.partial(_rms_matmul_kernel, row_chunk=256),
        grid=(n // tm, nm // tn),
        in_specs=[
            pl.BlockSpec((tm, d), lambda i, j: (i, 0)),
            pl.BlockSpec((1, d), lambda i, j: (0, 0)),
            pl.BlockSpec((d, tn), lambda i, j: (0, j)),
            pl.BlockSpec((d, LANES), lambda i, j: (0, 0)),
        ],
        out_specs=[
            pl.BlockSpec((tm, tn), lambda i, j: (i, j)),
            pl.BlockSpec((tm, LANES), lambda i, j: (i, 0)),
        ],
        out_shape=[jax.ShapeDtypeStruct((n, nm), out_dtype),
                   jax.ShapeDtypeStruct((n, LANES), F32)],
        scratch_shapes=[pltpu.VMEM((tm, d), BF16)],
        compiler_params=_params(2),
    )(h, gain.reshape(1, d), w, w_extra)


def _matmul_res_kernel(x_ref, w_ref, r_ref, o_ref):
    o_ref[...] = r_ref[...] + _dot(x_ref[...], w_ref[...])


def matmul_residual(x, w, res, *, tm=1024, tn=512):
    n, k = x.shape
    nm = w.shape[1]
    tm = min(tm, n)
    return pl.pallas_call(
        _matmul_res_kernel,
        grid=(n // tm, nm // tn),
        in_specs=[
            pl.BlockSpec((tm, k), lambda i, j: (i, 0)),
            pl.BlockSpec((k, tn), lambda i, j: (0, j)),
            pl.BlockSpec((tm, tn), lambda i, j: (i, j)),
        ],
        out_specs=pl.BlockSpec((tm, tn), lambda i, j: (i, j)),
        out_shape=jax.ShapeDtypeStruct((n, nm), F32),
        compiler_params=_params(2),
    )(x, w, res)


def _router_kernel(h_ref, g_ref, r_ref, o_ref, *, n_experts, row_chunk):
    for r in range(0, h_ref.shape[0], row_chunk):
        xn = _rms(h_ref[r:r + row_chunk, :], g_ref[...])
        logits = jnp.dot(xn, r_ref[...], preferred_element_type=F32,
                         precision=lax.Precision.HIGHEST)
        lane = lax.broadcasted_iota(jnp.int32, logits.shape, 1)
        valid = lane < n_experts
        z = jnp.where(valid, logits, NEG)
        z = z - jnp.max(z, axis=-1, keepdims=True)
        p = jnp.where(valid, jnp.exp(z), 0.0)
        p = p / jnp.sum(p, axis=-1, keepdims=True)
        p = jnp.where(valid, p, -1.0)
        p1 = jnp.max(p, axis=-1, keepdims=True)
        i1 = jnp.min(jnp.where(p == p1, lane, LANES), axis=-1, keepdims=True)
        pm = jnp.where(lane == i1, -1.0, p)
        p2 = jnp.max(pm, axis=-1, keepdims=True)
        i2 = jnp.min(jnp.where(pm == p2, lane, LANES), axis=-1, keepdims=True)
        den = p1 + p2
        o_ref[r:r + row_chunk, :] = (jnp.where(lane == i1, p1 / den, 0.0)
                                     + jnp.where(lane == i2, p2 / den, 0.0))


def route_tokens(h, gain, router, *, tm=1024):
    n, d = h.shape
    n_experts = router.shape[1]
    tm = min(tm, n)
    r_pad = jnp.pad(router, ((0, 0), (0, LANES - n_experts)))
    return pl.pallas_call(
        functools.partial(_router_kernel, n_experts=n_experts, row_chunk=256),
        grid=(n // tm,),
        in_specs=[
            pl.BlockSpec((tm, d), lambda i: (i, 0)),
            pl.BlockSpec((1, d), lambda i: (0, 0)),
            pl.BlockSpec((d, LANES), lambda i: (0, 0)),
        ],
        out_specs=pl.BlockSpec((tm, LANES), lambda i: (i, 0)),
        out_shape=jax.ShapeDtypeStruct((n, LANES), F32),
        compiler_params=_params(1),
    )(h, gain.reshape(1, d), r_pad)


def _ffn_kernel(h_ref, g_ref, c_ref, wg_ref, wu_ref, wd_ref, fg_ref, o_ref, xn_ref, *,
                n_e, n_f, use_comb, final_norm, row_chunk):
    e = pl.program_id(1)
    f = pl.program_id(2)

    @pl.when((e == 0) & (f == 0))
    def _():
        for r in range(0, h_ref.shape[0], row_chunk):
            x = h_ref[r:r + row_chunk, :]
            xn_ref[r:r + row_chunk, :] = _rms(x, g_ref[...]).astype(BF16)
            o_ref[r:r + row_chunk, :] = x

    xn = xn_ref[...]
    gate = _dot(xn, wg_ref[0])
    up = _dot(xn, wu_ref[0])
    act = gate * _sigmoid(gate) * up
    if use_comb:
        act = act * _pick_lane(c_ref[...], e)
    o_ref[...] += _dot(act.astype(BF16), wd_ref[0])

    if final_norm:
        @pl.when((e == n_e - 1) & (f == n_f - 1))
        def _():
            for r in range(0, h_ref.shape[0], row_chunk):
                o_ref[r:r + row_chunk, :] = _rms(o_ref[r:r + row_chunk, :], fg_ref[...])


def ffn_residual(h, gain, comb, w_gate, w_up, w_down, final_gain, *, tm=1024, tf=512):
    n, d = h.shape
    n_e, _, ff = w_gate.shape
    w_gate, w_up, w_down = w_gate.astype(BF16), w_up.astype(BF16), w_down.astype(BF16)
    tm = min(tm, n)
    use_comb = comb is not None
    final_norm = final_gain is not None
    if comb is None:
        comb = jnp.ones((n, LANES), F32)
    if final_gain is None:
        final_gain = jnp.ones((d,), F32)
    n_f = ff // tf
    return pl.pallas_call(
        functools.partial(_ffn_kernel, n_e=n_e, n_f=n_f, use_comb=use_comb,
                          final_norm=final_norm, row_chunk=256),
        grid=(n // tm, n_e, n_f),
        in_specs=[
            pl.BlockSpec((tm, d), lambda i, e, f: (i, 0), pipeline_mode=pl.Buffered(1)),
            pl.BlockSpec((1, d), lambda i, e, f: (0, 0)),
            pl.BlockSpec((tm, LANES), lambda i, e, f: (i, 0), pipeline_mode=pl.Buffered(1)),
            pl.BlockSpec((1, d, tf), lambda i, e, f: (e, 0, f)),
            pl.BlockSpec((1, d, tf), lambda i, e, f: (e, 0, f)),
            pl.BlockSpec((1, tf, d), lambda i, e, f: (e, f, 0)),
            pl.BlockSpec((1, d), lambda i, e, f: (0, 0)),
        ],
        out_specs=pl.BlockSpec((tm, d), lambda i, e, f: (i, 0)),
        out_shape=jax.ShapeDtypeStruct((n, d), F32),
        scratch_shapes=[pltpu.VMEM((tm, d), BF16)],
        compiler_params=_params(3),
    )(h, gain.reshape(1, d), comb, w_gate, w_up, w_down, final_gain.reshape(1, d))


def _moe_kernel(h_ref, g_ref, c_ref, ct_ref, wg_ref, wu_ref, wd_ref, fg_ref, o_ref,
                xn_ref, rc_ref, rr_ref, xs_ref, ye_ref, *, n_e, n_f, cap, final_norm, row_chunk):
    e = pl.program_id(1)
    f = pl.program_id(2)
    tm, d = h_ref.shape
    col_chunk = 512

    @pl.when((e == 0) & (f == 0))
    def _():
        for r in range(0, tm, row_chunk):
            x = h_ref[r:r + row_chunk, :]
            xn_ref[r:r + row_chunk, :] = _rms(x, g_ref[...]).astype(BF16)
            o_ref[r:r + row_chunk, :] = x
        ti = lax.broadcasted_iota(jnp.int32, (tm, tm), 0)
        tj = lax.broadcasted_iota(jnp.int32, (tm, tm), 1)
        routed_c = jnp.where(c_ref[...] > 0.0, 1.0, 0.0).astype(BF16)
        routed_r = jnp.where(ct_ref[...] > 0.0, 1.0, 0.0).astype(BF16)
        rc_ref[...] = _dot(jnp.where(tj < ti, 1.0, 0.0).astype(BF16), routed_c)
        rr_ref[...] = _dot(routed_r, jnp.where(ti < tj, 1.0, 0.0).astype(BF16))

    key_row = jnp.where(ct_ref[pl.ds(e, 1), :] > 0.0, rr_ref[pl.ds(e, 1), :], -1.0)
    w_row = ct_ref[pl.ds(e, 1), :]
    n_routed = jnp.sum(jnp.where(key_row >= 0.0, 1.0, 0.0)).astype(jnp.int32)
    n_rounds = (n_routed + cap - 1) // cap

    def gather_mask(rnd):
        slot = lax.broadcasted_iota(jnp.int32, (cap, tm), 0) + rnd * cap
        return key_row == slot.astype(F32)

    def expert(xs):
        gate = _dot(xs, wg_ref[0])
        up = _dot(xs, wu_ref[0])
        return _dot((gate * _sigmoid(gate) * up).astype(BF16), wd_ref[0])

    def scatter_add(rnd, y):
        picked = gather_mask(rnd)
        w_slot = jnp.sum(jnp.where(picked, w_row, 0.0), axis=-1, keepdims=True)
        yw = (y * w_slot).astype(BF16)
        key_col = jnp.where(_pick_lane(c_ref[...], e) > 0.0, _pick_lane(rc_ref[...], e), -1.0)
        slot = lax.broadcasted_iota(jnp.int32, (tm, cap), 1) + rnd * cap
        back = jnp.where(key_col == slot.astype(F32), 1.0, 0.0).astype(BF16)
        for c0 in range(0, d, col_chunk):
            o_ref[:, c0:c0 + col_chunk] += _dot(back, yw[:, c0:c0 + col_chunk])

    @pl.when(f == 0)
    def _():
        xs_ref[...] = _dot(jnp.where(gather_mask(0), 1.0, 0.0).astype(BF16), xn_ref[...]).astype(BF16)

    part = expert(xs_ref[...])

    @pl.when(f == 0)
    def _():
        ye_ref[...] = part

    @pl.when(f > 0)
    def _():
        ye_ref[...] += part

    @pl.when(f == n_f - 1)
    def _():
        scatter_add(0, ye_ref[...])

    def extra_round(rnd, carry):
        xs = _dot(jnp.where(gather_mask(rnd), 1.0, 0.0).astype(BF16), xn_ref[...]).astype(BF16)
        scatter_add(rnd, expert(xs))
        return carry

    lax.fori_loop(1, n_rounds, extra_round, 0)

    if final_norm:
        @pl.when((e == n_e - 1) & (f == n_f - 1))
        def _():
            for r in range(0, tm, row_chunk):
                o_ref[r:r + row_chunk, :] = _rms(o_ref[r:r + row_chunk, :], fg_ref[...])


def moe_residual(h, gain, comb, w_gate, w_up, w_down, final_gain, *, tm=1024, tf=512, cap=320):
    n, d = h.shape
    n_e, _, ff = w_gate.shape
    w_gate, w_up, w_down = w_gate.astype(BF16), w_up.astype(BF16), w_down.astype(BF16)
    tm = min(tm, n)
    final_norm = final_gain is not None
    if final_gain is None:
        final_gain = jnp.ones((d,), F32)
    n_f = ff // tf
    once = pl.Buffered(1)
    return pl.pallas_call(
        functools.partial(_moe_kernel, n_e=n_e, n_f=n_f, cap=cap, final_norm=final_norm,
                          row_chunk=256),
        grid=(n // tm, n_e, n_f),
        in_specs=[
            pl.BlockSpec((tm, d), lambda i, e, f: (i, 0), pipeline_mode=once),
            pl.BlockSpec((1, d), lambda i, e, f: (0, 0)),
            pl.BlockSpec((tm, LANES), lambda i, e, f: (i, 0), pipeline_mode=once),
            pl.BlockSpec((LANES, tm), lambda i, e, f: (0, i), pipeline_mode=once),
            pl.BlockSpec((1, d, tf), lambda i, e, f: (e, 0, f)),
            pl.BlockSpec((1, d, tf), lambda i, e, f: (e, 0, f)),
            pl.BlockSpec((1, tf, d), lambda i, e, f: (e, f, 0)),
            pl.BlockSpec((1, d), lambda i, e, f: (0, 0)),
        ],
        out_specs=pl.BlockSpec((tm, d), lambda i, e, f: (i, 0)),
        out_shape=jax.ShapeDtypeStruct((n, d), F32),
        scratch_shapes=[pltpu.VMEM((tm, d), BF16), pltpu.VMEM((tm, LANES), F32),
                        pltpu.VMEM((LANES, tm), F32), pltpu.VMEM((cap, d), BF16),
                        pltpu.VMEM((cap, d), F32)],
        compiler_params=_params(3),
    )(h, gain.reshape(1, d), comb, comb.T, w_gate, w_up, w_down, final_gain.reshape(1, d))


def _tri_inv_many(lows, nilpotency):
    c = lows[0].shape[0]
    ii = lax.broadcasted_iota(jnp.int32, (c, c), 0)
    jj = lax.broadcasted_iota(jnp.int32, (c, c), 1)
    eye = jnp.where(ii == jj, 1.0, 0.0)
    lb = [low.astype(BF16) for low in lows]
    ps = [_dot(b, b).astype(BF16) for b in lb]
    ts = [eye - low for low in lows]
    levels = int(math.log2(nilpotency)) - 1
    for level in range(levels):
        tb = [t.astype(BF16) for t in ts]
        if level + 1 < levels:
            nxt = [_dot(p, p).astype(BF16) for p in ps]
        ts = [t + _dot(b, p) for t, b, p in zip(ts, tb, ps)]
        if level + 1 < levels:
            ps = nxt
    return ts


def _gdn_local_kernel(alog_ref, dtb_ref, q_ref, k_ref, v_ref, cq_ref, ck_ref, cv_ref,
                      a1_ref, a2_ref, b2_ref,
                      u_ref, w_ref, qd_ref, kd_ref, aqk_ref, gl_ref,
                      xp_ref, qs_ref, ks_ref, vs_ref, gc1_ref, gc2_ref, gl2_ref, be2_ref, *,
                      seq, row_chunk):
    hd = pl.program_id(1)
    dk = q_ref.shape[1]
    c = GDN_CHUNK
    grp = GDN_GROUP
    n_groups = seq // grp
    pad = 8

    def conv_silu(x_ref, w_ref, dst_ref, normalise, scale):
        xp_ref[0:pad, :] = jnp.zeros((pad, dk), F32)
        xp_ref[pad:pad + seq, :] = x_ref[...].astype(F32)
        w = w_ref[...]
        for r in range(0, seq, row_chunk):
            y = xp_ref[pad - 3 + r:pad - 3 + r + row_chunk, :] * w[0:1, :]
            for j in range(1, GDN_CONV):
                y = y + xp_ref[pad - 3 + j + r:pad - 3 + j + r + row_chunk, :] * w[j:j + 1, :]
            y = y * _sigmoid(y)
            if normalise:
                y = y * lax.rsqrt(jnp.sum(y * y, axis=-1, keepdims=True) + NORM_EPS)
            if scale != 1.0:
                y = y * scale
            dst_ref[r:r + row_chunk, :] = y

    conv_silu(q_ref, cq_ref, qs_ref, True, dk ** -0.5)
    conv_silu(k_ref, ck_ref, ks_ref, True, 1.0)
    conv_silu(v_ref, cv_ref, vs_ref, False, 1.0)

    a_coef = -jnp.exp(alog_ref[hd])
    dtb = dtb_ref[hd]

    def log_decay(a):
        x = a + dtb
        return a_coef * (jnp.maximum(x, 0.0) + jnp.log(1.0 + jnp.exp(-jnp.abs(x))))

    ii = lax.broadcasted_iota(jnp.int32, (grp, grp), 0)
    jj = lax.broadcasted_iota(jnp.int32, (grp, grp), 1)
    same = (ii // c) == (jj // c)
    hi = lax.Precision.HIGHEST
    g1 = log_decay(a1_ref[0, 0])
    g2 = log_decay(a2_ref[0, 0])
    ones_b = jnp.where(same, 1.0, 0.0)
    gc1_ref[...] = jnp.dot(g1, jnp.where(same & (ii <= jj), 1.0, 0.0),
                           preferred_element_type=F32, precision=hi)
    gc2_ref[...] = jnp.dot(jnp.where(same & (jj <= ii), 1.0, 0.0), g2,
                           preferred_element_type=F32, precision=hi)
    gl2_ref[...] = jnp.dot(ones_b, g2, preferred_element_type=F32, precision=hi)
    gl_ref[0, 0] = jnp.exp(jnp.dot(g1, ones_b, preferred_element_type=F32, precision=hi))
    be2_ref[...] = _sigmoid(b2_ref[0, 0])

    causal = same & (ii >= jj)
    strict = same & (ii > jj)
    n_par = GDN_GROUPS_IN_FLIGHT

    def groups(it, carry):
        gis = [it * n_par + t for t in range(n_par)]
        rows = [pl.ds(pl.multiple_of(gi * grp, grp), grp) for gi in gis]
        q = [qs_ref[r, :] for r in rows]
        k = [ks_ref[r, :] for r in rows]
        v = [vs_ref[r, :] for r in rows]
        g_row = [gc1_ref[pl.ds(gi, 1), :] for gi in gis]
        g_col = [_pick_lane(gc2_ref[...], gi) for gi in gis]
        gl_col = [_pick_lane(gl2_ref[...], gi) for gi in gis]
        b_col = [_pick_lane(be2_ref[...], gi) for gi in gis]
        kbf = [x.astype(BF16) for x in k]
        kb = [x * b for x, b in zip(k, b_col)]
        kk = [_dot_nt(x.astype(BF16), y) for x, y in zip(kb, kbf)]
        qk = [_dot_nt(x.astype(BF16), y) for x, y in zip(q, kbf)]
        decay = [jnp.where(causal, jnp.exp(jnp.where(causal, gc - gr, 0.0)), 0.0)
                 for gc, gr in zip(g_col, g_row)]
        t_inv = _tri_inv_many([jnp.where(strict, x * d, 0.0) for x, d in zip(kk, decay)], c)
        e_col = [jnp.exp(gc) for gc in g_col]
        rhs = [jnp.concatenate([x * b, y * e], axis=1).astype(BF16)
               for x, b, y, e in zip(v, b_col, kb, e_col)]
        uw = [_dot(t.astype(BF16), r) for t, r in zip(t_inv, rhs)]
        for t in range(n_par):
            u_ref[rows[t], :] = uw[t][:, :dk].astype(u_ref.dtype)
            w_ref[rows[t], :] = uw[t][:, dk:].astype(w_ref.dtype)
            a_qk = qk[t] * decay[t]
            a_c = a_qk[:, 0:c]
            for blk in range(1, grp // c):
                a_c = a_c + a_qk[:, blk * c:(blk + 1) * c]
            aqk_ref[0, 0, rows[t], :] = a_c.astype(aqk_ref.dtype)
            qd_ref[rows[t], :] = (q[t] * e_col[t]).astype(qd_ref.dtype)
            kd_ref[rows[t], :] = (k[t] * jnp.exp(gl_col[t] - g_col[t])).astype(kd_ref.dtype)
        return carry

    lax.fori_loop(0, n_groups // n_par, groups, 0)


def _gdn_scan_kernel(gl_ref, u_ref, w_ref, qd_ref, kd_ref, aqk_ref, z_ref, on_ref, o_ref, *,
                     seq, heads_per_step):
    b = pl.program_id(0)
    hb = pl.program_id(1)
    c = GDN_CHUNK
    dk = on_ref.shape[1]

    def step(ci, states):
        rows = pl.ds(pl.multiple_of(ci * c, c), c)
        hs = range(heads_per_step)
        cols = [slice(hh * dk, (hh + 1) * dk) for hh in hs]
        ws_qs = [_dot(jnp.concatenate([w_ref[rows, cols[hh]], qd_ref[rows, cols[hh]]], axis=0),
                      states[hh].astype(BF16)) for hh in hs]
        v_new = [(u_ref[rows, cols[hh]].astype(F32) - ws_qs[hh][:c]).astype(BF16) for hh in hs]
        intra = [_dot(aqk_ref[0, hh, rows, :], v_new[hh]) for hh in hs]
        outer = [_dot_tn(kd_ref[rows, cols[hh]], v_new[hh]) for hh in hs]
        new_states = []
        for hh in hs:
            decay = gl_ref[b, hb * heads_per_step + hh, ci]
            new_states.append(states[hh] * decay + outer[hh])
            z = z_ref[rows, cols[hh]].astype(F32)
            o = ws_qs[hh][c:] + intra[hh]
            o_ref[rows, cols[hh]] = (_rms(o, on_ref[...]) * (z * _sigmoid(z))).astype(o_ref.dtype)
        return tuple(new_states)

    lax.fori_loop(0, seq // c, step,
                  tuple(jnp.zeros((dk, dk), F32) for _ in range(heads_per_step)))


def gdn_core(proj, gates, conv_w, a_log, dt_bias, out_norm, *, batch, seq):
    n = proj.shape[0]
    nh = GDN_HEADS
    dk = proj.shape[1] // (4 * nh)
    c = GDN_CHUNK
    grp = GDN_GROUP
    n_groups = seq // grp
    hps = GDN_HEADS_PER_STEP
    bl = gates[:, :nh].reshape(batch, seq, nh).transpose(0, 2, 1)
    al = gates[:, nh:2 * nh].reshape(batch, seq, nh).transpose(0, 2, 1)
    a1 = al.reshape(batch, nh, n_groups, grp)
    a2 = a1.transpose(0, 1, 3, 2)
    b2 = bl.reshape(batch, nh, n_groups, grp).transpose(0, 1, 3, 2)
    smem = pl.BlockSpec(memory_space=pltpu.SMEM)
    col = lambda off: pl.BlockSpec((seq, dk), lambda b, h, off=off: (b, off * nh + h))
    cw = lambda off: pl.BlockSpec((GDN_CONV, dk), lambda b, h, off=off: (0, off * nh + h))
    g1 = pl.BlockSpec((1, 1, n_groups, grp), lambda b, h: (b, h, 0, 0))
    g2 = pl.BlockSpec((1, 1, grp, n_groups), lambda b, h: (b, h, 0, 0))
    head_out = pl.BlockSpec((seq, dk), lambda b, h: (b, h))
    wide = jax.ShapeDtypeStruct((n, nh * dk), BF16)
    u, w, qd, kd, aqk, gl = pl.pallas_call(
        functools.partial(_gdn_local_kernel, seq=seq, row_chunk=256),
        grid=(batch, nh),
        in_specs=[smem, smem, col(0), col(1), col(2), cw(0), cw(1), cw(2), g1, g2, g2],
        out_specs=[head_out, head_out, head_out, head_out,
                   pl.BlockSpec((1, 1, seq, c), lambda b, h: (b, h, 0, 0)), g1],
        out_shape=[wide, wide, wide, wide,
                   jax.ShapeDtypeStruct((batch, nh, seq, c), BF16),
                   jax.ShapeDtypeStruct((batch, nh, n_groups, grp), F32)],
        scratch_shapes=[
            pltpu.VMEM((seq + 8, dk), F32),
            pltpu.VMEM((seq, dk), F32), pltpu.VMEM((seq, dk), F32), pltpu.VMEM((seq, dk), F32),
            pltpu.VMEM((n_groups, grp), F32), pltpu.VMEM((grp, n_groups), F32),
            pltpu.VMEM((grp, n_groups), F32), pltpu.VMEM((grp, n_groups), F32),
        ],
        compiler_params=_params(2),
    )(a_log, dt_bias, proj, proj, proj, conv_w, conv_w, conv_w, a1, a2, b2)
    chunk_decay = gl[..., c - 1::c].reshape(batch, nh, seq // c)
    heads = pl.BlockSpec((seq, hps * dk), lambda b, h: (b, h))
    return pl.pallas_call(
        functools.partial(_gdn_scan_kernel, seq=seq, heads_per_step=hps),
        grid=(batch, nh // hps),
        in_specs=[smem, heads, heads, heads, heads,
                  pl.BlockSpec((1, hps, seq, c), lambda b, h: (b, h, 0, 0)),
                  pl.BlockSpec((seq, hps * dk), lambda b, h: (b, 3 * (nh // hps) + h)),
                  pl.BlockSpec((1, dk), lambda b, h: (0, 0))],
        out_specs=heads,
        out_shape=wide,
        compiler_params=_params(2),
    )(chunk_decay, u, w, qd, kd, aqk, proj, out_norm.reshape(1, dk))


def _rope(x, cos, sin_a, sin_b):
    half = ROPE_DIM // 2
    return (x * cos + pltpu.roll(x, LANES - half, 1) * sin_a + pltpu.roll(x, half, 1) * sin_b)


def _kv_rope_kernel(x_ref, cos_ref, sa_ref, sb_ref, o_ref):
    j = pl.program_id(1)
    tensor = j // NSA_GROUPS
    x = x_ref[...]
    roped = _rope(x, cos_ref[...], sa_ref[...], sb_ref[...])
    o_ref[...] = jnp.where((tensor == 2) | (tensor == 4), roped, x).astype(o_ref.dtype)


def kv_rope(kv, cos, sin_a, sin_b, *, tm=1024):
    n, width = kv.shape
    tm = min(tm, n)
    tab = pl.BlockSpec((tm, LANES), lambda i, j: (i, 0))
    return pl.pallas_call(
        _kv_rope_kernel,
        grid=(n // tm, width // LANES),
        in_specs=[pl.BlockSpec((tm, LANES), lambda i, j: (i, j)), tab, tab, tab],
        out_specs=pl.BlockSpec((tm, LANES), lambda i, j: (i, j)),
        out_shape=jax.ShapeDtypeStruct((n, width), BF16),
        compiler_params=_params(2),
    )(kv, cos, sin_a, sin_b)


def _compress_kernel(kc_ref, vc_ref, pk_ref, pv_ref, w1k_ref, w1v_ref, w2k_ref, w2v_ref,
                     ok_ref, ov_ref, tp_ref, *, seq):
    dh = kc_ref.shape[1]
    n_rows = ok_ref.shape[2]
    hidden = w1k_ref.shape[1]

    def run(t_ref, pos_ref, w1_ref, w2_ref, out_ref):
        tp_ref[0:seq, :] = t_ref[...]
        tp_ref[seq:seq + CMP_STRIDE, :] = jnp.zeros((CMP_STRIDE, dh), F32)
        acc = jnp.zeros((n_rows, hidden), F32)
        for l in range(CMP_BLOCK):
            x = tp_ref[pl.ds(l, n_rows, stride=CMP_STRIDE), :] + pos_ref[l:l + 1, :]
            acc = acc + _dot(x.astype(BF16), w1_ref[l * dh:(l + 1) * dh, :])
        hid = acc * _sigmoid(acc)
        out_ref[0, 0] = _dot(hid.astype(BF16), w2_ref[...]).astype(out_ref.dtype)

    run(kc_ref, pk_ref, w1k_ref, w2k_ref, ok_ref)
    run(vc_ref, pv_ref, w1v_ref, w2v_ref, ov_ref)


def nsa_compress(kv, pos_k, pos_v, w1k, w1v, w2k, w2v, *, batch, seq):
    dh = pos_k.shape[1]
    n_rows = seq // CMP_STRIDE
    hidden = w1k.shape[1]
    full = lambda shape: pl.BlockSpec(shape, lambda b, g: (0,) * len(shape))
    out = pl.BlockSpec((1, 1, n_rows, dh), lambda b, g: (b, g, 0, 0))
    return pl.pallas_call(
        functools.partial(_compress_kernel, seq=seq),
        grid=(batch, NSA_GROUPS),
        in_specs=[
            pl.BlockSpec((seq, dh), lambda b, g: (b, g)),
            pl.BlockSpec((seq, dh), lambda b, g: (b, NSA_GROUPS + g)),
            full((CMP_BLOCK, dh)), full((CMP_BLOCK, dh)),
            full((CMP_BLOCK * dh, hidden)), full((CMP_BLOCK * dh, hidden)),
            full((hidden, dh)), full((hidden, dh)),
        ],
        out_specs=[out, out],
        out_shape=[jax.ShapeDtypeStruct((batch, NSA_GROUPS, n_rows, dh), BF16)] * 2,
        scratch_shapes=[pltpu.VMEM((seq + CMP_STRIDE, dh), F32)],
        compiler_params=_params(2),
    )(kv, kv, pos_k, pos_v, w1k, w1v, w2k, w2v)


def _masked_softmax(s, mask):
    sm = jnp.where(mask, s, NEG)
    m = jnp.max(sm, axis=-1, keepdims=True)
    p = jnp.where(mask, jnp.exp(sm - m), 0.0)
    return p / jnp.maximum(jnp.sum(p, axis=-1, keepdims=True), 1e-30)


def _nsa_kernel(q_ref, gate_ref, cos_ref, sa_ref, sb_ref, kc_ref, vc_ref, ks_ref, vs_ref,
                kw_ref, vw_ref, ovt_ref, eb_ref, wb_ref, o_ref, *, seq, tq, kc_len):
    g = pl.program_id(1)
    n = pl.program_id(2)
    t0 = n * tq
    dh = LANES
    hpg = NSA_HPG
    m_rows = hpg * tq
    n_sel = seq // SEL_BLOCK
    topk = min(SEL_TOPK, n_sel)
    scale = dh ** -0.5

    stack = lambda x: jnp.concatenate([x] * hpg, axis=0)
    q = q_ref[...]
    qs = jnp.concatenate([q[:, h * dh:(h + 1) * dh] for h in range(hpg)], axis=0) * scale
    qr = _rope(qs, stack(cos_ref[...]), stack(sa_ref[...]), stack(sb_ref[...]))
    qb = qs.astype(BF16)
    qrb = qr.astype(BF16)
    row = lax.broadcasted_iota(jnp.int32, (m_rows, 1), 0)
    tq_col = t0 + (row & (tq - 1))

    wk = WINDOW + tq
    start = pl.multiple_of(jnp.maximum(t0 - WINDOW, 0), LANES)
    case = jnp.minimum(n, WINDOW // tq)
    row_id = lax.broadcasted_iota(jnp.int32, (m_rows, LANES), 0) & (tq - 1)
    lane_id = lax.broadcasted_iota(jnp.int32, (m_rows, LANES), 1)
    lhs_w = jnp.concatenate([qrb, jnp.where(row_id == lane_id, 1.0, 0.0).astype(BF16)], axis=1)
    keys_w = jnp.concatenate([kw_ref[pl.ds(start, wk), :], wb_ref[case]], axis=1)
    s_c = _dot_nt(qb, kc_ref[0, 0])
    s_w = _dot_nt(lhs_w, keys_w)
    lane_c = lax.broadcasted_iota(jnp.int32, s_c.shape, 1)
    p_c = _masked_softmax(s_c, lane_c * CMP_STRIDE + (CMP_BLOCK - 1) <= tq_col)
    p_w = jnp.exp(s_w - jnp.max(s_w, axis=-1, keepdims=True))
    p_w = p_w / jnp.maximum(jnp.sum(p_w, axis=-1, keepdims=True), 1e-30)
    o_c = _dot(p_c.astype(BF16), vc_ref[0, 0])
    o_w = _dot(p_w.astype(BF16), vw_ref[pl.ds(start, wk), :])

    p_sum = p_c[0:tq]
    for h in range(1, hpg):
        p_sum = p_sum + p_c[h * tq:(h + 1) * tq]
    imp_t = lax.dot_general(ovt_ref[...], p_sum, (((1,), (1,)), ((), ())),
                            preferred_element_type=F32, precision=lax.Precision.HIGHEST)
    nsp = -(-n_sel // 8) * 8
    imp_t = imp_t[0:nsp]
    blk = lax.broadcasted_iota(jnp.int32, (nsp, tq), 0)
    t1 = t0 + lax.broadcasted_iota(jnp.int32, (nsp, tq), 1)
    cur = t1 // SEL_BLOCK
    forced = (blk == 0) | (blk == cur) | (blk == cur - 1)
    visible = blk * SEL_BLOCK <= t1
    score = jnp.where(forced, BIG, jnp.where(visible, imp_t, -BIG))
    score = jnp.where(blk < n_sel, score, -3e38)
    rank = jnp.zeros((nsp, tq), F32)
    for j2 in range(n_sel):
        other = score[j2:j2 + 1, :]
        rank = rank + jnp.where(other > score, 1.0,
                                jnp.where((other == score) & (blk > j2), 1.0, 0.0))
    chosen = (rank < topk) & visible & (blk < n_sel)
    bias_t = jnp.where(chosen, 0.0, NEG)
    if nsp < LANES:
        bias_t = jnp.concatenate([bias_t, jnp.full((LANES - nsp, tq), NEG, F32)], axis=0)
    sel_bias = stack(bias_t.T).astype(BF16)

    lhs_sel = jnp.concatenate([qrb, sel_bias], axis=1)

    def sel_step(ci, carry, diagonal):
        m_i, l_i, acc = carry
        k0 = pl.multiple_of(ci * kc_len, kc_len)
        keys = jnp.concatenate([ks_ref[pl.ds(k0, kc_len), :], eb_ref[pl.ds(k0, kc_len), :]], axis=1)
        s = _dot_nt(lhs_sel, keys)
        if diagonal:
            kpos = k0 + lax.broadcasted_iota(jnp.int32, (m_rows, kc_len), 1)
            s = jnp.where(kpos <= tq_col, s, NEG)
        m_new = jnp.maximum(m_i, jnp.max(s, axis=-1, keepdims=True))
        alpha = jnp.exp(m_i - m_new)
        p = jnp.exp(s - m_new)
        l_new = alpha * l_i + jnp.sum(p, axis=-1, keepdims=True)
        acc_new = alpha * acc + _dot(p.astype(BF16), vs_ref[pl.ds(k0, kc_len), :])
        return m_new, l_new, acc_new

    n_kc = (t0 + tq + kc_len - 1) // kc_len
    init = (jnp.full((m_rows, 1), NEG, F32), jnp.zeros((m_rows, 1), F32),
            jnp.zeros((m_rows, dh), F32))
    carry = lax.fori_loop(0, n_kc - 1, functools.partial(sel_step, diagonal=False), init)
    _, l_s, acc_s = sel_step(n_kc - 1, carry, True)
    o_s = acc_s / jnp.maximum(l_s, 1e-30)

    gates = _sigmoid(gate_ref[...])
    for h in range(hpg):
        base = (g * hpg + h) * NSA_N_BRANCH
        rows = slice(h * tq, (h + 1) * tq)
        out = (_pick_lane(gates, base) * o_c[rows] + _pick_lane(gates, base + 1) * o_s[rows]
               + _pick_lane(gates, base + 2) * o_w[rows])
        o_ref[:, h * dh:(h + 1) * dh] = out.astype(o_ref.dtype)


def nsa_attention(q, gates, cos, sin_a, sin_b, k_cmp, v_cmp, kvr, tables, *, batch, seq, kc_len=512):
    overlap_t, block_onehot, window_bias = tables
    n = q.shape[0]
    dh = LANES
    tq = LANES
    gw = NSA_HPG * dh
    nq = seq // tq
    kc_len = min(kc_len, seq)
    rows = lambda width: pl.BlockSpec((tq, width), lambda b, g, i: (b * nq + i, 0))
    cmp_spec = pl.BlockSpec((1, 1, k_cmp.shape[2], dh), lambda b, g, i: (b, g, 0, 0))
    kv_spec = lambda t: pl.BlockSpec((seq, dh), lambda b, g, i, t=t: (b, t * NSA_GROUPS + g))
    const = lambda shape: pl.BlockSpec(shape, lambda b, g, i: (0,) * len(shape))
    return pl.pallas_call(
        functools.partial(_nsa_kernel, seq=seq, tq=tq, kc_len=kc_len),
        grid=(batch, NSA_GROUPS, nq),
        in_specs=[
            pl.BlockSpec((tq, gw), lambda b, g, i: (b * nq + i, g)),
            rows(LANES), rows(LANES), rows(LANES), rows(LANES),
            cmp_spec, cmp_spec, kv_spec(2), kv_spec(3), kv_spec(4), kv_spec(5),
            const(overlap_t.shape), const(block_onehot.shape), const(window_bias.shape),
        ],
        out_specs=pl.BlockSpec((tq, gw), lambda b, g, i: (b * nq + i, g)),
        out_shape=jax.ShapeDtypeStruct((n, NSA_HEADS * dh), BF16),
        compiler_params=_params(3),
    )(q, gates, cos, sin_a, sin_b, k_cmp, v_cmp, kvr, kvr, kvr, kvr,
      overlap_t, block_onehot, window_bias)


def _rope_tables(positions):
    half = ROPE_DIM // 2
    inv = ROPE_THETA ** (-jnp.arange(0, ROPE_DIM, 2, dtype=F32) / ROPE_DIM)
    ang = positions.astype(F32).reshape(-1, 1) * inv
    cos, sin = jnp.cos(ang), jnp.sin(ang)
    n = ang.shape[0]
    ones = jnp.ones((n, LANES - ROPE_DIM), F32)
    zeros_h = jnp.zeros((n, half), F32)
    zeros_t = jnp.zeros((n, LANES - ROPE_DIM), F32)
    cos_t = jnp.concatenate([cos, cos, ones], axis=1)
    sin_a = jnp.concatenate([-sin, zeros_h, zeros_t], axis=1)
    sin_b = jnp.concatenate([zeros_h, sin, zeros_t], axis=1)
    return cos_t, sin_a, sin_b


def _overlap_table(seq):
    n_cmp = (seq - CMP_BLOCK) // CMP_STRIDE + 1
    n_sel = seq // SEL_BLOCK
    c_idx = jnp.arange(n_cmp)
    j_idx = jnp.arange(n_sel)
    ov = jnp.clip(jnp.minimum(c_idx[:, None] * CMP_STRIDE + CMP_BLOCK, (j_idx[None, :] + 1) * SEL_BLOCK)
                  - jnp.maximum(c_idx[:, None] * CMP_STRIDE, j_idx[None, :] * SEL_BLOCK), 0)
    ov = ov.astype(F32) / CMP_STRIDE
    return jnp.pad(ov, ((0, LANES - n_cmp), (0, LANES - n_sel)))


def _nsa_tables(seq):
    tq = LANES
    key = jnp.arange(seq)
    block_onehot = (key[:, None] // SEL_BLOCK == jnp.arange(LANES)[None, :]).astype(BF16)
    x = jnp.arange(WINDOW + tq)[None, :, None]
    t = (jnp.arange(WINDOW // tq + 1) * tq)[:, None, None] + jnp.arange(tq)[None, None, :]
    window_bias = jnp.where((x <= t) & (x > t - WINDOW), 0.0, NEG).astype(BF16)
    return _overlap_table(seq).T, block_onehot, window_bias


def _split_pad(w, main):
    extra = w[:, main:]
    return w[:, :main].astype(BF16), jnp.pad(extra, ((0, 0), (0, LANES - extra.shape[1]))).astype(BF16)


def kernel(x, positions, attn_norm, ffn_norm, final_norm, gdn_w_in, gdn_conv, gdn_a_log, gdn_dt_bias,
           gdn_out_norm, gdn_w_out, kv_norm, nsa_w_kv, cmp_pos_k, cmp_w1_k, cmp_w2_k, cmp_pos_v, cmp_w1_v,
           cmp_w2_v, nsa_w_q, nsa_w_o, ffn_w_gate, ffn_w_up, ffn_w_down, moe_router, moe_w_gate, moe_w_up,
           moe_w_down):
    batch, seq, d = x.shape
    depth = attn_norm.shape[0]
    n_a = depth // 2
    h = x.reshape(batch * seq, d)
    gdn_width = gdn_w_out.shape[1]
    nsa_width = nsa_w_o.shape[1]
    shared = None
    for l in range(depth):
        if l < n_a:
            w_main, w_gates = _split_pad(gdn_w_in[l], 4 * gdn_width)
            proj, gates = rms_matmul(h, attn_norm[l], w_main, w_gates, out_dtype=BF16)
            o = gdn_core(proj, gates, gdn_conv[l], gdn_a_log[l], gdn_dt_bias[l], gdn_out_norm[l],
                         batch=batch, seq=seq)
            h = matmul_residual(o, gdn_w_out[l].astype(BF16), h)
        else:
            cos, sin_a, sin_b, k_cmp, v_cmp, kvr, tables = shared
            w_main, w_gates = _split_pad(nsa_w_q[l - n_a], nsa_width)
            q, gates = rms_matmul(h, attn_norm[l], w_main, w_gates)
            o = nsa_attention(q, gates, cos, sin_a, sin_b, k_cmp, v_cmp, kvr, tables,
                              batch=batch, seq=seq)
            h = matmul_residual(o, nsa_w_o[l - n_a].astype(BF16), h)
        final_gain = final_norm if l == depth - 1 else None
        if l % 2 == 0:
            i = l // 2
            h = ffn_residual(h, ffn_norm[l], None, ffn_w_gate[i][None], ffn_w_up[i][None],
                             ffn_w_down[i][None], final_gain)
        else:
            i = l // 2
            comb = route_tokens(h, ffn_norm[l], moe_router[i])
            h = moe_residual(h, ffn_norm[l], comb, moe_w_gate[i], moe_w_up[i], moe_w_down[i], final_gain)
        if l == n_a - 1:
            cos, sin_a, sin_b = _rope_tables(positions)
            kv, _ = rms_matmul(h, kv_norm, nsa_w_kv.astype(BF16), jnp.zeros((d, LANES), BF16))
            kvr = kv_rope(kv, cos, sin_a, sin_b)
            k_cmp, v_cmp = nsa_compress(kv, cmp_pos_k, cmp_pos_v, cmp_w1_k.astype(BF16),
                                        cmp_w1_v.astype(BF16), cmp_w2_k.astype(BF16),
                                        cmp_w2_v.astype(BF16), batch=batch, seq=seq)
            shared = (cos, sin_a, sin_b, k_cmp, v_cmp, kvr, _nsa_tables(seq))
    return h.reshape(batch, seq, d)
```

```python
import functools
import math

import jax
import jax.numpy as jnp
from jax import lax
from jax.experimental import pallas as pl
from jax.experimental.pallas import tpu as pltpu

F32 = jnp.float32
BF16 = jnp.bfloat16

NORM_EPS = 1e-6
LANES = 128
GDN_HEADS = 16
GDN_CONV = 4
GDN_CHUNK = 64
GDN_GROUP = 256
GDN_GROUPS_IN_FLIGHT = 4
GDN_HEADS_PER_STEP = 4
NSA_HEADS = 16
NSA_GROUPS = 4
NSA_HPG = NSA_HEADS // NSA_GROUPS
NSA_N_BRANCH = 3
CMP_BLOCK = 32
CMP_STRIDE = 16
SEL_BLOCK = 64
SEL_TOPK = 16
WINDOW = 512
BIG = 1e9
NEG = -1e30
ROPE_THETA = 500000.0
ROPE_DIM = 32
TOP_K = 2

VMEM_LIMIT_BYTES = 56 * 1024 * 1024


def _params(n_axes):
    return pltpu.CompilerParams(dimension_semantics=("arbitrary",) * n_axes,
                                vmem_limit_bytes=VMEM_LIMIT_BYTES)


def _rms(x, gain):
    ms = jnp.mean(x * x, axis=-1, keepdims=True)
    return x * lax.rsqrt(ms + NORM_EPS) * gain


def _sigmoid(x):
    return 1.0 / (1.0 + jnp.exp(-x))


def _dot(a, b):
    return jnp.dot(a, b, preferred_element_type=F32)


def _dot_nt(a, b):
    return lax.dot_general(a, b, (((1,), (1,)), ((), ())), preferred_element_type=F32)


def _dot_tn(a, b):
    return lax.dot_general(a, b, (((0,), (0,)), ((), ())), preferred_element_type=F32)


def _pick_lane(x, idx):
    lane = lax.broadcasted_iota(jnp.int32, x.shape, 1)
    return jnp.sum(jnp.where(lane == idx, x, 0.0), axis=-1, keepdims=True)


def _rms_matmul_kernel(h_ref, g_ref, w_ref, we_ref, o_ref, oe_ref, xn_ref, *, row_chunk):
    j = pl.program_id(1)

    @pl.when(j == 0)
    def _():
        for r in range(0, h_ref.shape[0], row_chunk):
            xn_ref[r:r + row_chunk, :] = _rms(h_ref[r:r + row_chunk, :], g_ref[...]).astype(BF16)
        oe_ref[...] = _dot(xn_ref[...], we_ref[...])

    o_ref[...] = _dot(xn_ref[...], w_ref[...]).astype(o_ref.dtype)


def rms_matmul(h, gain, w, w_extra, *, tm=1024, tn=1024, out_dtype=F32):
    n, d = h.shape
    nm = w.shape[1]
    tm = min(tm, n)
    tn = min(tn, nm)
    return pl.pallas_call(
        functools.partial(_rms_matmul_kernel, row_chunk=256),
        grid=(n // tm, nm // tn),
        in_specs=[
            pl.BlockSpec((tm, d), lambda i, j: (i, 0)),
            pl.BlockSpec((1, d), lambda i, j: (0, 0)),
            pl.BlockSpec((d, tn), lambda i, j: (0, j)),
            pl.BlockSpec((d, LANES), lambda i, j: (0, 0)),
        ],
        out_specs=[
            pl.BlockSpec((tm, tn), lambda i, j: (i, j)),
            pl.BlockSpec((tm, LANES), lambda i, j: (i, 0)),
        ],
        out_shape=[jax.ShapeDtypeStruct((n, nm), out_dtype),
                   jax.ShapeDtypeStruct((n, LANES), F32)],
        scratch_shapes=[pltpu.VMEM((tm, d), BF16)],
        compiler_params=_params(2),
    )(h, gain.reshape(1, d), w, w_extra)


def _matmul_res_kernel(x_ref, w_ref, r_ref, o_ref):
    o_ref[...] = r_ref[...] + _dot(x_ref[...], w_ref[...])


def matmul_residual(x, w, res, *, tm=1024, tn=512):
    n, k = x.shape
    nm = w.shape[1]
    tm = min(tm, n)
    return pl.pallas_call(
        _matmul_res_kernel,
        grid=(n // tm, nm // tn),
        in_specs=[
            pl.BlockSpec((tm, k), lambda i, j: (i, 0)),
            pl.BlockSpec((k, tn), lambda i, j: (0, j)),
            pl.BlockSpec((tm, tn), lambda i, j: (i, j)),
        ],
        out_specs=pl.BlockSpec((tm, tn), lambda i, j: (i, j)),
        out_shape=jax.ShapeDtypeStruct((n, nm), F32),
        compiler_params=_params(2),
    )(x, w, res)


def _router_kernel(h_ref, g_ref, r_ref, o_ref, *, n_experts, row_chunk):
    for r in range(0, h_ref.shape[0], row_chunk):
        xn = _rms(h_ref[r:r + row_chunk, :], g_ref[...])
        logits = jnp.dot(xn, r_ref[...], preferred_element_type=F32,
                         precision=lax.Precision.HIGHEST)
        lane = lax.broadcasted_iota(jnp.int32, logits.shape, 1)
        valid = lane < n_experts
        z = jnp.where(valid, logits, NEG)
        z = z - jnp.max(z, axis=-1, keepdims=True)
        p = jnp.where(valid, jnp.exp(z), 0.0)
        p = p / jnp.sum(p, axis=-1, keepdims=True)
        p = jnp.where(valid, p, -1.0)
        p1 = jnp.max(p, axis=-1, keepdims=True)
        i1 = jnp.min(jnp.where(p == p1, lane, LANES), axis=-1, keepdims=True)
        pm = jnp.where(lane == i1, -1.0, p)
        p2 = jnp.max(pm, axis=-1, keepdims=True)
        i2 = jnp.min(jnp.where(pm == p2, lane, LANES), axis=-1, keepdims=True)
        den = p1 + p2
        o_ref[r:r + row_chunk, :] = (jnp.where(lane == i1, p1 / den, 0.0)
                                     + jnp.where(lane == i2, p2 / den, 0.0))


def route_tokens(h, gain, router, *, tm=1024):
    n, d = h.shape
    n_experts = router.shape[1]
    tm = min(tm, n)
    r_pad = jnp.pad(router, ((0, 0), (0, LANES - n_experts)))
    return pl.pallas_call(
        functools.partial(_router_kernel, n_experts=n_experts, row_chunk=256),
        grid=(n // tm,),
        in_specs=[
            pl.BlockSpec((tm, d), lambda i: (i, 0)),
            pl.BlockSpec((1, d), lambda i: (0, 0)),
            pl.BlockSpec((d, LANES), lambda i: (0, 0)),
        ],
        out_specs=pl.BlockSpec((tm, LANES), lambda i: (i, 0)),
        out_shape=jax.ShapeDtypeStruct((n, LANES), F32),
        compiler_params=_params(1),
    )(h, gain.reshape(1, d), r_pad)


def _ffn_kernel(h_ref, g_ref, c_ref, wg_ref, wu_ref, wd_ref, fg_ref, o_ref, xn_ref, *,
                n_e, n_f, use_comb, final_norm, row_chunk):
    e = pl.program_id(1)
    f = pl.program_id(2)

    @pl.when((e == 0) & (f == 0))
    def _():
        for r in range(0, h_ref.shape[0], row_chunk):
            x = h_ref[r:r + row_chunk, :]
            xn_ref[r:r + row_chunk, :] = _rms(x, g_ref[...]).astype(BF16)
            o_ref[r:r + row_chunk, :] = x

    xn = xn_ref[...]
    gate = _dot(xn, wg_ref[0])
    up = _dot(xn, wu_ref[0])
    act = gate * _sigmoid(gate) * up
    if use_comb:
        act = act * _pick_lane(c_ref[...], e)
    o_ref[...] += _dot(act.astype(BF16), wd_ref[0])

    if final_norm:
        @pl.when((e == n_e - 1) & (f == n_f - 1))
        def _():
            for r in range(0, h_ref.shape[0], row_chunk):
                o_ref[r:r + row_chunk, :] = _rms(o_ref[r:r + row_chunk, :], fg_ref[...])


def ffn_residual(h, gain, comb, w_gate, w_up, w_down, final_gain, *, tm=512, tf=512):
    n, d = h.shape
    n_e, _, ff = w_gate.shape
    w_gate, w_up, w_down = w_gate.astype(BF16), w_up.astype(BF16), w_down.astype(BF16)
    tm = min(tm, n)
    use_comb = comb is not None
    final_norm = final_gain is not None
    if comb is None:
        comb = jnp.ones((n, LANES), F32)
    if final_gain is None:
        final_gain = jnp.ones((d,), F32)
    n_f = ff // tf
    return pl.pallas_call(
        functools.partial(_ffn_kernel, n_e=n_e, n_f=n_f, use_comb=use_comb,
                          final_norm=final_norm, row_chunk=256),
        grid=(n // tm, n_e, n_f),
        in_specs=[
            pl.BlockSpec((tm, d), lambda i, e, f: (i, 0), pipeline_mode=pl.Buffered(1)),
            pl.BlockSpec((1, d), lambda i, e, f: (0, 0)),
            pl.BlockSpec((tm, LANES), lambda i, e, f: (i, 0), pipeline_mode=pl.Buffered(1)),
            pl.BlockSpec((1, d, tf), lambda i, e, f: (e, 0, f)),
            pl.BlockSpec((1, d, tf), lambda i, e, f: (e, 0, f)),
            pl.BlockSpec((1, tf, d), lambda i, e, f: (e, f, 0)),
            pl.BlockSpec((1, d), lambda i, e, f: (0, 0)),
        ],
        out_specs=pl.BlockSpec((tm, d), lambda i, e, f: (i, 0)),
        out_shape=jax.ShapeDtypeStruct((n, d), F32),
        scratch_shapes=[pltpu.VMEM((tm, d), BF16)],
        compiler_params=_params(3),
    )(h, gain.reshape(1, d), comb, w_gate, w_up, w_down, final_gain.reshape(1, d))


def _moe_kernel(h_ref, g_ref, c_ref, ct_ref, wg_ref, wu_ref, wd_ref, fg_ref, o_ref,
                xn_ref, rc_ref, rr_ref, xs_ref, ye_ref, *, n_e, n_f, cap, final_norm, row_chunk):
    e = pl.program_id(1)
    f = pl.program_id(2)
    tm, d = h_ref.shape
    col_chunk = 512

    @pl.when((e == 0) & (f == 0))
    def _():
        for r in range(0, tm, row_chunk):
            x = h_ref[r:r + row_chunk, :]
            xn_ref[r:r + row_chunk, :] = _rms(x, g_ref[...]).astype(BF16)
            o_ref[r:r + row_chunk, :] = x
        ti = lax.broadcasted_iota(jnp.int32, (tm, tm), 0)
        tj = lax.broadcasted_iota(jnp.int32, (tm, tm), 1)
        routed_c = jnp.where(c_ref[...] > 0.0, 1.0, 0.0).astype(BF16)
        routed_r = jnp.where(ct_ref[...] > 0.0, 1.0, 0.0).astype(BF16)
        rc_ref[...] = _dot(jnp.where(tj < ti, 1.0, 0.0).astype(BF16), routed_c)
        rr_ref[...] = _dot(routed_r, jnp.where(ti < tj, 1.0, 0.0).astype(BF16))

    key_row = jnp.where(ct_ref[pl.ds(e, 1), :] > 0.0, rr_ref[pl.ds(e, 1), :], -1.0)
    w_row = ct_ref[pl.ds(e, 1), :]
    n_routed = jnp.sum(jnp.where(key_row >= 0.0, 1.0, 0.0)).astype(jnp.int32)
    n_rounds = (n_routed + cap - 1) // cap

    def gather_mask(rnd):
        slot = lax.broadcasted_iota(jnp.int32, (cap, tm), 0) + rnd * cap
        return key_row == slot.astype(F32)

    def expert(xs):
        gate = _dot(xs, wg_ref[0])
        up = _dot(xs, wu_ref[0])
        return _dot((gate * _sigmoid(gate) * up).astype(BF16), wd_ref[0])

    def scatter_add(rnd, y):
        picked = gather_mask(rnd)
        w_slot = jnp.sum(jnp.where(picked, w_row, 0.0), axis=-1, keepdims=True)
        yw = (y * w_slot).astype(BF16)
        key_col = jnp.where(_pick_lane(c_ref[...], e) > 0.0, _pick_lane(rc_ref[...], e), -1.0)
        slot = lax.broadcasted_iota(jnp.int32, (tm, cap), 1) + rnd * cap
        back = jnp.where(key_col == slot.astype(F32), 1.0, 0.0).astype(BF16)
        for c0 in range(0, d, col_chunk):
            o_ref[:, c0:c0 + col_chunk] += _dot(back, yw[:, c0:c0 + col_chunk])

    @pl.when(f == 0)
    def _():
        xs_ref[...] = _dot(jnp.where(gather_mask(0), 1.0, 0.0).astype(BF16), xn_ref[...]).astype(BF16)

    part = expert(xs_ref[...])

    @pl.when(f == 0)
    def _():
        ye_ref[...] = part

    @pl.when(f > 0)
    def _():
        ye_ref[...] += part

    @pl.when(f == n_f - 1)
    def _():
        scatter_add(0, ye_ref[...])

    def extra_round(rnd, carry):
        xs = _dot(jnp.where(gather_mask(rnd), 1.0, 0.0).astype(BF16), xn_ref[...]).astype(BF16)
        scatter_add(rnd, expert(xs))
        return carry

    lax.fori_loop(1, n_rounds, extra_round, 0)

    if final_norm:
        @pl.when((e == n_e - 1) & (f == n_f - 1))
        def _():
            for r in range(0, tm, row_chunk):
                o_ref[r:r + row_chunk, :] = _rms(o_ref[r:r + row_chunk, :], fg_ref[...])


def moe_residual(h, gain, comb, w_gate, w_up, w_down, final_gain, *, tm=1024, tf=512, cap=320):
    n, d = h.shape
    n_e, _, ff = w_gate.shape
    w_gate, w_up, w_down = w_gate.astype(BF16), w_up.astype(BF16), w_down.astype(BF16)
    tm = min(tm, n)
    final_norm = final_gain is not None
    if final_gain is None:
        final_gain = jnp.ones((d,), F32)
    n_f = ff // tf
    once = pl.Buffered(1)
    return pl.pallas_call(
        functools.partial(_moe_kernel, n_e=n_e, n_f=n_f, cap=cap, final_norm=final_norm,
                          row_chunk=256),
        grid=(n // tm, n_e, n_f),
        in_specs=[
            pl.BlockSpec((tm, d), lambda i, e, f: (i, 0), pipeline_mode=once),
            pl.BlockSpec((1, d), lambda i, e, f: (0, 0)),
            pl.BlockSpec((tm, LANES), lambda i, e, f: (i, 0), pipeline_mode=once),
            pl.BlockSpec((LANES, tm), lambda i, e, f: (0, i), pipeline_mode=once),
            pl.BlockSpec((1, d, tf), lambda i, e, f: (e, 0, f)),
            pl.BlockSpec((1, d, tf), lambda i, e, f: (e, 0, f)),
            pl.BlockSpec((1, tf, d), lambda i, e, f: (e, f, 0)),
            pl.BlockSpec((1, d), lambda i, e, f: (0, 0)),
        ],
        out_specs=pl.BlockSpec((tm, d), lambda i, e, f: (i, 0)),
        out_shape=jax.ShapeDtypeStruct((n, d), F32),
        scratch_shapes=[pltpu.VMEM((tm, d), BF16), pltpu.VMEM((tm, LANES), F32),
                        pltpu.VMEM((LANES, tm), F32), pltpu.VMEM((cap, d), BF16),
                        pltpu.VMEM((cap, d), F32)],
        compiler_params=_params(3),
    )(h, gain.reshape(1, d), comb, comb.T, w_gate, w_up, w_down, final_gain.reshape(1, d))


def _tri_inv_many(lows, nilpotency):
    c = lows[0].shape[0]
    ii = lax.broadcasted_iota(jnp.int32, (c, c), 0)
    jj = lax.broadcasted_iota(jnp.int32, (c, c), 1)
    eye = jnp.where(ii == jj, 1.0, 0.0)
    lb = [low.astype(BF16) for low in lows]
    ps = [_dot(b, b).astype(BF16) for b in lb]
    ts = [eye - low for low in lows]
    levels = int(math.log2(nilpotency)) - 1
    for level in range(levels):
        tb = [t.astype(BF16) for t in ts]
        if level + 1 < levels:
            nxt = [_dot(p, p).astype(BF16) for p in ps]
        ts = [t + _dot(b, p) for t, b, p in zip(ts, tb, ps)]
        if level + 1 < levels:
            ps = nxt
    return ts


def _gdn_local_kernel(alog_ref, dtb_ref, q_ref, k_ref, v_ref, cq_ref, ck_ref, cv_ref,
                      a1_ref, a2_ref, b2_ref,
                      u_ref, w_ref, qd_ref, kd_ref, aqk_ref, gl_ref,
                      xp_ref, qs_ref, ks_ref, vs_ref, gc1_ref, gc2_ref, gl2_ref, be2_ref, *,
                      seq, row_chunk):
    hd = pl.program_id(1)
    dk = q_ref.shape[1]
    c = GDN_CHUNK
    grp = GDN_GROUP
    n_groups = seq // grp
    pad = 8

    def conv_silu(x_ref, w_ref, dst_ref, normalise, scale):
        xp_ref[0:pad, :] = jnp.zeros((pad, dk), F32)
        xp_ref[pad:pad + seq, :] = x_ref[...].astype(F32)
        w = w_ref[...]
        for r in range(0, seq, row_chunk):
            y = xp_ref[pad - 3 + r:pad - 3 + r + row_chunk, :] * w[0:1, :]
            for j in range(1, GDN_CONV):
                y = y + xp_ref[pad - 3 + j + r:pad - 3 + j + r + row_chunk, :] * w[j:j + 1, :]
            y = y * _sigmoid(y)
            if normalise:
                y = y * lax.rsqrt(jnp.sum(y * y, axis=-1, keepdims=True) + NORM_EPS)
            if scale != 1.0:
                y = y * scale
            dst_ref[r:r + row_chunk, :] = y

    conv_silu(q_ref, cq_ref, qs_ref, True, dk ** -0.5)
    conv_silu(k_ref, ck_ref, ks_ref, True, 1.0)
    conv_silu(v_ref, cv_ref, vs_ref, False, 1.0)

    a_coef = -jnp.exp(alog_ref[hd])
    dtb = dtb_ref[hd]

    def log_decay(a):
        x = a + dtb
        return a_coef * (jnp.maximum(x, 0.0) + jnp.log(1.0 + jnp.exp(-jnp.abs(x))))

    ii = lax.broadcasted_iota(jnp.int32, (grp, grp), 0)
    jj = lax.broadcasted_iota(jnp.int32, (grp, grp), 1)
    same = (ii // c) == (jj // c)
    hi = lax.Precision.HIGHEST
    g1 = log_decay(a1_ref[0, 0])
    g2 = log_decay(a2_ref[0, 0])
    ones_b = jnp.where(same, 1.0, 0.0)
    gc1_ref[...] = jnp.dot(g1, jnp.where(same & (ii <= jj), 1.0, 0.0),
                           preferred_element_type=F32, precision=hi)
    gc2_ref[...] = jnp.dot(jnp.where(same & (jj <= ii), 1.0, 0.0), g2,
                           preferred_element_type=F32, precision=hi)
    gl2_ref[...] = jnp.dot(ones_b, g2, preferred_element_type=F32, precision=hi)
    gl_ref[0, 0] = jnp.exp(jnp.dot(g1, ones_b, preferred_element_type=F32, precision=hi))
    be2_ref[...] = _sigmoid(b2_ref[0, 0])

    causal = same & (ii >= jj)
    strict = same & (ii > jj)
    n_par = GDN_GROUPS_IN_FLIGHT

    def groups(it, carry):
        gis = [it * n_par + t for t in range(n_par)]
        rows = [pl.ds(pl.multiple_of(gi * grp, grp), grp) for gi in gis]
        q = [qs_ref[r, :] for r in rows]
        k = [ks_ref[r, :] for r in rows]
        v = [vs_ref[r, :] for r in rows]
        g_row = [gc1_ref[pl.ds(gi, 1), :] for gi in gis]
        g_col = [_pick_lane(gc2_ref[...], gi) for gi in gis]
        gl_col = [_pick_lane(gl2_ref[...], gi) for gi in gis]
        b_col = [_pick_lane(be2_ref[...], gi) for gi in gis]
        kbf = [x.astype(BF16) for x in k]
        kb = [x * b for x, b in zip(k, b_col)]
        kk = [_dot_nt(x.astype(BF16), y) for x, y in zip(kb, kbf)]
        qk = [_dot_nt(x.astype(BF16), y) for x, y in zip(q, kbf)]
        decay = [jnp.where(causal, jnp.exp(jnp.where(causal, gc - gr, 0.0)), 0.0)
                 for gc, gr in zip(g_col, g_row)]
        t_inv = _tri_inv_many([jnp.where(strict, x * d, 0.0) for x, d in zip(kk, decay)], c)
        e_col = [jnp.exp(gc) for gc in g_col]
        rhs = [jnp.concatenate([x * b, y * e], axis=1).astype(BF16)
               for x, b, y, e in zip(v, b_col, kb, e_col)]
        uw = [_dot(t.astype(BF16), r) for t, r in zip(t_inv, rhs)]
        for t in range(n_par):
            u_ref[rows[t], :] = uw[t][:, :dk].astype(u_ref.dtype)
            w_ref[rows[t], :] = uw[t][:, dk:].astype(w_ref.dtype)
            a_qk = qk[t] * decay[t]
            a_c = a_qk[:, 0:c]
            for blk in range(1, grp // c):
                a_c = a_c + a_qk[:, blk * c:(blk + 1) * c]
            aqk_ref[0, 0, rows[t], :] = a_c.astype(aqk_ref.dtype)
            qd_ref[rows[t], :] = (q[t] * e_col[t]).astype(qd_ref.dtype)
            kd_ref[rows[t], :] = (k[t] * jnp.exp(gl_col[t] - g_col[t])).astype(kd_ref.dtype)
        return carry

    lax.fori_loop(0, n_groups // n_par, groups, 0)


def _gdn_scan_kernel(gl_ref, u_ref, w_ref, qd_ref, kd_ref, aqk_ref, z_ref, on_ref, o_ref, *,
                     seq, heads_per_step):
    b = pl.program_id(0)
    hb = pl.program_id(1)
    c = GDN_CHUNK
    dk = on_ref.shape[1]

    def step(ci, states):
        rows = pl.ds(pl.multiple_of(ci * c, c), c)
        hs = range(heads_per_step)
        cols = [slice(hh * dk, (hh + 1) * dk) for hh in hs]
        ws_qs = [_dot(jnp.concatenate([w_ref[rows, cols[hh]], qd_ref[rows, cols[hh]]], axis=0),
                      states[hh].astype(BF16)) for hh in hs]
        v_new = [(u_ref[rows, cols[hh]].astype(F32) - ws_qs[hh][:c]).astype(BF16) for hh in hs]
        intra = [_dot(aqk_ref[0, hh, rows, :], v_new[hh]) for hh in hs]
        outer = [_dot_tn(kd_ref[rows, cols[hh]], v_new[hh]) for hh in hs]
        new_states = []
        for hh in hs:
            decay = gl_ref[b, hb * heads_per_step + hh, ci]
            new_states.append(states[hh] * decay + outer[hh])
            z = z_ref[rows, cols[hh]].astype(F32)
            o = ws_qs[hh][c:] + intra[hh]
            o_ref[rows, cols[hh]] = (_rms(o, on_ref[...]) * (z * _sigmoid(z))).astype(o_ref.dtype)
        return tuple(new_states)

    lax.fori_loop(0, seq // c, step,
                  tuple(jnp.zeros((dk, dk), F32) for _ in range(heads_per_step)))


def gdn_core(proj, gates, conv_w, a_log, dt_bias, out_norm, *, batch, seq):
    n = proj.shape[0]
    nh = GDN_HEADS
    dk = proj.shape[1] // (4 * nh)
    c = GDN_CHUNK
    grp = GDN_GROUP
    n_groups = seq // grp
    hps = GDN_HEADS_PER_STEP
    bl = gates[:, :nh].reshape(batch, seq, nh).transpose(0, 2, 1)
    al = gates[:, nh:2 * nh].reshape(batch, seq, nh).transpose(0, 2, 1)
    a1 = al.reshape(batch, nh, n_groups, grp)
    a2 = a1.transpose(0, 1, 3, 2)
    b2 = bl.reshape(batch, nh, n_groups, grp).transpose(0, 1, 3, 2)
    smem = pl.BlockSpec(memory_space=pltpu.SMEM)
    col = lambda off: pl.BlockSpec((seq, dk), lambda b, h, off=off: (b, off * nh + h))
    cw = lambda off: pl.BlockSpec((GDN_CONV, dk), lambda b, h, off=off: (0, off * nh + h))
    g1 = pl.BlockSpec((1, 1, n_groups, grp), lambda b, h: (b, h, 0, 0))
    g2 = pl.BlockSpec((1, 1, grp, n_groups), lambda b, h: (b, h, 0, 0))
    head_out = pl.BlockSpec((seq, dk), lambda b, h: (b, h))
    wide = jax.ShapeDtypeStruct((n, nh * dk), BF16)
    u, w, qd, kd, aqk, gl = pl.pallas_call(
        functools.partial(_gdn_local_kernel, seq=seq, row_chunk=256),
        grid=(batch, nh),
        in_specs=[smem, smem, col(0), col(1), col(2), cw(0), cw(1), cw(2), g1, g2, g2],
        out_specs=[head_out, head_out, head_out, head_out,
                   pl.BlockSpec((1, 1, seq, c), lambda b, h: (b, h, 0, 0)), g1],
        out_shape=[wide, wide, wide, wide,
                   jax.ShapeDtypeStruct((batch, nh, seq, c), BF16),
                   jax.ShapeDtypeStruct((batch, nh, n_groups, grp), F32)],
        scratch_shapes=[
            pltpu.VMEM((seq + 8, dk), F32),
            pltpu.VMEM((seq, dk), F32), pltpu.VMEM((seq, dk), F32), pltpu.VMEM((seq, dk), F32),
            pltpu.VMEM((n_groups, grp), F32), pltpu.VMEM((grp, n_groups), F32),
            pltpu.VMEM((grp, n_groups), F32), pltpu.VMEM((grp, n_groups), F32),
        ],
        compiler_params=_params(2),
    )(a_log, dt_bias, proj, proj, proj, conv_w, conv_w, conv_w, a1, a2, b2)
    chunk_decay = gl[..., c - 1::c].reshape(batch, nh, seq // c)
    heads = pl.BlockSpec((seq, hps * dk), lambda b, h: (b, h))
    return pl.pallas_call(
        functools.partial(_gdn_scan_kernel, seq=seq, heads_per_step=hps),
        grid=(batch, nh // hps),
        in_specs=[smem, heads, heads, heads, heads,
                  pl.BlockSpec((1, hps, seq, c), lambda b, h: (b, h, 0, 0)),
                  pl.BlockSpec((seq, hps * dk), lambda b, h: (b, 3 * (nh // hps) + h)),
                  pl.BlockSpec((1, dk), lambda b, h: (0, 0))],
        out_specs=heads,
        out_shape=wide,
        compiler_params=_params(2),
    )(chunk_decay, u, w, qd, kd, aqk, proj, out_norm.reshape(1, dk))


def _rope(x, cos, sin_a, sin_b):
    half = ROPE_DIM // 2
    return (x * cos + pltpu.roll(x, LANES - half, 1) * sin_a + pltpu.roll(x, half, 1) * sin_b)


def _compress_kernel(kc_ref, vc_ref, pk_ref, pv_ref, w1k_ref, w1v_ref, w2k_ref, w2v_ref,
                     ok_ref, ov_ref, tp_ref, *, seq):
    dh = kc_ref.shape[1]
    n_rows = ok_ref.shape[2]
    hidden = w1k_ref.shape[1]

    def run(t_ref, pos_ref, w1_ref, w2_ref, out_ref):
        tp_ref[0:seq, :] = t_ref[...].astype(F32)
        tp_ref[seq:seq + CMP_STRIDE, :] = jnp.zeros((CMP_STRIDE, dh), F32)
        acc = jnp.zeros((n_rows, hidden), F32)
        for l in range(CMP_BLOCK):
            x = tp_ref[pl.ds(l, n_rows, stride=CMP_STRIDE), :] + pos_ref[l:l + 1, :]
            acc = acc + _dot(x.astype(BF16), w1_ref[l * dh:(l + 1) * dh, :])
        hid = acc * _sigmoid(acc)
        out_ref[0, 0] = _dot(hid.astype(BF16), w2_ref[...]).astype(out_ref.dtype)

    run(kc_ref, pk_ref, w1k_ref, w2k_ref, ok_ref)
    run(vc_ref, pv_ref, w1v_ref, w2v_ref, ov_ref)


def nsa_compress(kv, pos_k, pos_v, w1k, w1v, w2k, w2v, *, batch, seq):
    dh = pos_k.shape[1]
    n_rows = seq // CMP_STRIDE
    hidden = w1k.shape[1]
    full = lambda shape: pl.BlockSpec(shape, lambda b, g: (0,) * len(shape))
    out = pl.BlockSpec((1, 1, n_rows, dh), lambda b, g: (b, g, 0, 0))
    return pl.pallas_call(
        functools.partial(_compress_kernel, seq=seq),
        grid=(batch, NSA_GROUPS),
        in_specs=[
            pl.BlockSpec((seq, dh), lambda b, g: (b, g)),
            pl.BlockSpec((seq, dh), lambda b, g: (b, NSA_GROUPS + g)),
            full((CMP_BLOCK, dh)), full((CMP_BLOCK, dh)),
            full((CMP_BLOCK * dh, hidden)), full((CMP_BLOCK * dh, hidden)),
            full((hidden, dh)), full((hidden, dh)),
        ],
        out_specs=[out, out],
        out_shape=[jax.ShapeDtypeStruct((batch, NSA_GROUPS, n_rows, dh), BF16)] * 2,
        scratch_shapes=[pltpu.VMEM((seq + CMP_STRIDE, dh), F32)],
        compiler_params=_params(2),
    )(kv, kv, pos_k, pos_v, w1k, w1v, w2k, w2v)


def _masked_softmax(s, mask):
    sm = jnp.where(mask, s, NEG)
    m = jnp.max(sm, axis=-1, keepdims=True)
    p = jnp.where(mask, jnp.exp(sm - m), 0.0)
    return p / jnp.maximum(jnp.sum(p, axis=-1, keepdims=True), 1e-30)


def _nsa_kernel(q_ref, gate_ref, cos_ref, sa_ref, sb_ref, kcos_ref, ksa_ref, ksb_ref,
                kc_ref, vc_ref, ks_raw_ref, vs_ref, kw_raw_ref, vw_ref, ovt_ref, eb_ref, wb_ref,
                o_ref, ks_ref, kw_ref, *, seq, tq, kc_len, row_chunk):
    g = pl.program_id(1)
    n = pl.program_id(2)
    t0 = n * tq
    dh = LANES
    hpg = NSA_HPG
    m_rows = hpg * tq
    n_sel = seq // SEL_BLOCK
    topk = min(SEL_TOPK, n_sel)
    scale = dh ** -0.5

    stack = lambda x: jnp.concatenate([x] * hpg, axis=0)
    @pl.when(n == 0)
    def _():
        for r in range(0, seq, row_chunk):
            rs = slice(r, r + row_chunk)
            tabs = (kcos_ref[rs, :], ksa_ref[rs, :], ksb_ref[rs, :])
            ks_ref[rs, :] = _rope(ks_raw_ref[rs, :].astype(F32), *tabs).astype(BF16)
            kw_ref[rs, :] = _rope(kw_raw_ref[rs, :].astype(F32), *tabs).astype(BF16)

    q = q_ref[...].astype(F32)
    qs = jnp.concatenate([q[:, h * dh:(h + 1) * dh] for h in range(hpg)], axis=0) * scale
    qr = _rope(qs, stack(cos_ref[...]), stack(sa_ref[...]), stack(sb_ref[...]))
    qb = qs.astype(BF16)
    qrb = qr.astype(BF16)
    row = lax.broadcasted_iota(jnp.int32, (m_rows, 1), 0)
    tq_col = t0 + (row & (tq - 1))

    wk = WINDOW + tq
    start = pl.multiple_of(jnp.maximum(t0 - WINDOW, 0), LANES)
    case = jnp.minimum(n, WINDOW // tq)
    row_id = lax.broadcasted_iota(jnp.int32, (m_rows, LANES), 0) & (tq - 1)
    lane_id = lax.broadcasted_iota(jnp.int32, (m_rows, LANES), 1)
    lhs_w = jnp.concatenate([qrb, jnp.where(row_id == lane_id, 1.0, 0.0).astype(BF16)], axis=1)
    keys_w = jnp.concatenate([kw_ref[pl.ds(start, wk), :], wb_ref[case]], axis=1)
    s_c = _dot_nt(qb, kc_ref[0, 0])
    s_w = _dot_nt(lhs_w, keys_w)
    lane_c = lax.broadcasted_iota(jnp.int32, s_c.shape, 1)
    p_c = _masked_softmax(s_c, lane_c * CMP_STRIDE + (CMP_BLOCK - 1) <= tq_col)
    p_w = jnp.exp(s_w - jnp.max(s_w, axis=-1, keepdims=True))
    p_w = p_w / jnp.maximum(jnp.sum(p_w, axis=-1, keepdims=True), 1e-30)
    o_c = _dot(p_c.astype(BF16), vc_ref[0, 0])
    o_w = _dot(p_w.astype(BF16), vw_ref[pl.ds(start, wk), :])

    p_sum = p_c[0:tq]
    for h in range(1, hpg):
        p_sum = p_sum + p_c[h * tq:(h + 1) * tq]
    imp_t = lax.dot_general(ovt_ref[...], p_sum, (((1,), (1,)), ((), ())),
                            preferred_element_type=F32, precision=lax.Precision.HIGHEST)
    nsp = -(-n_sel // 8) * 8
    imp_t = imp_t[0:nsp]
    blk = lax.broadcasted_iota(jnp.int32, (nsp, tq), 0)
    t1 = t0 + lax.broadcasted_iota(jnp.int32, (nsp, tq), 1)
    cur = t1 // SEL_BLOCK
    forced = (blk == 0) | (blk == cur) | (blk == cur - 1)
    visible = blk * SEL_BLOCK <= t1
    score = jnp.where(forced, BIG, jnp.where(visible, imp_t, -BIG))
    score = jnp.where(blk < n_sel, score, -3e38)
    rank = jnp.zeros((nsp, tq), F32)
    for j2 in range(n_sel):
        other = score[j2:j2 + 1, :]
        rank = rank + jnp.where(other > score, 1.0,
                                jnp.where((other == score) & (blk > j2), 1.0, 0.0))
    chosen = (rank < topk) & visible & (blk < n_sel)
    bias_t = jnp.where(chosen, 0.0, NEG)
    if nsp < LANES:
        bias_t = jnp.concatenate([bias_t, jnp.full((LANES - nsp, tq), NEG, F32)], axis=0)
    sel_bias = stack(bias_t.T).astype(BF16)

    lhs_sel = jnp.concatenate([qrb, sel_bias], axis=1)

    def sel_step(ci, carry, diagonal):
        m_i, l_i, acc = carry
        k0 = pl.multiple_of(ci * kc_len, kc_len)
        keys = jnp.concatenate([ks_ref[pl.ds(k0, kc_len), :], eb_ref[pl.ds(k0, kc_len), :]], axis=1)
        s = _dot_nt(lhs_sel, keys)
        if diagonal:
            kpos = k0 + lax.broadcasted_iota(jnp.int32, (m_rows, kc_len), 1)
            s = jnp.where(kpos <= tq_col, s, NEG)
        m_new = jnp.maximum(m_i, jnp.max(s, axis=-1, keepdims=True))
        alpha = jnp.exp(m_i - m_new)
        p = jnp.exp(s - m_new)
        l_new = alpha * l_i + jnp.sum(p, axis=-1, keepdims=True)
        acc_new = alpha * acc + _dot(p.astype(BF16), vs_ref[pl.ds(k0, kc_len), :])
        return m_new, l_new, acc_new

    n_kc = (t0 + tq + kc_len - 1) // kc_len
    init = (jnp.full((m_rows, 1), NEG, F32), jnp.zeros((m_rows, 1), F32),
            jnp.zeros((m_rows, dh), F32))
    carry = lax.fori_loop(0, n_kc - 1, functools.partial(sel_step, diagonal=False), init)
    _, l_s, acc_s = sel_step(n_kc - 1, carry, True)
    o_s = acc_s / jnp.maximum(l_s, 1e-30)

    gates = _sigmoid(gate_ref[...])
    for h in range(hpg):
        base = (g * hpg + h) * NSA_N_BRANCH
        rows = slice(h * tq, (h + 1) * tq)
        out = (_pick_lane(gates, base) * o_c[rows] + _pick_lane(gates, base + 1) * o_s[rows]
               + _pick_lane(gates, base + 2) * o_w[rows])
        o_ref[:, h * dh:(h + 1) * dh] = out.astype(o_ref.dtype)


def nsa_attention(q, gates, cos, sin_a, sin_b, k_cmp, v_cmp, kvr, tables, *, batch, seq, kc_len=512):
    overlap_t, block_onehot, window_bias = tables
    n = q.shape[0]
    dh = LANES
    tq = LANES
    gw = NSA_HPG * dh
    nq = seq // tq
    kc_len = min(kc_len, seq)
    rows = lambda width: pl.BlockSpec((tq, width), lambda b, g, i: (b * nq + i, 0))
    cmp_spec = pl.BlockSpec((1, 1, k_cmp.shape[2], dh), lambda b, g, i: (b, g, 0, 0))
    kv_spec = lambda t: pl.BlockSpec((seq, dh), lambda b, g, i, t=t: (b, t * NSA_GROUPS + g))
    const = lambda shape: pl.BlockSpec(shape, lambda b, g, i: (0,) * len(shape))
    key_tab = pl.BlockSpec((seq, LANES), lambda b, g, i: (b, 0))
    return pl.pallas_call(
        functools.partial(_nsa_kernel, seq=seq, tq=tq, kc_len=kc_len, row_chunk=256),
        grid=(batch, NSA_GROUPS, nq),
        in_specs=[
            pl.BlockSpec((tq, gw), lambda b, g, i: (b * nq + i, g)),
            rows(LANES), rows(LANES), rows(LANES), rows(LANES),
            key_tab, key_tab, key_tab,
            cmp_spec, cmp_spec, kv_spec(2), kv_spec(3), kv_spec(4), kv_spec(5),
            const(overlap_t.shape), const(block_onehot.shape), const(window_bias.shape),
        ],
        out_specs=pl.BlockSpec((tq, gw), lambda b, g, i: (b * nq + i, g)),
        out_shape=jax.ShapeDtypeStruct((n, NSA_HEADS * dh), BF16),
        scratch_shapes=[pltpu.VMEM((seq, dh), BF16), pltpu.VMEM((seq, dh), BF16)],
        compiler_params=_params(3),
    )(q, gates, cos, sin_a, sin_b, cos, sin_a, sin_b, k_cmp, v_cmp, kvr, kvr, kvr, kvr,
      overlap_t, block_onehot, window_bias)


def _rope_tables(positions):
    half = ROPE_DIM // 2
    inv = ROPE_THETA ** (-jnp.arange(0, ROPE_DIM, 2, dtype=F32) / ROPE_DIM)
    ang = positions.astype(F32).reshape(-1, 1) * inv
    cos, sin = jnp.cos(ang), jnp.sin(ang)
    n = ang.shape[0]
    ones = jnp.ones((n, LANES - ROPE_DIM), F32)
    zeros_h = jnp.zeros((n, half), F32)
    zeros_t = jnp.zeros((n, LANES - ROPE_DIM), F32)
    cos_t = jnp.concatenate([cos, cos, ones], axis=1)
    sin_a = jnp.concatenate([-sin, zeros_h, zeros_t], axis=1)
    sin_b = jnp.concatenate([zeros_h, sin, zeros_t], axis=1)
    return cos_t, sin_a, sin_b


def _overlap_table(seq):
    n_cmp = (seq - CMP_BLOCK) // CMP_STRIDE + 1
    n_sel = seq // SEL_BLOCK
    c_idx = jnp.arange(n_cmp)
    j_idx = jnp.arange(n_sel)
    ov = jnp.clip(jnp.minimum(c_idx[:, None] * CMP_STRIDE + CMP_BLOCK, (j_idx[None, :] + 1) * SEL_BLOCK)
                  - jnp.maximum(c_idx[:, None] * CMP_STRIDE, j_idx[None, :] * SEL_BLOCK), 0)
    ov = ov.astype(F32) / CMP_STRIDE
    return jnp.pad(ov, ((0, LANES - n_cmp), (0, LANES - n_sel)))


def _nsa_tables(seq):
    tq = LANES
    key = jnp.arange(seq)
    block_onehot = (key[:, None] // SEL_BLOCK == jnp.arange(LANES)[None, :]).astype(BF16)
    x = jnp.arange(WINDOW + tq)[None, :, None]
    t = (jnp.arange(WINDOW // tq + 1) * tq)[:, None, None] + jnp.arange(tq)[None, None, :]
    window_bias = jnp.where((x <= t) & (x > t - WINDOW), 0.0, NEG).astype(BF16)
    return _overlap_table(seq).T, block_onehot, window_bias


def _split_pad(w, main):
    extra = w[:, main:]
    return w[:, :main].astype(BF16), jnp.pad(extra, ((0, 0), (0, LANES - extra.shape[1]))).astype(BF16)


def kernel(x, positions, attn_norm, ffn_norm, final_norm, gdn_w_in, gdn_conv, gdn_a_log, gdn_dt_bias,
           gdn_out_norm, gdn_w_out, kv_norm, nsa_w_kv, cmp_pos_k, cmp_w1_k, cmp_w2_k, cmp_pos_v, cmp_w1_v,
           cmp_w2_v, nsa_w_q, nsa_w_o, ffn_w_gate, ffn_w_up, ffn_w_down, moe_router, moe_w_gate, moe_w_up,
           moe_w_down):
    batch, seq, d = x.shape
    depth = attn_norm.shape[0]
    n_a = depth // 2
    h = x.reshape(batch * seq, d)
    gdn_width = gdn_w_out.shape[1]
    nsa_width = nsa_w_o.shape[1]
    shared = None
    for l in range(depth):
        if l < n_a:
            w_main, w_gates = _split_pad(gdn_w_in[l], 4 * gdn_width)
            proj, gates = rms_matmul(h, attn_norm[l], w_main, w_gates, out_dtype=BF16)
            o = gdn_core(proj, gates, gdn_conv[l], gdn_a_log[l], gdn_dt_bias[l], gdn_out_norm[l],
                         batch=batch, seq=seq)
            h = matmul_residual(o, gdn_w_out[l].astype(BF16), h)
        else:
            cos, sin_a, sin_b, k_cmp, v_cmp, kvr, tables = shared
            w_main, w_gates = _split_pad(nsa_w_q[l - n_a], nsa_width)
            q, gates = rms_matmul(h, attn_norm[l], w_main, w_gates, out_dtype=BF16)
            o = nsa_attention(q, gates, cos, sin_a, sin_b, k_cmp, v_cmp, kvr, tables,
                              batch=batch, seq=seq)
            h = matmul_residual(o, nsa_w_o[l - n_a].astype(BF16), h)
        final_gain = final_norm if l == depth - 1 else None
        if l % 2 == 0:
            i = l // 2
            h = ffn_residual(h, ffn_norm[l], None, ffn_w_gate[i][None], ffn_w_up[i][None],
                             ffn_w_down[i][None], final_gain)
        else:
            i = l // 2
            comb = route_tokens(h, ffn_norm[l], moe_router[i])
            h = moe_residual(h, ffn_norm[l], comb, moe_w_gate[i], moe_w_up[i], moe_w_down[i], final_gain)
        if l == n_a - 1:
            cos, sin_a, sin_b = _rope_tables(positions)
            kv, _ = rms_matmul(h, kv_norm, nsa_w_kv.astype(BF16), jnp.zeros((d, LANES), BF16),
                               out_dtype=BF16)
            kvr = kv
            k_cmp, v_cmp = nsa_compress(kv, cmp_pos_k, cmp_pos_v, cmp_w1_k.astype(BF16),
                                        cmp_w1_v.astype(BF16), cmp_w2_k.astype(BF16),
                                        cmp_w2_v.astype(BF16), batch=batch, seq=seq)
            shared = (cos, sin_a, sin_b, k_cmp, v_cmp, kvr, _nsa_tables(seq))
    return h.reshape(batch, seq, d)
```

```python
import functools
import math

import jax
import jax.numpy as jnp
from jax import lax
from jax.experimental import pallas as pl
from jax.experimental.pallas import tpu as pltpu

F32 = jnp.float32
BF16 = jnp.bfloat16

NORM_EPS = 1e-6
LANES = 128
GDN_HEADS = 16
GDN_CONV = 4
GDN_CHUNK = 64
GDN_GROUP = 256
GDN_GROUPS_IN_FLIGHT = 4
GDN_HEADS_PER_STEP = 4
NSA_HEADS = 16
NSA_GROUPS = 4
NSA_HPG = NSA_HEADS // NSA_GROUPS
NSA_N_BRANCH = 3
CMP_BLOCK = 32
CMP_STRIDE = 16
SEL_BLOCK = 64
SEL_TOPK = 16
WINDOW = 512
BIG = 1e9
NEG = -1e30
ROPE_THETA = 500000.0
ROPE_DIM = 32
TOP_K = 2

VMEM_LIMIT_BYTES = 56 * 1024 * 1024


def _params(n_axes):
    return pltpu.CompilerParams(dimension_semantics=("arbitrary",) * n_axes,
                                vmem_limit_bytes=VMEM_LIMIT_BYTES)


def _rms(x, gain):
    ms = jnp.mean(x * x, axis=-1, keepdims=True)
    return x * lax.rsqrt(ms + NORM_EPS) * gain


def _sigmoid(x):
    return 1.0 / (1.0 + jnp.exp(-x))


def _dot(a, b):
    return jnp.dot(a, b, preferred_element_type=F32)


def _dot_nt(a, b):
    return lax.dot_general(a, b, (((1,), (1,)), ((), ())), preferred_element_type=F32)


def _dot_tn(a, b):
    return lax.dot_general(a, b, (((0,), (0,)), ((), ())), preferred_element_type=F32)


def _pick_lane(x, idx):
    lane = lax.broadcasted_iota(jnp.int32, x.shape, 1)
    return jnp.sum(jnp.where(lane == idx, x, 0.0), axis=-1, keepdims=True)


def _rms_matmul_kernel(h_ref, g_ref, w_ref, we_ref, o_ref, oe_ref, xn_ref, *, row_chunk):
    j = pl.program_id(1)

    @pl.when(j == 0)
    def _():
        for r in range(0, h_ref.shape[0], row_chunk):
            xn_ref[r:r + row_chunk, :] = _rms(h_ref[r:r + row_chunk, :], g_ref[...]).astype(BF16)
        oe_ref[...] = _dot(xn_ref[...], we_ref[...])

    o_ref[...] = _dot(xn_ref[...], w_ref[...]).astype(o_ref.dtype)


def rms_matmul(h, gain, w, w_extra, *, tm=1024, tn=1024, out_dtype=F32):
    n, d = h.shape
    nm = w.shape[1]
    tm = min(tm, n)
    tn = min(tn, nm)
    return pl.pallas_call(
        functools.partial(_rms_matmul_kernel, row_chunk=256),
        grid=(n // tm, nm // tn),
        in_specs=[
            pl.BlockSpec((tm, d), lambda i, j: (i, 0)),
            pl.BlockSpec((1, d), lambda i, j: (0, 0)),
            pl.BlockSpec((d, tn), lambda i, j: (0, j)),
            pl.BlockSpec((d, LANES), lambda i, j: (0, 0)),
        ],
        out_specs=[
            pl.BlockSpec((tm, tn), lambda i, j: (i, j)),
            pl.BlockSpec((tm, LANES), lambda i, j: (i, 0)),
        ],
        out_shape=[jax.ShapeDtypeStruct((n, nm), out_dtype),
                   jax.ShapeDtypeStruct((n, LANES), F32)],
        scratch_shapes=[pltpu.VMEM((tm, d), BF16)],
        compiler_params=_params(2),
    )(h, gain.reshape(1, d), w, w_extra)


def _matmul_res_kernel(x_ref, w_ref, r_ref, o_ref):
    o_ref[...] = r_ref[...] + _dot(x_ref[...], w_ref[...])


def matmul_residual(x, w, res, *, tm=1024, tn=512):
    n, k = x.shape
    nm = w.shape[1]
    tm = min(tm, n)
    return pl.pallas_call(
        _matmul_res_kernel,
        grid=(n // tm, nm // tn),
        in_specs=[
            pl.BlockSpec((tm, k), lambda i, j: (i, 0)),
            pl.BlockSpec((k, tn), lambda i, j: (0, j)),
            pl.BlockSpec((tm, tn), lambda i, j: (i, j)),
        ],
        out_specs=pl.BlockSpec((tm, tn), lambda i, j: (i, j)),
        out_shape=jax.ShapeDtypeStruct((n, nm), F32),
        compiler_params=_params(2),
    )(x, w, res)


def _router_kernel(h_ref, g_ref, r_ref, o_ref, *, n_experts, row_chunk):
    for r in range(0, h_ref.shape[0], row_chunk):
        xn = _rms(h_ref[r:r + row_chunk, :], g_ref[...])
        logits = jnp.dot(xn, r_ref[...], preferred_element_type=F32,
                         precision=lax.Precision.HIGHEST)
        lane = lax.broadcasted_iota(jnp.int32, logits.shape, 1)
        valid = lane < n_experts
        z = jnp.where(valid, logits, NEG)
        z = z - jnp.max(z, axis=-1, keepdims=True)
        p = jnp.where(valid, jnp.exp(z), 0.0)
        p = p / jnp.sum(p, axis=-1, keepdims=True)
        p = jnp.where(valid, p, -1.0)
        p1 = jnp.max(p, axis=-1, keepdims=True)
        i1 = jnp.min(jnp.where(p == p1, lane, LANES), axis=-1, keepdims=True)
        pm = jnp.where(lane == i1, -1.0, p)
        p2 = jnp.max(pm, axis=-1, keepdims=True)
        i2 = jnp.min(jnp.where(pm == p2, lane, LANES), axis=-1, keepdims=True)
        den = p1 + p2
        o_ref[r:r + row_chunk, :] = (jnp.where(lane == i1, p1 / den, 0.0)
                                     + jnp.where(lane == i2, p2 / den, 0.0))


def route_tokens(h, gain, router, *, tm=1024):
    n, d = h.shape
    n_experts = router.shape[1]
    tm = min(tm, n)
    r_pad = jnp.pad(router, ((0, 0), (0, LANES - n_experts)))
    return pl.pallas_call(
        functools.partial(_router_kernel, n_experts=n_experts, row_chunk=256),
        grid=(n // tm,),
        in_specs=[
            pl.BlockSpec((tm, d), lambda i: (i, 0)),
            pl.BlockSpec((1, d), lambda i: (0, 0)),
            pl.BlockSpec((d, LANES), lambda i: (0, 0)),
        ],
        out_specs=pl.BlockSpec((tm, LANES), lambda i: (i, 0)),
        out_shape=jax.ShapeDtypeStruct((n, LANES), F32),
        compiler_params=_params(1),
    )(h, gain.reshape(1, d), r_pad)


def _ffn_kernel(h_ref, g_ref, c_ref, wg_ref, wu_ref, wd_ref, fg_ref, o_ref, xn_ref, *,
                n_e, n_f, use_comb, final_norm, row_chunk):
    e = pl.program_id(1)
    f = pl.program_id(2)

    @pl.when((e == 0) & (f == 0))
    def _():
        for r in range(0, h_ref.shape[0], row_chunk):
            x = h_ref[r:r + row_chunk, :]
            xn_ref[r:r + row_chunk, :] = _rms(x, g_ref[...]).astype(BF16)
            o_ref[r:r + row_chunk, :] = x

    xn = xn_ref[...]
    gate = _dot(xn, wg_ref[0])
    up = _dot(xn, wu_ref[0])
    act = gate * _sigmoid(gate) * up
    if use_comb:
        act = act * _pick_lane(c_ref[...], e)
    o_ref[...] += _dot(act.astype(BF16), wd_ref[0])

    if final_norm:
        @pl.when((e == n_e - 1) & (f == n_f - 1))
        def _():
            for r in range(0, h_ref.shape[0], row_chunk):
                o_ref[r:r + row_chunk, :] = _rms(o_ref[r:r + row_chunk, :], fg_ref[...])


def ffn_residual(h, gain, comb, w_gate, w_up, w_down, final_gain, *, tm=512, tf=512):
    n, d = h.shape
    n_e, _, ff = w_gate.shape
    w_gate, w_up, w_down = w_gate.astype(BF16), w_up.astype(BF16), w_down.astype(BF16)
    tm = min(tm, n)
    use_comb = comb is not None
    final_norm = final_gain is not None
    if comb is None:
        comb = jnp.ones((n, LANES), F32)
    if final_gain is None:
        final_gain = jnp.ones((d,), F32)
    n_f = ff // tf
    return pl.pallas_call(
        functools.partial(_ffn_kernel, n_e=n_e, n_f=n_f, use_comb=use_comb,
                          final_norm=final_norm, row_chunk=256),
        grid=(n // tm, n_e, n_f),
        in_specs=[
            pl.BlockSpec((tm, d), lambda i, e, f: (i, 0)),
            pl.BlockSpec((1, d), lambda i, e, f: (0, 0)),
            pl.BlockSpec((tm, LANES), lambda i, e, f: (i, 0)),
            pl.BlockSpec((1, d, tf), lambda i, e, f: (e, 0, f)),
            pl.BlockSpec((1, d, tf), lambda i, e, f: (e, 0, f)),
            pl.BlockSpec((1, tf, d), lambda i, e, f: (e, f, 0)),
            pl.BlockSpec((1, d), lambda i, e, f: (0, 0)),
        ],
        out_specs=pl.BlockSpec((tm, d), lambda i, e, f: (i, 0)),
        out_shape=jax.ShapeDtypeStruct((n, d), F32),
        scratch_shapes=[pltpu.VMEM((tm, d), BF16)],
        compiler_params=_params(3),
    )(h, gain.reshape(1, d), comb, w_gate, w_up, w_down, final_gain.reshape(1, d))


def _moe_kernel(h_ref, g_ref, c_ref, ct_ref, wg_ref, wu_ref, wd_ref, fg_ref, o_ref,
                xn_ref, rc_ref, rr_ref, xs_ref, ye_ref, *, n_e, n_f, cap, final_norm, row_chunk):
    e = pl.program_id(1)
    f = pl.program_id(2)
    tm, d = h_ref.shape
    col_chunk = 512

    @pl.when((e == 0) & (f == 0))
    def _():
        for r in range(0, tm, row_chunk):
            x = h_ref[r:r + row_chunk, :]
            xn_ref[r:r + row_chunk, :] = _rms(x, g_ref[...]).astype(BF16)
            o_ref[r:r + row_chunk, :] = x
        ti = lax.broadcasted_iota(jnp.int32, (tm, tm), 0)
        tj = lax.broadcasted_iota(jnp.int32, (tm, tm), 1)
        routed_c = jnp.where(c_ref[...] > 0.0, 1.0, 0.0).astype(BF16)
        routed_r = jnp.where(ct_ref[...] > 0.0, 1.0, 0.0).astype(BF16)
        rc_ref[...] = _dot(jnp.where(tj < ti, 1.0, 0.0).astype(BF16), routed_c)
        rr_ref[...] = _dot(routed_r, jnp.where(ti < tj, 1.0, 0.0).astype(BF16))

    key_row = jnp.where(ct_ref[pl.ds(e, 1), :] > 0.0, rr_ref[pl.ds(e, 1), :], -1.0)
    w_row = ct_ref[pl.ds(e, 1), :]
    n_routed = jnp.sum(jnp.where(key_row >= 0.0, 1.0, 0.0)).astype(jnp.int32)
    n_rounds = (n_routed + cap - 1) // cap

    def gather_mask(rnd):
        slot = lax.broadcasted_iota(jnp.int32, (cap, tm), 0) + rnd * cap
        return key_row == slot.astype(F32)

    def expert(xs):
        gate = _dot(xs, wg_ref[0])
        up = _dot(xs, wu_ref[0])
        return _dot((gate * _sigmoid(gate) * up).astype(BF16), wd_ref[0])

    def scatter_add(rnd, y):
        picked = gather_mask(rnd)
        w_slot = jnp.sum(jnp.where(picked, w_row, 0.0), axis=-1, keepdims=True)
        yw = (y * w_slot).astype(BF16)
        key_col = jnp.where(_pick_lane(c_ref[...], e) > 0.0, _pick_lane(rc_ref[...], e), -1.0)
        slot = lax.broadcasted_iota(jnp.int32, (tm, cap), 1) + rnd * cap
        back = jnp.where(key_col == slot.astype(F32), 1.0, 0.0).astype(BF16)
        for c0 in range(0, d, col_chunk):
            o_ref[:, c0:c0 + col_chunk] += _dot(back, yw[:, c0:c0 + col_chunk])

    @pl.when(f == 0)
    def _():
        xs_ref[...] = _dot(jnp.where(gather_mask(0), 1.0, 0.0).astype(BF16), xn_ref[...]).astype(BF16)

    part = expert(xs_ref[...])

    @pl.when(f == 0)
    def _():
        ye_ref[...] = part

    @pl.when(f > 0)
    def _():
        ye_ref[...] += part

    @pl.when(f == n_f - 1)
    def _():
        scatter_add(0, ye_ref[...])

    def extra_round(rnd, carry):
        xs = _dot(jnp.where(gather_mask(rnd), 1.0, 0.0).astype(BF16), xn_ref[...]).astype(BF16)
        scatter_add(rnd, expert(xs))
        return carry

    lax.fori_loop(1, n_rounds, extra_round, 0)

    if final_norm:
        @pl.when((e == n_e - 1) & (f == n_f - 1))
        def _():
            for r in range(0, tm, row_chunk):
                o_ref[r:r + row_chunk, :] = _rms(o_ref[r:r + row_chunk, :], fg_ref[...])


def moe_residual(h, gain, comb, w_gate, w_up, w_down, final_gain, *, tm=1024, tf=1024, cap=320):
    n, d = h.shape
    n_e, _, ff = w_gate.shape
    w_gate, w_up, w_down = w_gate.astype(BF16), w_up.astype(BF16), w_down.astype(BF16)
    tm = min(tm, n)
    final_norm = final_gain is not None
    if final_gain is None:
        final_gain = jnp.ones((d,), F32)
    n_f = ff // tf
    once = pl.Buffered(1)
    return pl.pallas_call(
        functools.partial(_moe_kernel, n_e=n_e, n_f=n_f, cap=cap, final_norm=final_norm,
                          row_chunk=256),
        grid=(n // tm, n_e, n_f),
        in_specs=[
            pl.BlockSpec((tm, d), lambda i, e, f: (i, 0), pipeline_mode=once),
            pl.BlockSpec((1, d), lambda i, e, f: (0, 0)),
            pl.BlockSpec((tm, LANES), lambda i, e, f: (i, 0), pipeline_mode=once),
            pl.BlockSpec((LANES, tm), lambda i, e, f: (0, i), pipeline_mode=once),
            pl.BlockSpec((1, d, tf), lambda i, e, f: (e, 0, f)),
            pl.BlockSpec((1, d, tf), lambda i, e, f: (e, 0, f)),
            pl.BlockSpec((1, tf, d), lambda i, e, f: (e, f, 0)),
            pl.BlockSpec((1, d), lambda i, e, f: (0, 0)),
        ],
        out_specs=pl.BlockSpec((tm, d), lambda i, e, f: (i, 0), pipeline_mode=once),
        out_shape=jax.ShapeDtypeStruct((n, d), F32),
        scratch_shapes=[pltpu.VMEM((tm, d), BF16), pltpu.VMEM((tm, LANES), F32),
                        pltpu.VMEM((LANES, tm), F32), pltpu.VMEM((cap, d), BF16),
                        pltpu.VMEM((cap, d), F32)],
        compiler_params=_params(3),
    )(h, gain.reshape(1, d), comb, comb.T, w_gate, w_up, w_down, final_gain.reshape(1, d))


def _tri_inv_many(lows, nilpotency):
    c = lows[0].shape[0]
    ii = lax.broadcasted_iota(jnp.int32, (c, c), 0)
    jj = lax.broadcasted_iota(jnp.int32, (c, c), 1)
    eye = jnp.where(ii == jj, 1.0, 0.0)
    lb = [low.astype(BF16) for low in lows]
    ps = [_dot(b, b).astype(BF16) for b in lb]
    ts = [eye - low for low in lows]
    levels = int(math.log2(nilpotency)) - 1
    for level in range(levels):
        tb = [t.astype(BF16) for t in ts]
        if level + 1 < levels:
            nxt = [_dot(p, p).astype(BF16) for p in ps]
        ts = [t + _dot(b, p) for t, b, p in zip(ts, tb, ps)]
        if level + 1 < levels:
            ps = nxt
    return ts


def _gdn_local_kernel(alog_ref, dtb_ref, q_ref, k_ref, v_ref, cq_ref, ck_ref, cv_ref,
                      a1_ref, a2_ref, b2_ref,
                      u_ref, w_ref, qd_ref, kd_ref, aqk_ref, gl_ref,
                      xp_ref, qs_ref, ks_ref, vs_ref, gc1_ref, gc2_ref, gl2_ref, be2_ref, *,
                      seq, row_chunk):
    hd = pl.program_id(1)
    dk = q_ref.shape[1]
    c = GDN_CHUNK
    grp = GDN_GROUP
    n_groups = seq // grp
    pad = 8

    def conv_silu(x_ref, w_ref, dst_ref, normalise, scale):
        xp_ref[0:pad, :] = jnp.zeros((pad, dk), F32)
        xp_ref[pad:pad + seq, :] = x_ref[...].astype(F32)
        w = w_ref[...]
        for r in range(0, seq, row_chunk):
            y = xp_ref[pad - 3 + r:pad - 3 + r + row_chunk, :] * w[0:1, :]
            for j in range(1, GDN_CONV):
                y = y + xp_ref[pad - 3 + j + r:pad - 3 + j + r + row_chunk, :] * w[j:j + 1, :]
            y = y * _sigmoid(y)
            if normalise:
                y = y * lax.rsqrt(jnp.sum(y * y, axis=-1, keepdims=True) + NORM_EPS)
            if scale != 1.0:
                y = y * scale
            dst_ref[r:r + row_chunk, :] = y

    conv_silu(q_ref, cq_ref, qs_ref, True, dk ** -0.5)
    conv_silu(k_ref, ck_ref, ks_ref, True, 1.0)
    conv_silu(v_ref, cv_ref, vs_ref, False, 1.0)

    a_coef = -jnp.exp(alog_ref[hd])
    dtb = dtb_ref[hd]

    def log_decay(a):
        x = a + dtb
        return a_coef * (jnp.maximum(x, 0.0) + jnp.log(1.0 + jnp.exp(-jnp.abs(x))))

    ii = lax.broadcasted_iota(jnp.int32, (grp, grp), 0)
    jj = lax.broadcasted_iota(jnp.int32, (grp, grp), 1)
    same = (ii // c) == (jj // c)
    hi = lax.Precision.HIGHEST
    g1 = log_decay(a1_ref[0, 0])
    g2 = log_decay(a2_ref[0, 0])
    ones_b = jnp.where(same, 1.0, 0.0)
    gc1_ref[...] = jnp.dot(g1, jnp.where(same & (ii <= jj), 1.0, 0.0),
                           preferred_element_type=F32, precision=hi)
    gc2_ref[...] = jnp.dot(jnp.where(same & (jj <= ii), 1.0, 0.0), g2,
                           preferred_element_type=F32, precision=hi)
    gl2_ref[...] = jnp.dot(ones_b, g2, preferred_element_type=F32, precision=hi)
    gl_ref[0, 0] = jnp.exp(jnp.dot(g1, ones_b, preferred_element_type=F32, precision=hi))
    be2_ref[...] = _sigmoid(b2_ref[0, 0])

    causal = same & (ii >= jj)
    strict = same & (ii > jj)
    n_par = GDN_GROUPS_IN_FLIGHT

    def groups(it, carry):
        gis = [it * n_par + t for t in range(n_par)]
        rows = [pl.ds(pl.multiple_of(gi * grp, grp), grp) for gi in gis]
        q = [qs_ref[r, :] for r in rows]
        k = [ks_ref[r, :] for r in rows]
        v = [vs_ref[r, :] for r in rows]
        g_row = [gc1_ref[pl.ds(gi, 1), :] for gi in gis]
        g_col = [_pick_lane(gc2_ref[...], gi) for gi in gis]
        gl_col = [_pick_lane(gl2_ref[...], gi) for gi in gis]
        b_col = [_pick_lane(be2_ref[...], gi) for gi in gis]
        kbf = [x.astype(BF16) for x in k]
        kb = [x * b for x, b in zip(k, b_col)]
        kk = [_dot_nt(x.astype(BF16), y) for x, y in zip(kb, kbf)]
        qk = [_dot_nt(x.astype(BF16), y) for x, y in zip(q, kbf)]
        decay = [jnp.where(causal, jnp.exp(jnp.where(causal, gc - gr, 0.0)), 0.0)
                 for gc, gr in zip(g_col, g_row)]
        t_inv = _tri_inv_many([jnp.where(strict, x * d, 0.0) for x, d in zip(kk, decay)], c)
        e_col = [jnp.exp(gc) for gc in g_col]
        rhs = [jnp.concatenate([x * b, y * e], axis=1).astype(BF16)
               for x, b, y, e in zip(v, b_col, kb, e_col)]
        uw = [_dot(t.astype(BF16), r) for t, r in zip(t_inv, rhs)]
        for t in range(n_par):
            u_ref[rows[t], :] = uw[t][:, :dk].astype(u_ref.dtype)
            w_ref[rows[t], :] = uw[t][:, dk:].astype(w_ref.dtype)
            a_qk = qk[t] * decay[t]
            a_c = a_qk[:, 0:c]
            for blk in range(1, grp // c):
                a_c = a_c + a_qk[:, blk * c:(blk + 1) * c]
            aqk_ref[0, 0, rows[t], :] = a_c.astype(aqk_ref.dtype)
            qd_ref[rows[t], :] = (q[t] * e_col[t]).astype(qd_ref.dtype)
            kd_ref[rows[t], :] = (k[t] * jnp.exp(gl_col[t] - g_col[t])).astype(kd_ref.dtype)
        return carry

    lax.fori_loop(0, n_groups // n_par, groups, 0)


def _gdn_scan_kernel(gl_ref, u_ref, w_ref, qd_ref, kd_ref, aqk_ref, z_ref, on_ref, o_ref, *,
                     seq, heads_per_step):
    b = pl.program_id(0)
    hb = pl.program_id(1)
    c = GDN_CHUNK
    dk = on_ref.shape[1]

    def step(ci, states):
        rows = pl.ds(pl.multiple_of(ci * c, c), c)
        hs = range(heads_per_step)
        cols = [slice(hh * dk, (hh + 1) * dk) for hh in hs]
        ws_qs = [_dot(jnp.concatenate([w_ref[rows, cols[hh]], qd_ref[rows, cols[hh]]], axis=0),
                      states[hh].astype(BF16)) for hh in hs]
        v_new = [(u_ref[rows, cols[hh]].astype(F32) - ws_qs[hh][:c]).astype(BF16) for hh in hs]
        intra = [_dot(aqk_ref[0, hh, rows, :], v_new[hh]) for hh in hs]
        outer = [_dot_tn(kd_ref[rows, cols[hh]], v_new[hh]) for hh in hs]
        new_states = []
        for hh in hs:
            decay = gl_ref[b, hb * heads_per_step + hh, ci]
            new_states.append(states[hh] * decay + outer[hh])
            z = z_ref[rows, cols[hh]].astype(F32)
            o = ws_qs[hh][c:] + intra[hh]
            o_ref[rows, cols[hh]] = (_rms(o, on_ref[...]) * (z * _sigmoid(z))).astype(o_ref.dtype)
        return tuple(new_states)

    lax.fori_loop(0, seq // c, step,
                  tuple(jnp.zeros((dk, dk), F32) for _ in range(heads_per_step)))


def gdn_core(proj, gates, conv_w, a_log, dt_bias, out_norm, *, batch, seq):
    n = proj.shape[0]
    nh = GDN_HEADS
    dk = proj.shape[1] // (4 * nh)
    c = GDN_CHUNK
    grp = GDN_GROUP
    n_groups = seq // grp
    hps = GDN_HEADS_PER_STEP
    bl = gates[:, :nh].reshape(batch, seq, nh).transpose(0, 2, 1)
    al = gates[:, nh:2 * nh].reshape(batch, seq, nh).transpose(0, 2, 1)
    a1 = al.reshape(batch, nh, n_groups, grp)
    a2 = a1.transpose(0, 1, 3, 2)
    b2 = bl.reshape(batch, nh, n_groups, grp).transpose(0, 1, 3, 2)
    smem = pl.BlockSpec(memory_space=pltpu.SMEM)
    col = lambda off: pl.BlockSpec((seq, dk), lambda b, h, off=off: (b, off * nh + h))
    cw = lambda off: pl.BlockSpec((GDN_CONV, dk), lambda b, h, off=off: (0, off * nh + h))
    g1 = pl.BlockSpec((1, 1, n_groups, grp), lambda b, h: (b, h, 0, 0))
    g2 = pl.BlockSpec((1, 1, grp, n_groups), lambda b, h: (b, h, 0, 0))
    head_out = pl.BlockSpec((seq, dk), lambda b, h: (b, h))
    wide = jax.ShapeDtypeStruct((n, nh * dk), BF16)
    u, w, qd, kd, aqk, gl = pl.pallas_call(
        functools.partial(_gdn_local_kernel, seq=seq, row_chunk=256),
        grid=(batch, nh),
        in_specs=[smem, smem, col(0), col(1), col(2), cw(0), cw(1), cw(2), g1, g2, g2],
        out_specs=[head_out, head_out, head_out, head_out,
                   pl.BlockSpec((1, 1, seq, c), lambda b, h: (b, h, 0, 0)), g1],
        out_shape=[wide, wide, wide, wide,
                   jax.ShapeDtypeStruct((batch, nh, seq, c), BF16),
                   jax.ShapeDtypeStruct((batch, nh, n_groups, grp), F32)],
        scratch_shapes=[
            pltpu.VMEM((seq + 8, dk), F32),
            pltpu.VMEM((seq, dk), F32), pltpu.VMEM((seq, dk), F32), pltpu.VMEM((seq, dk), F32),
            pltpu.VMEM((n_groups, grp), F32), pltpu.VMEM((grp, n_groups), F32),
            pltpu.VMEM((grp, n_groups), F32), pltpu.VMEM((grp, n_groups), F32),
        ],
        compiler_params=_params(2),
    )(a_log, dt_bias, proj, proj, proj, conv_w, conv_w, conv_w, a1, a2, b2)
    chunk_decay = gl[..., c - 1::c].reshape(batch, nh, seq // c)
    heads = pl.BlockSpec((seq, hps * dk), lambda b, h: (b, h))
    return pl.pallas_call(
        functools.partial(_gdn_scan_kernel, seq=seq, heads_per_step=hps),
        grid=(batch, nh // hps),
        in_specs=[smem, heads, heads, heads, heads,
                  pl.BlockSpec((1, hps, seq, c), lambda b, h: (b, h, 0, 0)),
                  pl.BlockSpec((seq, hps * dk), lambda b, h: (b, 3 * (nh // hps) + h)),
                  pl.BlockSpec((1, dk), lambda b, h: (0, 0))],
        out_specs=heads,
        out_shape=wide,
        compiler_params=_params(2),
    )(chunk_decay, u, w, qd, kd, aqk, proj, out_norm.reshape(1, dk))


def _rope(x, cos, sin_a, sin_b):
    half = ROPE_DIM // 2
    return (x * cos + pltpu.roll(x, LANES - half, 1) * sin_a + pltpu.roll(x, half, 1) * sin_b)


def _compress_kernel(kc_ref, vc_ref, pk_ref, pv_ref, w1k_ref, w1v_ref, w2k_ref, w2v_ref,
                     ok_ref, ov_ref, tp_ref, *, seq):
    dh = kc_ref.shape[1]
    n_rows = ok_ref.shape[2]
    hidden = w1k_ref.shape[1]

    def run(t_ref, pos_ref, w1_ref, w2_ref, out_ref):
        tp_ref[0:seq, :] = t_ref[...].astype(F32)
        tp_ref[seq:seq + CMP_STRIDE, :] = jnp.zeros((CMP_STRIDE, dh), F32)
        acc = jnp.zeros((n_rows, hidden), F32)
        for l in range(CMP_BLOCK):
            x = tp_ref[pl.ds(l, n_rows, stride=CMP_STRIDE), :] + pos_ref[l:l + 1, :]
            acc = acc + _dot(x.astype(BF16), w1_ref[l * dh:(l + 1) * dh, :])
        hid = acc * _sigmoid(acc)
        out_ref[0, 0] = _dot(hid.astype(BF16), w2_ref[...]).astype(out_ref.dtype)

    run(kc_ref, pk_ref, w1k_ref, w2k_ref, ok_ref)
    run(vc_ref, pv_ref, w1v_ref, w2v_ref, ov_ref)


def nsa_compress(kv, pos_k, pos_v, w1k, w1v, w2k, w2v, *, batch, seq):
    dh = pos_k.shape[1]
    n_rows = seq // CMP_STRIDE
    hidden = w1k.shape[1]
    full = lambda shape: pl.BlockSpec(shape, lambda b, g: (0,) * len(shape))
    out = pl.BlockSpec((1, 1, n_rows, dh), lambda b, g: (b, g, 0, 0))
    return pl.pallas_call(
        functools.partial(_compress_kernel, seq=seq),
        grid=(batch, NSA_GROUPS),
        in_specs=[
            pl.BlockSpec((seq, dh), lambda b, g: (b, g)),
            pl.BlockSpec((seq, dh), lambda b, g: (b, NSA_GROUPS + g)),
            full((CMP_BLOCK, dh)), full((CMP_BLOCK, dh)),
            full((CMP_BLOCK * dh, hidden)), full((CMP_BLOCK * dh, hidden)),
            full((hidden, dh)), full((hidden, dh)),
        ],
        out_specs=[out, out],
        out_shape=[jax.ShapeDtypeStruct((batch, NSA_GROUPS, n_rows, dh), BF16)] * 2,
        scratch_shapes=[pltpu.VMEM((seq + CMP_STRIDE, dh), F32)],
        compiler_params=_params(2),
    )(kv, kv, pos_k, pos_v, w1k, w1v, w2k, w2v)


def _masked_softmax(s, mask):
    sm = jnp.where(mask, s, NEG)
    m = jnp.max(sm, axis=-1, keepdims=True)
    p = jnp.where(mask, jnp.exp(sm - m), 0.0)
    return p / jnp.maximum(jnp.sum(p, axis=-1, keepdims=True), 1e-30)


def _nsa_kernel(q_ref, gate_ref, cos_ref, sa_ref, sb_ref, kcos_ref, ksa_ref, ksb_ref,
                kc_ref, vc_ref, ks_raw_ref, vs_ref, kw_raw_ref, vw_ref, ovt_ref, eb_ref, wb_ref,
                o_ref, ks_ref, kw_ref, *, seq, tq, kc_len, row_chunk):
    g = pl.program_id(1)
    n = pl.program_id(2)
    t0 = n * tq
    dh = LANES
    hpg = NSA_HPG
    m_rows = hpg * tq
    n_sel = seq // SEL_BLOCK
    topk = min(SEL_TOPK, n_sel)
    scale = dh ** -0.5

    stack = lambda x: jnp.concatenate([x] * hpg, axis=0)
    @pl.when(n == 0)
    def _():
        for r in range(0, seq, row_chunk):
            rs = slice(r, r + row_chunk)
            tabs = (kcos_ref[rs, :], ksa_ref[rs, :], ksb_ref[rs, :])
            ks_ref[rs, :] = _rope(ks_raw_ref[rs, :].astype(F32), *tabs).astype(BF16)
            kw_ref[rs, :] = _rope(kw_raw_ref[rs, :].astype(F32), *tabs).astype(BF16)

    q = q_ref[...].astype(F32)
    qs = jnp.concatenate([q[:, h * dh:(h + 1) * dh] for h in range(hpg)], axis=0) * scale
    qr = _rope(qs, stack(cos_ref[...]), stack(sa_ref[...]), stack(sb_ref[...]))
    qb = qs.astype(BF16)
    qrb = qr.astype(BF16)
    row = lax.broadcasted_iota(jnp.int32, (m_rows, 1), 0)
    tq_col = t0 + (row & (tq - 1))

    wk = WINDOW + tq
    start = pl.multiple_of(jnp.maximum(t0 - WINDOW, 0), LANES)
    case = jnp.minimum(n, WINDOW // tq)
    row_id = lax.broadcasted_iota(jnp.int32, (m_rows, LANES), 0) & (tq - 1)
    lane_id = lax.broadcasted_iota(jnp.int32, (m_rows, LANES), 1)
    lhs_w = jnp.concatenate([qrb, jnp.where(row_id == lane_id, 1.0, 0.0).astype(BF16)], axis=1)
    keys_w = jnp.concatenate([kw_ref[pl.ds(start, wk), :], wb_ref[case]], axis=1)
    s_c = _dot_nt(qb, kc_ref[0, 0])
    s_w = _dot_nt(lhs_w, keys_w)
    lane_c = lax.broadcasted_iota(jnp.int32, s_c.shape, 1)
    p_c = _masked_softmax(s_c, lane_c * CMP_STRIDE + (CMP_BLOCK - 1) <= tq_col)
    p_w = jnp.exp(s_w - jnp.max(s_w, axis=-1, keepdims=True))
    p_w = p_w / jnp.maximum(jnp.sum(p_w, axis=-1, keepdims=True), 1e-30)
    o_c = _dot(p_c.astype(BF16), vc_ref[0, 0])
    o_w = _dot(p_w.astype(BF16), vw_ref[pl.ds(start, wk), :])

    p_sum = p_c[0:tq]
    for h in range(1, hpg):
        p_sum = p_sum + p_c[h * tq:(h + 1) * tq]
    imp_t = lax.dot_general(ovt_ref[...], p_sum, (((1,), (1,)), ((), ())),
                            preferred_element_type=F32, precision=lax.Precision.HIGHEST)
    nsp = -(-n_sel // 8) * 8
    imp_t = imp_t[0:nsp]
    blk = lax.broadcasted_iota(jnp.int32, (nsp, tq), 0)
    t1 = t0 + lax.broadcasted_iota(jnp.int32, (nsp, tq), 1)
    cur = t1 // SEL_BLOCK
    forced = (blk == 0) | (blk == cur) | (blk == cur - 1)
    visible = blk * SEL_BLOCK <= t1
    score = jnp.where(forced, BIG, jnp.where(visible, imp_t, -BIG))
    score = jnp.where(blk < n_sel, score, -3e38)
    rank = jnp.zeros((nsp, tq), F32)
    for j2 in range(n_sel):
        other = score[j2:j2 + 1, :]
        rank = rank + jnp.where(other > score, 1.0,
                                jnp.where((other == score) & (blk > j2), 1.0, 0.0))
    chosen = (rank < topk) & visible & (blk < n_sel)
    bias_t = jnp.where(chosen, 0.0, NEG)
    if nsp < LANES:
        bias_t = jnp.concatenate([bias_t, jnp.full((LANES - nsp, tq), NEG, F32)], axis=0)
    sel_bias = stack(bias_t.T).astype(BF16)

    lhs_sel = jnp.concatenate([qrb, sel_bias], axis=1)

    def sel_step(ci, carry, diagonal):
        m_i, l_i, acc = carry
        k0 = pl.multiple_of(ci * kc_len, kc_len)
        keys = jnp.concatenate([ks_ref[pl.ds(k0, kc_len), :], eb_ref[pl.ds(k0, kc_len), :]], axis=1)
        s = _dot_nt(lhs_sel, keys)
        if diagonal:
            kpos = k0 + lax.broadcasted_iota(jnp.int32, (m_rows, kc_len), 1)
            s = jnp.where(kpos <= tq_col, s, NEG)
        m_new = jnp.maximum(m_i, jnp.max(s, axis=-1, keepdims=True))
        alpha = jnp.exp(m_i - m_new)
        p = jnp.exp(s - m_new)
        l_new = alpha * l_i + jnp.sum(p, axis=-1, keepdims=True)
        acc_new = alpha * acc + _dot(p.astype(BF16), vs_ref[pl.ds(k0, kc_len), :])
        return m_new, l_new, acc_new

    n_kc = (t0 + tq + kc_len - 1) // kc_len
    init = (jnp.full((m_rows, 1), NEG, F32), jnp.zeros((m_rows, 1), F32),
            jnp.zeros((m_rows, dh), F32))
    carry = lax.fori_loop(0, n_kc - 1, functools.partial(sel_step, diagonal=False), init)
    _, l_s, acc_s = sel_step(n_kc - 1, carry, True)
    o_s = acc_s / jnp.maximum(l_s, 1e-30)

    gates = _sigmoid(gate_ref[...])
    for h in range(hpg):
        base = (g * hpg + h) * NSA_N_BRANCH
        rows = slice(h * tq, (h + 1) * tq)
        out = (_pick_lane(gates, base) * o_c[rows] + _pick_lane(gates, base + 1) * o_s[rows]
               + _pick_lane(gates, base + 2) * o_w[rows])
        o_ref[:, h * dh:(h + 1) * dh] = out.astype(o_ref.dtype)


def nsa_attention(q, gates, cos, sin_a, sin_b, k_cmp, v_cmp, kvr, tables, *, batch, seq, kc_len=512):
    overlap_t, block_onehot, window_bias = tables
    n = q.shape[0]
    dh = LANES
    tq = LANES
    gw = NSA_HPG * dh
    nq = seq // tq
    kc_len = min(kc_len, seq)
    rows = lambda width: pl.BlockSpec((tq, width), lambda b, g, i: (b * nq + i, 0))
    cmp_spec = pl.BlockSpec((1, 1, k_cmp.shape[2], dh), lambda b, g, i: (b, g, 0, 0))
    kv_spec = lambda t: pl.BlockSpec((seq, dh), lambda b, g, i, t=t: (b, t * NSA_GROUPS + g))
    const = lambda shape: pl.BlockSpec(shape, lambda b, g, i: (0,) * len(shape))
    key_tab = pl.BlockSpec((seq, LANES), lambda b, g, i: (b, 0))
    return pl.pallas_call(
        functools.partial(_nsa_kernel, seq=seq, tq=tq, kc_len=kc_len, row_chunk=256),
        grid=(batch, NSA_GROUPS, nq),
        in_specs=[
            pl.BlockSpec((tq, gw), lambda b, g, i: (b * nq + i, g)),
            rows(LANES), rows(LANES), rows(LANES), rows(LANES),
            key_tab, key_tab, key_tab,
            cmp_spec, cmp_spec, kv_spec(2), kv_spec(3), kv_spec(4), kv_spec(5),
            const(overlap_t.shape), const(block_onehot.shape), const(window_bias.shape),
        ],
        out_specs=pl.BlockSpec((tq, gw), lambda b, g, i: (b * nq + i, g)),
        out_shape=jax.ShapeDtypeStruct((n, NSA_HEADS * dh), BF16),
        scratch_shapes=[pltpu.VMEM((seq, dh), BF16), pltpu.VMEM((seq, dh), BF16)],
        compiler_params=_params(3),
    )(q, gates, cos, sin_a, sin_b, cos, sin_a, sin_b, k_cmp, v_cmp, kvr, kvr, kvr, kvr,
      overlap_t, block_onehot, window_bias)


def _rope_tables(positions):
    half = ROPE_DIM // 2
    inv = ROPE_THETA ** (-jnp.arange(0, ROPE_DIM, 2, dtype=F32) / ROPE_DIM)
    ang = positions.astype(F32).reshape(-1, 1) * inv
    cos, sin = jnp.cos(ang), jnp.sin(ang)
    n = ang.shape[0]
    ones = jnp.ones((n, LANES - ROPE_DIM), F32)
    zeros_h = jnp.zeros((n, half), F32)
    zeros_t = jnp.zeros((n, LANES - ROPE_DIM), F32)
    cos_t = jnp.concatenate([cos, cos, ones], axis=1)
    sin_a = jnp.concatenate([-sin, zeros_h, zeros_t], axis=1)
    sin_b = jnp.concatenate([zeros_h, sin, zeros_t], axis=1)
    return cos_t, sin_a, sin_b


def _overlap_table(seq):
    n_cmp = (seq - CMP_BLOCK) // CMP_STRIDE + 1
    n_sel = seq // SEL_BLOCK
    c_idx = jnp.arange(n_cmp)
    j_idx = jnp.arange(n_sel)
    ov = jnp.clip(jnp.minimum(c_idx[:, None] * CMP_STRIDE + CMP_BLOCK, (j_idx[None, :] + 1) * SEL_BLOCK)
                  - jnp.maximum(c_idx[:, None] * CMP_STRIDE, j_idx[None, :] * SEL_BLOCK), 0)
    ov = ov.astype(F32) / CMP_STRIDE
    return jnp.pad(ov, ((0, LANES - n_cmp), (0, LANES - n_sel)))


def _nsa_tables(seq):
    tq = LANES
    key = jnp.arange(seq)
    block_onehot = (key[:, None] // SEL_BLOCK == jnp.arange(LANES)[None, :]).astype(BF16)
    x = jnp.arange(WINDOW + tq)[None, :, None]
    t = (jnp.arange(WINDOW // tq + 1) * tq)[:, None, None] + jnp.arange(tq)[None, None, :]
    window_bias = jnp.where((x <= t) & (x > t - WINDOW), 0.0, NEG).astype(BF16)
    return _overlap_table(seq).T, block_onehot, window_bias


def _split_pad(w, main):
    extra = w[:, main:]
    return w[:, :main].astype(BF16), jnp.pad(extra, ((0, 0), (0, LANES - extra.shape[1]))).astype(BF16)


def kernel(x, positions, attn_norm, ffn_norm, final_norm, gdn_w_in, gdn_conv, gdn_a_log, gdn_dt_bias,
           gdn_out_norm, gdn_w_out, kv_norm, nsa_w_kv, cmp_pos_k, cmp_w1_k, cmp_w2_k, cmp_pos_v, cmp_w1_v,
           cmp_w2_v, nsa_w_q, nsa_w_o, ffn_w_gate, ffn_w_up, ffn_w_down, moe_router, moe_w_gate, moe_w_up,
           moe_w_down):
    batch, seq, d = x.shape
    depth = attn_norm.shape[0]
    n_a = depth // 2
    h = x.reshape(batch * seq, d)
    gdn_width = gdn_w_out.shape[1]
    nsa_width = nsa_w_o.shape[1]
    shared = None
    for l in range(depth):
        if l < n_a:
            w_main, w_gates = _split_pad(gdn_w_in[l], 4 * gdn_width)
            proj, gates = rms_matmul(h, attn_norm[l], w_main, w_gates, out_dtype=BF16)
            o = gdn_core(proj, gates, gdn_conv[l], gdn_a_log[l], gdn_dt_bias[l], gdn_out_norm[l],
                         batch=batch, seq=seq)
            h = matmul_residual(o, gdn_w_out[l].astype(BF16), h)
        else:
            cos, sin_a, sin_b, k_cmp, v_cmp, kvr, tables = shared
            w_main, w_gates = _split_pad(nsa_w_q[l - n_a], nsa_width)
            q, gates = rms_matmul(h, attn_norm[l], w_main, w_gates, out_dtype=BF16)
            o = nsa_attention(q, gates, cos, sin_a, sin_b, k_cmp, v_cmp, kvr, tables,
                              batch=batch, seq=seq)
            h = matmul_residual(o, nsa_w_o[l - n_a].astype(BF16), h)
        final_gain = final_norm if l == depth - 1 else None
        if l % 2 == 0:
            i = l // 2
            h = ffn_residual(h, ffn_norm[l], None, ffn_w_gate[i][None], ffn_w_up[i][None],
                             ffn_w_down[i][None], final_gain)
        else:
            i = l // 2
            comb = route_tokens(h, ffn_norm[l], moe_router[i])
            h = moe_residual(h, ffn_norm[l], comb, moe_w_gate[i], moe_w_up[i], moe_w_down[i], final_gain)
        if l == n_a - 1:
            cos, sin_a, sin_b = _rope_tables(positions)
            kv, _ = rms_matmul(h, kv_norm, nsa_w_kv.astype(BF16), jnp.zeros((d, LANES), BF16),
                               out_dtype=BF16)
            kvr = kv
            k_cmp, v_cmp = nsa_compress(kv, cmp_pos_k, cmp_pos_v, cmp_w1_k.astype(BF16),
                                        cmp_w1_v.astype(BF16), cmp_w2_k.astype(BF16),
                                        cmp_w2_v.astype(BF16), batch=batch, seq=seq)
            shared = (cos, sin_a, sin_b, k_cmp, v_cmp, kvr, _nsa_tables(seq))
    return h.reshape(batch, seq, d)
```

```python
import functools
import math

import jax
import jax.numpy as jnp
from jax import lax
from jax.experimental import pallas as pl
from jax.experimental.pallas import tpu as pltpu

F32 = jnp.float32
BF16 = jnp.bfloat16

NORM_EPS = 1e-6
LANES = 128
GDN_HEADS = 16
GDN_CONV = 4
GDN_CHUNK = 64
GDN_GROUP = 256
GDN_GROUPS_IN_FLIGHT = 4
GDN_HEADS_PER_STEP = 4
NSA_HEADS = 16
NSA_GROUPS = 4
NSA_HPG = NSA_HEADS // NSA_GROUPS
NSA_N_BRANCH = 3
CMP_BLOCK = 32
CMP_STRIDE = 16
SEL_BLOCK = 64
SEL_TOPK = 16
WINDOW = 512
BIG = 1e9
NEG = -1e30
ROPE_THETA = 500000.0
ROPE_DIM = 32
TOP_K = 2

VMEM_LIMIT_BYTES = 56 * 1024 * 1024


def _params(n_axes):
    return pltpu.CompilerParams(dimension_semantics=("arbitrary",) * n_axes,
                                vmem_limit_bytes=VMEM_LIMIT_BYTES)


def _rms(x, gain):
    ms = jnp.mean(x * x, axis=-1, keepdims=True)
    return x * lax.rsqrt(ms + NORM_EPS) * gain


def _sigmoid(x):
    return 1.0 / (1.0 + jnp.exp(-x))


def _dot(a, b):
    return jnp.dot(a, b, preferred_element_type=F32)


def _dot_nt(a, b):
    return lax.dot_general(a, b, (((1,), (1,)), ((), ())), preferred_element_type=F32)


def _dot_tn(a, b):
    return lax.dot_general(a, b, (((0,), (0,)), ((), ())), preferred_element_type=F32)


def _pick_lane(x, idx):
    lane = lax.broadcasted_iota(jnp.int32, x.shape, 1)
    return jnp.sum(jnp.where(lane == idx, x, 0.0), axis=-1, keepdims=True)


def _rms_matmul_kernel(h_ref, g_ref, w_ref, we_ref, o_ref, oe_ref, xn_ref, *, row_chunk):
    j = pl.program_id(1)

    @pl.when(j == 0)
    def _():
        for r in range(0, h_ref.shape[0], row_chunk):
            xn_ref[r:r + row_chunk, :] = _rms(h_ref[r:r + row_chunk, :], g_ref[...]).astype(BF16)
        oe_ref[...] = _dot(xn_ref[...], we_ref[...])

    o_ref[...] = _dot(xn_ref[...], w_ref[...]).astype(o_ref.dtype)


def rms_matmul(h, gain, w, w_extra, *, tm=1024, tn=1024, out_dtype=F32):
    n, d = h.shape
    nm = w.shape[1]
    tm = min(tm, n)
    tn = min(tn, nm)
    return pl.pallas_call(
        functools.partial(_rms_matmul_kernel, row_chunk=256),
        grid=(n // tm, nm // tn),
        in_specs=[
            pl.BlockSpec((tm, d), lambda i, j: (i, 0)),
            pl.BlockSpec((1, d), lambda i, j: (0, 0)),
            pl.BlockSpec((d, tn), lambda i, j: (0, j)),
            pl.BlockSpec((d, LANES), lambda i, j: (0, 0)),
        ],
        out_specs=[
            pl.BlockSpec((tm, tn), lambda i, j: (i, j)),
            pl.BlockSpec((tm, LANES), lambda i, j: (i, 0)),
        ],
        out_shape=[jax.ShapeDtypeStruct((n, nm), out_dtype),
                   jax.ShapeDtypeStruct((n, LANES), F32)],
        scratch_shapes=[pltpu.VMEM((tm, d), BF16)],
        compiler_params=_params(2),
    )(h, gain.reshape(1, d), w, w_extra)


def _matmul_res_kernel(x_ref, w_ref, r_ref, o_ref):
    o_ref[...] = r_ref[...] + _dot(x_ref[...], w_ref[...])


def matmul_residual(x, w, res, *, tm=1024, tn=512):
    n, k = x.shape
    nm = w.shape[1]
    tm = min(tm, n)
    return pl.pallas_call(
        _matmul_res_kernel,
        grid=(n // tm, nm // tn),
        in_specs=[
            pl.BlockSpec((tm, k), lambda i, j: (i, 0)),
            pl.BlockSpec((k, tn), lambda i, j: (0, j)),
            pl.BlockSpec((tm, tn), lambda i, j: (i, j)),
        ],
        out_specs=pl.BlockSpec((tm, tn), lambda i, j: (i, j)),
        out_shape=jax.ShapeDtypeStruct((n, nm), F32),
        compiler_params=_params(2),
    )(x, w, res)


def _router_kernel(h_ref, g_ref, r_ref, o_ref, *, n_experts, row_chunk):
    for r in range(0, h_ref.shape[0], row_chunk):
        xn = _rms(h_ref[r:r + row_chunk, :], g_ref[...])
        lane = lax.broadcasted_iota(jnp.int32, (row_chunk, LANES), 1)
        logits = jnp.zeros((row_chunk, LANES), F32)
        for ex in range(n_experts):
            col = jnp.sum(xn * r_ref[ex:ex + 1, :], axis=-1, keepdims=True)
            logits = jnp.where(lane == ex, col, logits)
        valid = lane < n_experts
        z = jnp.where(valid, logits, NEG)
        z = z - jnp.max(z, axis=-1, keepdims=True)
        p = jnp.where(valid, jnp.exp(z), 0.0)
        p = p / jnp.sum(p, axis=-1, keepdims=True)
        p = jnp.where(valid, p, -1.0)
        p1 = jnp.max(p, axis=-1, keepdims=True)
        i1 = jnp.min(jnp.where(p == p1, lane, LANES), axis=-1, keepdims=True)
        pm = jnp.where(lane == i1, -1.0, p)
        p2 = jnp.max(pm, axis=-1, keepdims=True)
        i2 = jnp.min(jnp.where(pm == p2, lane, LANES), axis=-1, keepdims=True)
        den = p1 + p2
        o_ref[r:r + row_chunk, :] = (jnp.where(lane == i1, p1 / den, 0.0)
                                     + jnp.where(lane == i2, p2 / den, 0.0))


def route_tokens(h, gain, router, *, tm=1024):
    n, d = h.shape
    n_experts = router.shape[1]
    tm = min(tm, n)
    r_rows = router.T
    return pl.pallas_call(
        functools.partial(_router_kernel, n_experts=n_experts, row_chunk=256),
        grid=(n // tm,),
        in_specs=[
            pl.BlockSpec((tm, d), lambda i: (i, 0)),
            pl.BlockSpec((1, d), lambda i: (0, 0)),
            pl.BlockSpec((n_experts, d), lambda i: (0, 0)),
        ],
        out_specs=pl.BlockSpec((tm, LANES), lambda i: (i, 0)),
        out_shape=jax.ShapeDtypeStruct((n, LANES), F32),
        compiler_params=_params(1),
    )(h, gain.reshape(1, d), r_rows)


def _ffn_kernel(h_ref, g_ref, c_ref, wg_ref, wu_ref, wd_ref, fg_ref, o_ref, xn_ref, *,
                n_e, n_f, use_comb, final_norm, row_chunk):
    e = pl.program_id(1)
    f = pl.program_id(2)

    @pl.when((e == 0) & (f == 0))
    def _():
        for r in range(0, h_ref.shape[0], row_chunk):
            x = h_ref[r:r + row_chunk, :]
            xn_ref[r:r + row_chunk, :] = _rms(x, g_ref[...]).astype(BF16)
            o_ref[r:r + row_chunk, :] = x

    xn = xn_ref[...]
    gate = _dot(xn, wg_ref[0])
    up = _dot(xn, wu_ref[0])
    act = gate * _sigmoid(gate) * up
    if use_comb:
        act = act * _pick_lane(c_ref[...], e)
    o_ref[...] += _dot(act.astype(BF16), wd_ref[0])

    if final_norm:
        @pl.when((e == n_e - 1) & (f == n_f - 1))
        def _():
            for r in range(0, h_ref.shape[0], row_chunk):
                o_ref[r:r + row_chunk, :] = _rms(o_ref[r:r + row_chunk, :], fg_ref[...])


def ffn_residual(h, gain, comb, w_gate, w_up, w_down, final_gain, *, tm=512, tf=512):
    n, d = h.shape
    n_e, _, ff = w_gate.shape
    w_gate, w_up, w_down = w_gate.astype(BF16), w_up.astype(BF16), w_down.astype(BF16)
    tm = min(tm, n)
    use_comb = comb is not None
    final_norm = final_gain is not None
    if comb is None:
        comb = jnp.ones((n, LANES), F32)
    if final_gain is None:
        final_gain = jnp.ones((d,), F32)
    n_f = ff // tf
    return pl.pallas_call(
        functools.partial(_ffn_kernel, n_e=n_e, n_f=n_f, use_comb=use_comb,
                          final_norm=final_norm, row_chunk=256),
        grid=(n // tm, n_e, n_f),
        in_specs=[
            pl.BlockSpec((tm, d), lambda i, e, f: (i, 0)),
            pl.BlockSpec((1, d), lambda i, e, f: (0, 0)),
            pl.BlockSpec((tm, LANES), lambda i, e, f: (i, 0)),
            pl.BlockSpec((1, d, tf), lambda i, e, f: (e, 0, f)),
            pl.BlockSpec((1, d, tf), lambda i, e, f: (e, 0, f)),
            pl.BlockSpec((1, tf, d), lambda i, e, f: (e, f, 0)),
            pl.BlockSpec((1, d), lambda i, e, f: (0, 0)),
        ],
        out_specs=pl.BlockSpec((tm, d), lambda i, e, f: (i, 0)),
        out_shape=jax.ShapeDtypeStruct((n, d), F32),
        scratch_shapes=[pltpu.VMEM((tm, d), BF16)],
        compiler_params=_params(3),
    )(h, gain.reshape(1, d), comb, w_gate, w_up, w_down, final_gain.reshape(1, d))


def _moe_kernel(h_ref, g_ref, c_ref, ct_ref, wg_ref, wu_ref, wd_ref, fg_ref, o_ref,
                xn_ref, rc_ref, rr_ref, xs_ref, ye_ref, *, n_e, n_f, cap, final_norm, row_chunk):
    e = pl.program_id(1)
    f = pl.program_id(2)
    tm, d = h_ref.shape
    col_chunk = 512

    @pl.when((e == 0) & (f == 0))
    def _():
        for r in range(0, tm, row_chunk):
            x = h_ref[r:r + row_chunk, :]
            xn_ref[r:r + row_chunk, :] = _rms(x, g_ref[...]).astype(BF16)
            o_ref[r:r + row_chunk, :] = x
        ti = lax.broadcasted_iota(jnp.int32, (tm, tm), 0)
        tj = lax.broadcasted_iota(jnp.int32, (tm, tm), 1)
        routed_c = jnp.where(c_ref[...] > 0.0, 1.0, 0.0).astype(BF16)
        routed_r = jnp.where(ct_ref[...] > 0.0, 1.0, 0.0).astype(BF16)
        rc_ref[...] = _dot(jnp.where(tj < ti, 1.0, 0.0).astype(BF16), routed_c)
        rr_ref[...] = _dot(routed_r, jnp.where(ti < tj, 1.0, 0.0).astype(BF16))

    key_row = jnp.where(ct_ref[pl.ds(e, 1), :] > 0.0, rr_ref[pl.ds(e, 1), :], -1.0)
    w_row = ct_ref[pl.ds(e, 1), :]
    n_routed = jnp.sum(jnp.where(key_row >= 0.0, 1.0, 0.0)).astype(jnp.int32)
    n_rounds = (n_routed + cap - 1) // cap

    def gather_mask(rnd):
        slot = lax.broadcasted_iota(jnp.int32, (cap, tm), 0) + rnd * cap
        return key_row == slot.astype(F32)

    def expert(xs):
        gate = _dot(xs, wg_ref[0])
        up = _dot(xs, wu_ref[0])
        return _dot((gate * _sigmoid(gate) * up).astype(BF16), wd_ref[0])

    def scatter_add(rnd, y):
        picked = gather_mask(rnd)
        w_slot = jnp.sum(jnp.where(picked, w_row, 0.0), axis=-1, keepdims=True)
        yw = (y * w_slot).astype(BF16)
        key_col = jnp.where(_pick_lane(c_ref[...], e) > 0.0, _pick_lane(rc_ref[...], e), -1.0)
        slot = lax.broadcasted_iota(jnp.int32, (tm, cap), 1) + rnd * cap
        back = jnp.where(key_col == slot.astype(F32), 1.0, 0.0).astype(BF16)
        for c0 in range(0, d, col_chunk):
            o_ref[:, c0:c0 + col_chunk] += _dot(back, yw[:, c0:c0 + col_chunk])

    @pl.when(f == 0)
    def _():
        xs_ref[...] = _dot(jnp.where(gather_mask(0), 1.0, 0.0).astype(BF16), xn_ref[...]).astype(BF16)

    part = expert(xs_ref[...])

    @pl.when(f == 0)
    def _():
        ye_ref[...] = part

    @pl.when(f > 0)
    def _():
        ye_ref[...] += part

    @pl.when(f == n_f - 1)
    def _():
        scatter_add(0, ye_ref[...])

    def extra_round(rnd, carry):
        xs = _dot(jnp.where(gather_mask(rnd), 1.0, 0.0).astype(BF16), xn_ref[...]).astype(BF16)
        scatter_add(rnd, expert(xs))
        return carry

    lax.fori_loop(1, n_rounds, extra_round, 0)

    if final_norm:
        @pl.when((e == n_e - 1) & (f == n_f - 1))
        def _():
            for r in range(0, tm, row_chunk):
                o_ref[r:r + row_chunk, :] = _rms(o_ref[r:r + row_chunk, :], fg_ref[...])


def moe_residual(h, gain, comb, w_gate, w_up, w_down, final_gain, *, tm=1024, tf=1024, cap=320):
    n, d = h.shape
    n_e, _, ff = w_gate.shape
    w_gate, w_up, w_down = w_gate.astype(BF16), w_up.astype(BF16), w_down.astype(BF16)
    tm = min(tm, n)
    final_norm = final_gain is not None
    if final_gain is None:
        final_gain = jnp.ones((d,), F32)
    n_f = ff // tf
    once = pl.Buffered(1)
    return pl.pallas_call(
        functools.partial(_moe_kernel, n_e=n_e, n_f=n_f, cap=cap, final_norm=final_norm,
                          row_chunk=256),
        grid=(n // tm, n_e, n_f),
        in_specs=[
            pl.BlockSpec((tm, d), lambda i, e, f: (i, 0), pipeline_mode=once),
            pl.BlockSpec((1, d), lambda i, e, f: (0, 0)),
            pl.BlockSpec((tm, LANES), lambda i, e, f: (i, 0), pipeline_mode=once),
            pl.BlockSpec((LANES, tm), lambda i, e, f: (0, i), pipeline_mode=once),
            pl.BlockSpec((1, d, tf), lambda i, e, f: (e, 0, f)),
            pl.BlockSpec((1, d, tf), lambda i, e, f: (e, 0, f)),
            pl.BlockSpec((1, tf, d), lambda i, e, f: (e, f, 0)),
            pl.BlockSpec((1, d), lambda i, e, f: (0, 0)),
        ],
        out_specs=pl.BlockSpec((tm, d), lambda i, e, f: (i, 0), pipeline_mode=once),
        out_shape=jax.ShapeDtypeStruct((n, d), F32),
        scratch_shapes=[pltpu.VMEM((tm, d), BF16), pltpu.VMEM((tm, LANES), F32),
                        pltpu.VMEM((LANES, tm), F32), pltpu.VMEM((cap, d), BF16),
                        pltpu.VMEM((cap, d), F32)],
        compiler_params=_params(3),
    )(h, gain.reshape(1, d), comb, comb.T, w_gate, w_up, w_down, final_gain.reshape(1, d))


def _tri_inv_many(lows, nilpotency):
    c = lows[0].shape[0]
    ii = lax.broadcasted_iota(jnp.int32, (c, c), 0)
    jj = lax.broadcasted_iota(jnp.int32, (c, c), 1)
    eye = jnp.where(ii == jj, 1.0, 0.0)
    lb = [low.astype(BF16) for low in lows]
    ps = [_dot(b, b).astype(BF16) for b in lb]
    ts = [eye - low for low in lows]
    levels = int(math.log2(nilpotency)) - 1
    for level in range(levels):
        tb = [t.astype(BF16) for t in ts]
        if level + 1 < levels:
            nxt = [_dot(p, p).astype(BF16) for p in ps]
        ts = [t + _dot(b, p) for t, b, p in zip(ts, tb, ps)]
        if level + 1 < levels:
            ps = nxt
    return ts


def _gdn_local_kernel(alog_ref, dtb_ref, q_ref, k_ref, v_ref, cq_ref, ck_ref, cv_ref,
                      a1_ref, a2_ref, b2_ref,
                      u_ref, w_ref, qd_ref, kd_ref, aqk_ref, gl_ref,
                      xp_ref, qs_ref, ks_ref, vs_ref, gc1_ref, gc2_ref, gl2_ref, be2_ref, *,
                      seq, row_chunk):
    hd = pl.program_id(1)
    dk = q_ref.shape[1]
    c = GDN_CHUNK
    grp = GDN_GROUP
    n_groups = seq // grp
    pad = 8

    def conv_silu(x_ref, w_ref, dst_ref, normalise, scale):
        xp_ref[0:pad, :] = jnp.zeros((pad, dk), F32)
        xp_ref[pad:pad + seq, :] = x_ref[...].astype(F32)
        w = w_ref[...]
        for r in range(0, seq, row_chunk):
            y = xp_ref[pad - 3 + r:pad - 3 + r + row_chunk, :] * w[0:1, :]
            for j in range(1, GDN_CONV):
                y = y + xp_ref[pad - 3 + j + r:pad - 3 + j + r + row_chunk, :] * w[j:j + 1, :]
            y = y * _sigmoid(y)
            if normalise:
                y = y * lax.rsqrt(jnp.sum(y * y, axis=-1, keepdims=True) + NORM_EPS)
            if scale != 1.0:
                y = y * scale
            dst_ref[r:r + row_chunk, :] = y

    conv_silu(q_ref, cq_ref, qs_ref, True, dk ** -0.5)
    conv_silu(k_ref, ck_ref, ks_ref, True, 1.0)
    conv_silu(v_ref, cv_ref, vs_ref, False, 1.0)

    a_coef = -jnp.exp(alog_ref[hd])
    dtb = dtb_ref[hd]

    def log_decay(a):
        x = a + dtb
        return a_coef * (jnp.maximum(x, 0.0) + jnp.log(1.0 + jnp.exp(-jnp.abs(x))))

    ii = lax.broadcasted_iota(jnp.int32, (grp, grp), 0)
    jj = lax.broadcasted_iota(jnp.int32, (grp, grp), 1)
    same = (ii // c) == (jj // c)
    hi = lax.Precision.HIGHEST
    g1 = log_decay(a1_ref[0, 0])
    g2 = log_decay(a2_ref[0, 0])
    ones_b = jnp.where(same, 1.0, 0.0)
    gc1_ref[...] = jnp.dot(g1, jnp.where(same & (ii <= jj), 1.0, 0.0),
                           preferred_element_type=F32, precision=hi)
    gc2_ref[...] = jnp.dot(jnp.where(same & (jj <= ii), 1.0, 0.0), g2,
                           preferred_element_type=F32, precision=hi)
    gl2_ref[...] = jnp.dot(ones_b, g2, preferred_element_type=F32, precision=hi)
    gl_ref[0, 0] = jnp.exp(jnp.dot(g1, ones_b, preferred_element_type=F32, precision=hi))
    be2_ref[...] = _sigmoid(b2_ref[0, 0])

    causal = same & (ii >= jj)
    strict = same & (ii > jj)
    n_par = GDN_GROUPS_IN_FLIGHT

    def groups(it, carry):
        gis = [it * n_par + t for t in range(n_par)]
        rows = [pl.ds(pl.multiple_of(gi * grp, grp), grp) for gi in gis]
        q = [qs_ref[r, :] for r in rows]
        k = [ks_ref[r, :] for r in rows]
        v = [vs_ref[r, :] for r in rows]
        g_row = [gc1_ref[pl.ds(gi, 1), :] for gi in gis]
        g_col = [_pick_lane(gc2_ref[...], gi) for gi in gis]
        gl_col = [_pick_lane(gl2_ref[...], gi) for gi in gis]
        b_col = [_pick_lane(be2_ref[...], gi) for gi in gis]
        kbf = [x.astype(BF16) for x in k]
        kb = [x * b for x, b in zip(k, b_col)]
        kk = [_dot_nt(x.astype(BF16), y) for x, y in zip(kb, kbf)]
        qk = [_dot_nt(x.astype(BF16), y) for x, y in zip(q, kbf)]
        decay = [jnp.where(causal, jnp.exp(jnp.where(causal, gc - gr, 0.0)), 0.0)
                 for gc, gr in zip(g_col, g_row)]
        t_inv = _tri_inv_many([jnp.where(strict, x * d, 0.0) for x, d in zip(kk, decay)], c)
        e_col = [jnp.exp(gc) for gc in g_col]
        rhs = [jnp.concatenate([x * b, y * e], axis=1).astype(BF16)
               for x, b, y, e in zip(v, b_col, kb, e_col)]
        uw = [_dot(t.astype(BF16), r) for t, r in zip(t_inv, rhs)]
        for t in range(n_par):
            u_ref[rows[t], :] = uw[t][:, :dk].astype(u_ref.dtype)
            w_ref[rows[t], :] = uw[t][:, dk:].astype(w_ref.dtype)
            a_qk = qk[t] * decay[t]
            a_c = a_qk[:, 0:c]
            for blk in range(1, grp // c):
                a_c = a_c + a_qk[:, blk * c:(blk + 1) * c]
            aqk_ref[0, 0, rows[t], :] = a_c.astype(aqk_ref.dtype)
            qd_ref[rows[t], :] = (q[t] * e_col[t]).astype(qd_ref.dtype)
            kd_ref[rows[t], :] = (k[t] * jnp.exp(gl_col[t] - g_col[t])).astype(kd_ref.dtype)
        return carry

    lax.fori_loop(0, n_groups // n_par, groups, 0)


def _gdn_scan_kernel(gl_ref, u_ref, w_ref, qd_ref, kd_ref, aqk_ref, z_ref, on_ref, o_ref, *,
                     seq, heads_per_step):
    b = pl.program_id(0)
    hb = pl.program_id(1)
    c = GDN_CHUNK
    dk = on_ref.shape[1]

    def step(ci, states):
        rows = pl.ds(pl.multiple_of(ci * c, c), c)
        hs = range(heads_per_step)
        cols = [slice(hh * dk, (hh + 1) * dk) for hh in hs]
        ws_qs = [_dot(jnp.concatenate([w_ref[rows, cols[hh]], qd_ref[rows, cols[hh]]], axis=0),
                      states[hh].astype(BF16)) for hh in hs]
        v_new = [(u_ref[rows, cols[hh]].astype(F32) - ws_qs[hh][:c]).astype(BF16) for hh in hs]
        intra = [_dot(aqk_ref[0, hh, rows, :], v_new[hh]) for hh in hs]
        outer = [_dot_tn(kd_ref[rows, cols[hh]], v_new[hh]) for hh in hs]
        new_states = []
        for hh in hs:
            decay = gl_ref[b, hb * heads_per_step + hh, ci]
            new_states.append(states[hh] * decay + outer[hh])
            z = z_ref[rows, cols[hh]].astype(F32)
            o = ws_qs[hh][c:] + intra[hh]
            o_ref[rows, cols[hh]] = (_rms(o, on_ref[...]) * (z * _sigmoid(z))).astype(o_ref.dtype)
        return tuple(new_states)

    lax.fori_loop(0, seq // c, step,
                  tuple(jnp.zeros((dk, dk), F32) for _ in range(heads_per_step)))


def gdn_core(proj, gates, conv_w, a_log, dt_bias, out_norm, *, batch, seq):
    n = proj.shape[0]
    nh = GDN_HEADS
    dk = proj.shape[1] // (4 * nh)
    c = GDN_CHUNK
    grp = GDN_GROUP
    n_groups = seq // grp
    hps = GDN_HEADS_PER_STEP
    bl = gates[:, :nh].reshape(batch, seq, nh).transpose(0, 2, 1)
    al = gates[:, nh:2 * nh].reshape(batch, seq, nh).transpose(0, 2, 1)
    a1 = al.reshape(batch, nh, n_groups, grp)
    a2 = a1.transpose(0, 1, 3, 2)
    b2 = bl.reshape(batch, nh, n_groups, grp).transpose(0, 1, 3, 2)
    smem = pl.BlockSpec(memory_space=pltpu.SMEM)
    col = lambda off: pl.BlockSpec((seq, dk), lambda b, h, off=off: (b, off * nh + h))
    cw = lambda off: pl.BlockSpec((GDN_CONV, dk), lambda b, h, off=off: (0, off * nh + h))
    g1 = pl.BlockSpec((1, 1, n_groups, grp), lambda b, h: (b, h, 0, 0))
    g2 = pl.BlockSpec((1, 1, grp, n_groups), lambda b, h: (b, h, 0, 0))
    head_out = pl.BlockSpec((seq, dk), lambda b, h: (b, h))
    wide = jax.ShapeDtypeStruct((n, nh * dk), BF16)
    u, w, qd, kd, aqk, gl = pl.pallas_call(
        functools.partial(_gdn_local_kernel, seq=seq, row_chunk=256),
        grid=(batch, nh),
        in_specs=[smem, smem, col(0), col(1), col(2), cw(0), cw(1), cw(2), g1, g2, g2],
        out_specs=[head_out, head_out, head_out, head_out,
                   pl.BlockSpec((1, 1, seq, c), lambda b, h: (b, h, 0, 0)), g1],
        out_shape=[wide, wide, wide, wide,
                   jax.ShapeDtypeStruct((batch, nh, seq, c), BF16),
                   jax.ShapeDtypeStruct((batch, nh, n_groups, grp), F32)],
        scratch_shapes=[
            pltpu.VMEM((seq + 8, dk), F32),
            pltpu.VMEM((seq, dk), F32), pltpu.VMEM((seq, dk), F32), pltpu.VMEM((seq, dk), F32),
            pltpu.VMEM((n_groups, grp), F32), pltpu.VMEM((grp, n_groups), F32),
            pltpu.VMEM((grp, n_groups), F32), pltpu.VMEM((grp, n_groups), F32),
        ],
        compiler_params=_params(2),
    )(a_log, dt_bias, proj, proj, proj, conv_w, conv_w, conv_w, a1, a2, b2)
    chunk_decay = gl[..., c - 1::c].reshape(batch, nh, seq // c)
    heads = pl.BlockSpec((seq, hps * dk), lambda b, h: (b, h))
    return pl.pallas_call(
        functools.partial(_gdn_scan_kernel, seq=seq, heads_per_step=hps),
        grid=(batch, nh // hps),
        in_specs=[smem, heads, heads, heads, heads,
                  pl.BlockSpec((1, hps, seq, c), lambda b, h: (b, h, 0, 0)),
                  pl.BlockSpec((seq, hps * dk), lambda b, h: (b, 3 * (nh // hps) + h)),
                  pl.BlockSpec((1, dk), lambda b, h: (0, 0))],
        out_specs=heads,
        out_shape=wide,
        compiler_params=_params(2),
    )(chunk_decay, u, w, qd, kd, aqk, proj, out_norm.reshape(1, dk))


def _rope(x, cos, sin_a, sin_b):
    half = ROPE_DIM // 2
    return (x * cos + pltpu.roll(x, LANES - half, 1) * sin_a + pltpu.roll(x, half, 1) * sin_b)


def _compress_kernel(kc_ref, vc_ref, pk_ref, pv_ref, w1k_ref, w1v_ref, w2k_ref, w2v_ref,
                     ok_ref, ov_ref, tp_ref, *, seq):
    dh = kc_ref.shape[1]
    n_rows = ok_ref.shape[2]
    hidden = w1k_ref.shape[1]

    def run(t_ref, pos_ref, w1_ref, w2_ref, out_ref):
        tp_ref[0:seq, :] = t_ref[...].astype(F32)
        tp_ref[seq:seq + CMP_STRIDE, :] = jnp.zeros((CMP_STRIDE, dh), F32)
        acc = jnp.zeros((n_rows, hidden), F32)
        for l in range(CMP_BLOCK):
            x = tp_ref[pl.ds(l, n_rows, stride=CMP_STRIDE), :] + pos_ref[l:l + 1, :]
            acc = acc + _dot(x.astype(BF16), w1_ref[l * dh:(l + 1) * dh, :])
        hid = acc * _sigmoid(acc)
        out_ref[0, 0] = _dot(hid.astype(BF16), w2_ref[...]).astype(out_ref.dtype)

    run(kc_ref, pk_ref, w1k_ref, w2k_ref, ok_ref)
    run(vc_ref, pv_ref, w1v_ref, w2v_ref, ov_ref)


def nsa_compress(kv, pos_k, pos_v, w1k, w1v, w2k, w2v, *, batch, seq):
    dh = pos_k.shape[1]
    n_rows = seq // CMP_STRIDE
    hidden = w1k.shape[1]
    full = lambda shape: pl.BlockSpec(shape, lambda b, g: (0,) * len(shape))
    out = pl.BlockSpec((1, 1, n_rows, dh), lambda b, g: (b, g, 0, 0))
    return pl.pallas_call(
        functools.partial(_compress_kernel, seq=seq),
        grid=(batch, NSA_GROUPS),
        in_specs=[
            pl.BlockSpec((seq, dh), lambda b, g: (b, g)),
            pl.BlockSpec((seq, dh), lambda b, g: (b, NSA_GROUPS + g)),
            full((CMP_BLOCK, dh)), full((CMP_BLOCK, dh)),
            full((CMP_BLOCK * dh, hidden)), full((CMP_BLOCK * dh, hidden)),
            full((hidden, dh)), full((hidden, dh)),
        ],
        out_specs=[out, out],
        out_shape=[jax.ShapeDtypeStruct((batch, NSA_GROUPS, n_rows, dh), BF16)] * 2,
        scratch_shapes=[pltpu.VMEM((seq + CMP_STRIDE, dh), F32)],
        compiler_params=_params(2),
    )(kv, kv, pos_k, pos_v, w1k, w1v, w2k, w2v)


def _masked_softmax(s, mask):
    sm = jnp.where(mask, s, NEG)
    m = jnp.max(sm, axis=-1, keepdims=True)
    p = jnp.where(mask, jnp.exp(sm - m), 0.0)
    return p / jnp.maximum(jnp.sum(p, axis=-1, keepdims=True), 1e-30)


def _nsa_kernel(q_ref, gate_ref, cos_ref, sa_ref, sb_ref, kcos_ref, ksa_ref, ksb_ref,
                kc_ref, vc_ref, ks_raw_ref, vs_ref, kw_raw_ref, vw_ref, ovt_ref, eb_ref, wb_ref,
                o_ref, ks_ref, kw_ref, *, seq, tq, kc_len, row_chunk):
    g = pl.program_id(1)
    n = pl.program_id(2)
    t0 = n * tq
    dh = LANES
    hpg = NSA_HPG
    m_rows = hpg * tq
    n_sel = seq // SEL_BLOCK
    topk = min(SEL_TOPK, n_sel)
    scale = dh ** -0.5

    stack = lambda x: jnp.concatenate([x] * hpg, axis=0)
    @pl.when(n == 0)
    def _():
        for r in range(0, seq, row_chunk):
            rs = slice(r, r + row_chunk)
            tabs = (kcos_ref[rs, :], ksa_ref[rs, :], ksb_ref[rs, :])
            ks_ref[rs, :] = _rope(ks_raw_ref[rs, :].astype(F32), *tabs).astype(BF16)
            kw_ref[rs, :] = _rope(kw_raw_ref[rs, :].astype(F32), *tabs).astype(BF16)

    q = q_ref[...].astype(F32)
    qs = jnp.concatenate([q[:, h * dh:(h + 1) * dh] for h in range(hpg)], axis=0) * scale
    qr = _rope(qs, stack(cos_ref[...]), stack(sa_ref[...]), stack(sb_ref[...]))
    qb = qs.astype(BF16)
    qrb = qr.astype(BF16)
    row = lax.broadcasted_iota(jnp.int32, (m_rows, 1), 0)
    tq_col = t0 + (row & (tq - 1))

    wk = WINDOW + tq
    start = pl.multiple_of(jnp.maximum(t0 - WINDOW, 0), LANES)
    case = jnp.minimum(n, WINDOW // tq)
    row_id = lax.broadcasted_iota(jnp.int32, (m_rows, LANES), 0) & (tq - 1)
    lane_id = lax.broadcasted_iota(jnp.int32, (m_rows, LANES), 1)
    lhs_w = jnp.concatenate([qrb, jnp.where(row_id == lane_id, 1.0, 0.0).astype(BF16)], axis=1)
    keys_w = jnp.concatenate([kw_ref[pl.ds(start, wk), :], wb_ref[case]], axis=1)
    s_c = _dot_nt(qb, kc_ref[0, 0])
    s_w = _dot_nt(lhs_w, keys_w)
    lane_c = lax.broadcasted_iota(jnp.int32, s_c.shape, 1)
    p_c = _masked_softmax(s_c, lane_c * CMP_STRIDE + (CMP_BLOCK - 1) <= tq_col)
    p_w = jnp.exp(s_w - jnp.max(s_w, axis=-1, keepdims=True))
    l_w = jnp.maximum(jnp.sum(p_w, axis=-1, keepdims=True), 1e-30)
    o_c = _dot(p_c.astype(BF16), vc_ref[0, 0])
    o_w = _dot(p_w.astype(BF16), vw_ref[pl.ds(start, wk), :]) / l_w

    p_sum = p_c[0:tq]
    for h in range(1, hpg):
        p_sum = p_sum + p_c[h * tq:(h + 1) * tq]
    imp_t = lax.dot_general(ovt_ref[...], p_sum, (((1,), (1,)), ((), ())),
                            preferred_element_type=F32, precision=lax.Precision.HIGHEST)
    nsp = -(-n_sel // 8) * 8
    imp_t = imp_t[0:nsp]
    blk = lax.broadcasted_iota(jnp.int32, (nsp, tq), 0)
    t1 = t0 + lax.broadcasted_iota(jnp.int32, (nsp, tq), 1)
    cur = t1 // SEL_BLOCK
    forced = (blk == 0) | (blk == cur) | (blk == cur - 1)
    visible = blk * SEL_BLOCK <= t1
    score = jnp.where(forced, BIG, jnp.where(visible, imp_t, -BIG))
    score = jnp.where(blk < n_sel, score, -3e38)
    rank = jnp.zeros((nsp, tq), F32)
    for j2 in range(n_sel):
        other = score[j2:j2 + 1, :]
        rank = rank + jnp.where(other > score, 1.0,
                                jnp.where((other == score) & (blk > j2), 1.0, 0.0))
    chosen = (rank < topk) & visible & (blk < n_sel)
    bias_t = jnp.where(chosen, 0.0, NEG)
    if nsp < LANES:
        bias_t = jnp.concatenate([bias_t, jnp.full((LANES - nsp, tq), NEG, F32)], axis=0)
    sel_bias = stack(bias_t.T).astype(BF16)

    lhs_sel = jnp.concatenate([qrb, sel_bias], axis=1)

    def sel_step(ci, carry, diagonal):
        m_i, l_i, acc = carry
        k0 = pl.multiple_of(ci * kc_len, kc_len)
        keys = jnp.concatenate([ks_ref[pl.ds(k0, kc_len), :], eb_ref[pl.ds(k0, kc_len), :]], axis=1)
        s = _dot_nt(lhs_sel, keys)
        if diagonal:
            kpos = k0 + lax.broadcasted_iota(jnp.int32, (m_rows, kc_len), 1)
            s = jnp.where(kpos <= tq_col, s, NEG)
        m_new = jnp.maximum(m_i, jnp.max(s, axis=-1, keepdims=True))
        alpha = jnp.exp(m_i - m_new)
        p = jnp.exp(s - m_new)
        l_new = alpha * l_i + jnp.sum(p, axis=-1, keepdims=True)
        acc_new = alpha * acc + _dot(p.astype(BF16), vs_ref[pl.ds(k0, kc_len), :])
        return m_new, l_new, acc_new

    n_kc = (t0 + tq + kc_len - 1) // kc_len
    init = (jnp.full((m_rows, 1), NEG, F32), jnp.zeros((m_rows, 1), F32),
            jnp.zeros((m_rows, dh), F32))
    carry = lax.fori_loop(0, n_kc - 1, functools.partial(sel_step, diagonal=False), init)
    _, l_s, acc_s = sel_step(n_kc - 1, carry, True)
    o_s = acc_s / jnp.maximum(l_s, 1e-30)

    gates = _sigmoid(gate_ref[...])
    for h in range(hpg):
        base = (g * hpg + h) * NSA_N_BRANCH
        rows = slice(h * tq, (h + 1) * tq)
        out = (_pick_lane(gates, base) * o_c[rows] + _pick_lane(gates, base + 1) * o_s[rows]
               + _pick_lane(gates, base + 2) * o_w[rows])
        o_ref[:, h * dh:(h + 1) * dh] = out.astype(o_ref.dtype)


def nsa_attention(q, gates, cos, sin_a, sin_b, k_cmp, v_cmp, kvr, tables, *, batch, seq, kc_len=512):
    overlap_t, block_onehot, window_bias = tables
    n = q.shape[0]
    dh = LANES
    tq = LANES
    gw = NSA_HPG * dh
    nq = seq // tq
    kc_len = min(kc_len, seq)
    rows = lambda width: pl.BlockSpec((tq, width), lambda b, g, i: (b * nq + i, 0))
    cmp_spec = pl.BlockSpec((1, 1, k_cmp.shape[2], dh), lambda b, g, i: (b, g, 0, 0))
    kv_spec = lambda t: pl.BlockSpec((seq, dh), lambda b, g, i, t=t: (b, t * NSA_GROUPS + g))
    const = lambda shape: pl.BlockSpec(shape, lambda b, g, i: (0,) * len(shape))
    key_tab = pl.BlockSpec((seq, LANES), lambda b, g, i: (b, 0))
    return pl.pallas_call(
        functools.partial(_nsa_kernel, seq=seq, tq=tq, kc_len=kc_len, row_chunk=256),
        grid=(batch, NSA_GROUPS, nq),
        in_specs=[
            pl.BlockSpec((tq, gw), lambda b, g, i: (b * nq + i, g)),
            rows(LANES), rows(LANES), rows(LANES), rows(LANES),
            key_tab, key_tab, key_tab,
            cmp_spec, cmp_spec, kv_spec(2), kv_spec(3), kv_spec(4), kv_spec(5),
            const(overlap_t.shape), const(block_onehot.shape), const(window_bias.shape),
        ],
        out_specs=pl.BlockSpec((tq, gw), lambda b, g, i: (b * nq + i, g)),
        out_shape=jax.ShapeDtypeStruct((n, NSA_HEADS * dh), BF16),
        scratch_shapes=[pltpu.VMEM((seq, dh), BF16), pltpu.VMEM((seq, dh), BF16)],
        compiler_params=_params(3),
    )(q, gates, cos, sin_a, sin_b, cos, sin_a, sin_b, k_cmp, v_cmp, kvr, kvr, kvr, kvr,
      overlap_t, block_onehot, window_bias)


def _rope_tables(positions):
    half = ROPE_DIM // 2
    inv = ROPE_THETA ** (-jnp.arange(0, ROPE_DIM, 2, dtype=F32) / ROPE_DIM)
    ang = positions.astype(F32).reshape(-1, 1) * inv
    cos, sin = jnp.cos(ang), jnp.sin(ang)
    n = ang.shape[0]
    ones = jnp.ones((n, LANES - ROPE_DIM), F32)
    zeros_h = jnp.zeros((n, half), F32)
    zeros_t = jnp.zeros((n, LANES - ROPE_DIM), F32)
    cos_t = jnp.concatenate([cos, cos, ones], axis=1)
    sin_a = jnp.concatenate([-sin, zeros_h, zeros_t], axis=1)
    sin_b = jnp.concatenate([zeros_h, sin, zeros_t], axis=1)
    return cos_t, sin_a, sin_b


def _overlap_table(seq):
    n_cmp = (seq - CMP_BLOCK) // CMP_STRIDE + 1
    n_sel = seq // SEL_BLOCK
    c_idx = jnp.arange(n_cmp)
    j_idx = jnp.arange(n_sel)
    ov = jnp.clip(jnp.minimum(c_idx[:, None] * CMP_STRIDE + CMP_BLOCK, (j_idx[None, :] + 1) * SEL_BLOCK)
                  - jnp.maximum(c_idx[:, None] * CMP_STRIDE, j_idx[None, :] * SEL_BLOCK), 0)
    ov = ov.astype(F32) / CMP_STRIDE
    return jnp.pad(ov, ((0, LANES - n_cmp), (0, LANES - n_sel)))


def _nsa_tables(seq):
    tq = LANES
    key = jnp.arange(seq)
    block_onehot = (key[:, None] // SEL_BLOCK == jnp.arange(LANES)[None, :]).astype(BF16)
    x = jnp.arange(WINDOW + tq)[None, :, None]
    t = (jnp.arange(WINDOW // tq + 1) * tq)[:, None, None] + jnp.arange(tq)[None, None, :]
    window_bias = jnp.where((x <= t) & (x > t - WINDOW), 0.0, NEG).astype(BF16)
    return _overlap_table(seq).T, block_onehot, window_bias


def _split_pad(w, main):
    extra = w[:, main:]
    return w[:, :main].astype(BF16), jnp.pad(extra, ((0, 0), (0, LANES - extra.shape[1]))).astype(BF16)


def kernel(x, positions, attn_norm, ffn_norm, final_norm, gdn_w_in, gdn_conv, gdn_a_log, gdn_dt_bias,
           gdn_out_norm, gdn_w_out, kv_norm, nsa_w_kv, cmp_pos_k, cmp_w1_k, cmp_w2_k, cmp_pos_v, cmp_w1_v,
           cmp_w2_v, nsa_w_q, nsa_w_o, ffn_w_gate, ffn_w_up, ffn_w_down, moe_router, moe_w_gate, moe_w_up,
           moe_w_down):
    batch, seq, d = x.shape
    depth = attn_norm.shape[0]
    n_a = depth // 2
    h = x.reshape(batch * seq, d)
    gdn_width = gdn_w_out.shape[1]
    nsa_width = nsa_w_o.shape[1]
    shared = None
    for l in range(depth):
        if l < n_a:
            w_main, w_gates = _split_pad(gdn_w_in[l], 4 * gdn_width)
            proj, gates = rms_matmul(h, attn_norm[l], w_main, w_gates, out_dtype=BF16)
            o = gdn_core(proj, gates, gdn_conv[l], gdn_a_log[l], gdn_dt_bias[l], gdn_out_norm[l],
                         batch=batch, seq=seq)
            h = matmul_residual(o, gdn_w_out[l].astype(BF16), h)
        else:
            cos, sin_a, sin_b, k_cmp, v_cmp, kvr, tables = shared
            w_main, w_gates = _split_pad(nsa_w_q[l - n_a], nsa_width)
            q, gates = rms_matmul(h, attn_norm[l], w_main, w_gates, out_dtype=BF16)
            o = nsa_attention(q, gates, cos, sin_a, sin_b, k_cmp, v_cmp, kvr, tables,
                              batch=batch, seq=seq)
            h = matmul_residual(o, nsa_w_o[l - n_a].astype(BF16), h)
        final_gain = final_norm if l == depth - 1 else None
        if l % 2 == 0:
            i = l // 2
            h = ffn_residual(h, ffn_norm[l], None, ffn_w_gate[i][None], ffn_w_up[i][None],
                             ffn_w_down[i][None], final_gain)
        else:
            i = l // 2
            comb = route_tokens(h, ffn_norm[l], moe_router[i])
            h = moe_residual(h, ffn_norm[l], comb, moe_w_gate[i], moe_w_up[i], moe_w_down[i], final_gain)
        if l == n_a - 1:
            cos, sin_a, sin_b = _rope_tables(positions)
            kv, _ = rms_matmul(h, kv_norm, nsa_w_kv.astype(BF16), jnp.zeros((d, LANES), BF16),
                               out_dtype=BF16)
            kvr = kv
            k_cmp, v_cmp = nsa_compress(kv, cmp_pos_k, cmp_pos_v, cmp_w1_k.astype(BF16),
                                        cmp_w1_v.astype(BF16), cmp_w2_k.astype(BF16),
                                        cmp_w2_v.astype(BF16), batch=batch, seq=seq)
            shared = (cos, sin_a, sin_b, k_cmp, v_cmp, kvr, _nsa_tables(seq))
    return h.reshape(batch, seq, d)
```

```python
import functools
import math

import jax
import jax.numpy as jnp
from jax import lax
from jax.experimental import pallas as pl
from jax.experimental.pallas import tpu as pltpu

F32 = jnp.float32
BF16 = jnp.bfloat16

NORM_EPS = 1e-6
LANES = 128
GDN_HEADS = 16
GDN_CONV = 4
GDN_CHUNK = 64
GDN_GROUP = 256
GDN_GROUPS_IN_FLIGHT = 4
GDN_HEADS_PER_STEP = 4
NSA_HEADS = 16
NSA_GROUPS = 4
NSA_HPG = NSA_HEADS // NSA_GROUPS
NSA_N_BRANCH = 3
CMP_BLOCK = 32
CMP_STRIDE = 16
SEL_BLOCK = 64
SEL_TOPK = 16
WINDOW = 512
BIG = 1e9
NEG = -1e30
ROPE_THETA = 500000.0
ROPE_DIM = 32
TOP_K = 2

VMEM_LIMIT_BYTES = 56 * 1024 * 1024


def _params(n_axes):
    return pltpu.CompilerParams(dimension_semantics=("arbitrary",) * n_axes,
                                vmem_limit_bytes=VMEM_LIMIT_BYTES)


def _rms(x, gain):
    ms = jnp.mean(x * x, axis=-1, keepdims=True)
    return x * lax.rsqrt(ms + NORM_EPS) * gain


def _sigmoid(x):
    return 1.0 / (1.0 + jnp.exp(-x))


def _dot(a, b):
    return jnp.dot(a, b, preferred_element_type=F32)


def _dot_nt(a, b):
    return lax.dot_general(a, b, (((1,), (1,)), ((), ())), preferred_element_type=F32)


def _dot_tn(a, b):
    return lax.dot_general(a, b, (((0,), (0,)), ((), ())), preferred_element_type=F32)


def _pick_lane(x, idx):
    lane = lax.broadcasted_iota(jnp.int32, x.shape, 1)
    return jnp.sum(jnp.where(lane == idx, x, 0.0), axis=-1, keepdims=True)


def _rms_matmul_kernel(h_ref, g_ref, w_ref, we_ref, o_ref, oe_ref, xn_ref, *, row_chunk):
    j = pl.program_id(1)

    @pl.when(j == 0)
    def _():
        for r in range(0, h_ref.shape[0], row_chunk):
            xn_ref[r:r + row_chunk, :] = _rms(h_ref[r:r + row_chunk, :], g_ref[...]).astype(BF16)
        oe_ref[...] = _dot(xn_ref[...], we_ref[...])

    o_ref[...] = _dot(xn_ref[...], w_ref[...]).astype(o_ref.dtype)


def rms_matmul(h, gain, w, w_extra, *, tm=1024, tn=1024, out_dtype=F32):
    n, d = h.shape
    nm = w.shape[1]
    tm = min(tm, n)
    tn = min(tn, nm)
    return pl.pallas_call(
        functools.partial(_rms_matmul_kernel, row_chunk=256),
        grid=(n // tm, nm // tn),
        in_specs=[
            pl.BlockSpec((tm, d), lambda i, j: (i, 0)),
            pl.BlockSpec((1, d), lambda i, j: (0, 0)),
            pl.BlockSpec((d, tn), lambda i, j: (0, j)),
            pl.BlockSpec((d, LANES), lambda i, j: (0, 0)),
        ],
        out_specs=[
            pl.BlockSpec((tm, tn), lambda i, j: (i, j)),
            pl.BlockSpec((tm, LANES), lambda i, j: (i, 0)),
        ],
        out_shape=[jax.ShapeDtypeStruct((n, nm), out_dtype),
                   jax.ShapeDtypeStruct((n, LANES), F32)],
        scratch_shapes=[pltpu.VMEM((tm, d), BF16)],
        compiler_params=_params(2),
    )(h, gain.reshape(1, d), w, w_extra)


def _matmul_res_kernel(x_ref, w_ref, r_ref, o_ref):
    o_ref[...] = r_ref[...] + _dot(x_ref[...], w_ref[...])


def matmul_residual(x, w, res, *, tm=512, tn=2048):
    n, k = x.shape
    nm = w.shape[1]
    tm = min(tm, n)
    return pl.pallas_call(
        _matmul_res_kernel,
        grid=(n // tm, nm // tn),
        in_specs=[
            pl.BlockSpec((tm, k), lambda i, j: (i, 0)),
            pl.BlockSpec((k, tn), lambda i, j: (0, j)),
            pl.BlockSpec((tm, tn), lambda i, j: (i, j)),
        ],
        out_specs=pl.BlockSpec((tm, tn), lambda i, j: (i, j)),
        out_shape=jax.ShapeDtypeStruct((n, nm), F32),
        compiler_params=_params(2),
    )(x, w, res)


def _router_kernel(h_ref, g_ref, r_ref, o_ref, *, n_experts, row_chunk):
    for r in range(0, h_ref.shape[0], row_chunk):
        xn = _rms(h_ref[r:r + row_chunk, :], g_ref[...])
        lane = lax.broadcasted_iota(jnp.int32, (row_chunk, LANES), 1)
        logits = jnp.zeros((row_chunk, LANES), F32)
        for ex in range(n_experts):
            col = jnp.sum(xn * r_ref[ex:ex + 1, :], axis=-1, keepdims=True)
            logits = jnp.where(lane == ex, col, logits)
        valid = lane < n_experts
        z = jnp.where(valid, logits, NEG)
        z = z - jnp.max(z, axis=-1, keepdims=True)
        p = jnp.where(valid, jnp.exp(z), 0.0)
        p = p / jnp.sum(p, axis=-1, keepdims=True)
        p = jnp.where(valid, p, -1.0)
        p1 = jnp.max(p, axis=-1, keepdims=True)
        i1 = jnp.min(jnp.where(p == p1, lane, LANES), axis=-1, keepdims=True)
        pm = jnp.where(lane == i1, -1.0, p)
        p2 = jnp.max(pm, axis=-1, keepdims=True)
        i2 = jnp.min(jnp.where(pm == p2, lane, LANES), axis=-1, keepdims=True)
        den = p1 + p2
        o_ref[r:r + row_chunk, :] = (jnp.where(lane == i1, p1 / den, 0.0)
                                     + jnp.where(lane == i2, p2 / den, 0.0))


def route_tokens(h, gain, router, *, tm=1024):
    n, d = h.shape
    n_experts = router.shape[1]
    tm = min(tm, n)
    r_rows = router.T
    return pl.pallas_call(
        functools.partial(_router_kernel, n_experts=n_experts, row_chunk=256),
        grid=(n // tm,),
        in_specs=[
            pl.BlockSpec((tm, d), lambda i: (i, 0)),
            pl.BlockSpec((1, d), lambda i: (0, 0)),
            pl.BlockSpec((n_experts, d), lambda i: (0, 0)),
        ],
        out_specs=pl.BlockSpec((tm, LANES), lambda i: (i, 0)),
        out_shape=jax.ShapeDtypeStruct((n, LANES), F32),
        compiler_params=_params(1),
    )(h, gain.reshape(1, d), r_rows)


def _ffn_kernel(h_ref, g_ref, c_ref, wg_ref, wu_ref, wd_ref, fg_ref, o_ref, xn_ref, *,
                n_e, n_f, use_comb, final_norm, row_chunk):
    e = pl.program_id(1)
    f = pl.program_id(2)

    @pl.when((e == 0) & (f == 0))
    def _():
        for r in range(0, h_ref.shape[0], row_chunk):
            x = h_ref[r:r + row_chunk, :]
            xn_ref[r:r + row_chunk, :] = _rms(x, g_ref[...]).astype(BF16)
            o_ref[r:r + row_chunk, :] = x

    xn = xn_ref[...]
    gate = _dot(xn, wg_ref[0])
    up = _dot(xn, wu_ref[0])
    act = gate * _sigmoid(gate) * up
    if use_comb:
        act = act * _pick_lane(c_ref[...], e)
    o_ref[...] += _dot(act.astype(BF16), wd_ref[0])

    if final_norm:
        @pl.when((e == n_e - 1) & (f == n_f - 1))
        def _():
            for r in range(0, h_ref.shape[0], row_chunk):
                o_ref[r:r + row_chunk, :] = _rms(o_ref[r:r + row_chunk, :], fg_ref[...])


def ffn_residual(h, gain, comb, w_gate, w_up, w_down, final_gain, *, tm=512, tf=512):
    n, d = h.shape
    n_e, _, ff = w_gate.shape
    w_gate, w_up, w_down = w_gate.astype(BF16), w_up.astype(BF16), w_down.astype(BF16)
    tm = min(tm, n)
    use_comb = comb is not None
    final_norm = final_gain is not None
    if comb is None:
        comb = jnp.ones((n, LANES), F32)
    if final_gain is None:
        final_gain = jnp.ones((d,), F32)
    n_f = ff // tf
    return pl.pallas_call(
        functools.partial(_ffn_kernel, n_e=n_e, n_f=n_f, use_comb=use_comb,
                          final_norm=final_norm, row_chunk=256),
        grid=(n // tm, n_e, n_f),
        in_specs=[
            pl.BlockSpec((tm, d), lambda i, e, f: (i, 0)),
            pl.BlockSpec((1, d), lambda i, e, f: (0, 0)),
            pl.BlockSpec((tm, LANES), lambda i, e, f: (i, 0)),
            pl.BlockSpec((1, d, tf), lambda i, e, f: (e, 0, f)),
            pl.BlockSpec((1, d, tf), lambda i, e, f: (e, 0, f)),
            pl.BlockSpec((1, tf, d), lambda i, e, f: (e, f, 0)),
            pl.BlockSpec((1, d), lambda i, e, f: (0, 0)),
        ],
        out_specs=pl.BlockSpec((tm, d), lambda i, e, f: (i, 0)),
        out_shape=jax.ShapeDtypeStruct((n, d), F32),
        scratch_shapes=[pltpu.VMEM((tm, d), BF16)],
        compiler_params=_params(3),
    )(h, gain.reshape(1, d), comb, w_gate, w_up, w_down, final_gain.reshape(1, d))


def _moe_kernel(h_ref, g_ref, c_ref, ct_ref, wg_ref, wu_ref, wd_ref, fg_ref, o_ref,
                xn_ref, rc_ref, rr_ref, xs_ref, ye_ref, *, n_e, n_f, cap, final_norm, row_chunk):
    e = pl.program_id(1)
    f = pl.program_id(2)
    tm, d = h_ref.shape
    col_chunk = 512

    @pl.when((e == 0) & (f == 0))
    def _():
        for r in range(0, tm, row_chunk):
            x = h_ref[r:r + row_chunk, :]
            xn_ref[r:r + row_chunk, :] = _rms(x, g_ref[...]).astype(BF16)
            o_ref[r:r + row_chunk, :] = x
        ti = lax.broadcasted_iota(jnp.int32, (tm, tm), 0)
        tj = lax.broadcasted_iota(jnp.int32, (tm, tm), 1)
        routed_c = jnp.where(c_ref[...] > 0.0, 1.0, 0.0).astype(BF16)
        routed_r = jnp.where(ct_ref[...] > 0.0, 1.0, 0.0).astype(BF16)
        rc_ref[...] = _dot(jnp.where(tj < ti, 1.0, 0.0).astype(BF16), routed_c)
        rr_ref[...] = _dot(routed_r, jnp.where(ti < tj, 1.0, 0.0).astype(BF16))

    key_row = jnp.where(ct_ref[pl.ds(e, 1), :] > 0.0, rr_ref[pl.ds(e, 1), :], -1.0)
    w_row = ct_ref[pl.ds(e, 1), :]
    n_routed = jnp.sum(jnp.where(key_row >= 0.0, 1.0, 0.0)).astype(jnp.int32)
    n_rounds = (n_routed + cap - 1) // cap

    def gather_mask(rnd):
        slot = lax.broadcasted_iota(jnp.int32, (cap, tm), 0) + rnd * cap
        return key_row == slot.astype(F32)

    def expert(xs):
        gate = _dot(xs, wg_ref[0])
        up = _dot(xs, wu_ref[0])
        return _dot((gate * _sigmoid(gate) * up).astype(BF16), wd_ref[0])

    def scatter_add(rnd, y):
        picked = gather_mask(rnd)
        w_slot = jnp.sum(jnp.where(picked, w_row, 0.0), axis=-1, keepdims=True)
        yw = (y * w_slot).astype(BF16)
        key_col = jnp.where(_pick_lane(c_ref[...], e) > 0.0, _pick_lane(rc_ref[...], e), -1.0)
        slot = lax.broadcasted_iota(jnp.int32, (tm, cap), 1) + rnd * cap
        back = jnp.where(key_col == slot.astype(F32), 1.0, 0.0).astype(BF16)
        for c0 in range(0, d, col_chunk):
            o_ref[:, c0:c0 + col_chunk] += _dot(back, yw[:, c0:c0 + col_chunk])

    @pl.when(f == 0)
    def _():
        xs_ref[...] = _dot(jnp.where(gather_mask(0), 1.0, 0.0).astype(BF16), xn_ref[...]).astype(BF16)

    part = expert(xs_ref[...])

    @pl.when(f == 0)
    def _():
        ye_ref[...] = part

    @pl.when(f > 0)
    def _():
        ye_ref[...] += part

    @pl.when(f == n_f - 1)
    def _():
        scatter_add(0, ye_ref[...])

    def extra_round(rnd, carry):
        xs = _dot(jnp.where(gather_mask(rnd), 1.0, 0.0).astype(BF16), xn_ref[...]).astype(BF16)
        scatter_add(rnd, expert(xs))
        return carry

    lax.fori_loop(1, n_rounds, extra_round, 0)

    if final_norm:
        @pl.when((e == n_e - 1) & (f == n_f - 1))
        def _():
            for r in range(0, tm, row_chunk):
                o_ref[r:r + row_chunk, :] = _rms(o_ref[r:r + row_chunk, :], fg_ref[...])


def moe_residual(h, gain, comb, w_gate, w_up, w_down, final_gain, *, tm=1024, tf=1024, cap=320):
    n, d = h.shape
    n_e, _, ff = w_gate.shape
    w_gate, w_up, w_down = w_gate.astype(BF16), w_up.astype(BF16), w_down.astype(BF16)
    tm = min(tm, n)
    final_norm = final_gain is not None
    if final_gain is None:
        final_gain = jnp.ones((d,), F32)
    n_f = ff // tf
    once = pl.Buffered(1)
    return pl.pallas_call(
        functools.partial(_moe_kernel, n_e=n_e, n_f=n_f, cap=cap, final_norm=final_norm,
                          row_chunk=256),
        grid=(n // tm, n_e, n_f),
        in_specs=[
            pl.BlockSpec((tm, d), lambda i, e, f: (i, 0), pipeline_mode=once),
            pl.BlockSpec((1, d), lambda i, e, f: (0, 0)),
            pl.BlockSpec((tm, LANES), lambda i, e, f: (i, 0), pipeline_mode=once),
            pl.BlockSpec((LANES, tm), lambda i, e, f: (0, i), pipeline_mode=once),
            pl.BlockSpec((1, d, tf), lambda i, e, f: (e, 0, f)),
            pl.BlockSpec((1, d, tf), lambda i, e, f: (e, 0, f)),
            pl.BlockSpec((1, tf, d), lambda i, e, f: (e, f, 0)),
            pl.BlockSpec((1, d), lambda i, e, f: (0, 0)),
        ],
        out_specs=pl.BlockSpec((tm, d), lambda i, e, f: (i, 0), pipeline_mode=once),
        out_shape=jax.ShapeDtypeStruct((n, d), F32),
        scratch_shapes=[pltpu.VMEM((tm, d), BF16), pltpu.VMEM((tm, LANES), F32),
                        pltpu.VMEM((LANES, tm), F32), pltpu.VMEM((cap, d), BF16),
                        pltpu.VMEM((cap, d), F32)],
        compiler_params=_params(3),
    )(h, gain.reshape(1, d), comb, comb.T, w_gate, w_up, w_down, final_gain.reshape(1, d))


def _tri_inv_many(lows, nilpotency):
    c = lows[0].shape[0]
    ii = lax.broadcasted_iota(jnp.int32, (c, c), 0)
    jj = lax.broadcasted_iota(jnp.int32, (c, c), 1)
    eye = jnp.where(ii == jj, 1.0, 0.0)
    lb = [low.astype(BF16) for low in lows]
    ps = [_dot(b, b).astype(BF16) for b in lb]
    ts = [eye - low for low in lows]
    levels = int(math.log2(nilpotency)) - 1
    for level in range(levels):
        tb = [t.astype(BF16) for t in ts]
        if level + 1 < levels:
            nxt = [_dot(p, p).astype(BF16) for p in ps]
        ts = [t + _dot(b, p) for t, b, p in zip(ts, tb, ps)]
        if level + 1 < levels:
            ps = nxt
    return ts


def _gdn_local_kernel(alog_ref, dtb_ref, q_ref, k_ref, v_ref, cq_ref, ck_ref, cv_ref,
                      a1_ref, a2_ref, b2_ref,
                      u_ref, w_ref, qd_ref, kd_ref, aqk_ref, gl_ref,
                      xp_ref, qs_ref, ks_ref, vs_ref, gc1_ref, gc2_ref, gl2_ref, be2_ref, *,
                      seq, row_chunk):
    hd = pl.program_id(1)
    dk = q_ref.shape[1]
    c = GDN_CHUNK
    grp = GDN_GROUP
    n_groups = seq // grp
    pad = 8

    def conv_silu(x_ref, w_ref, dst_ref, normalise, scale):
        xp_ref[0:pad, :] = jnp.zeros((pad, dk), F32)
        xp_ref[pad:pad + seq, :] = x_ref[...].astype(F32)
        w = w_ref[...]
        for r in range(0, seq, row_chunk):
            y = xp_ref[pad - 3 + r:pad - 3 + r + row_chunk, :] * w[0:1, :]
            for j in range(1, GDN_CONV):
                y = y + xp_ref[pad - 3 + j + r:pad - 3 + j + r + row_chunk, :] * w[j:j + 1, :]
            y = y * _sigmoid(y)
            if normalise:
                y = y * lax.rsqrt(jnp.sum(y * y, axis=-1, keepdims=True) + NORM_EPS)
            if scale != 1.0:
                y = y * scale
            dst_ref[r:r + row_chunk, :] = y

    conv_silu(q_ref, cq_ref, qs_ref, True, dk ** -0.5)
    conv_silu(k_ref, ck_ref, ks_ref, True, 1.0)
    conv_silu(v_ref, cv_ref, vs_ref, False, 1.0)

    a_coef = -jnp.exp(alog_ref[hd])
    dtb = dtb_ref[hd]

    def log_decay(a):
        x = a + dtb
        return a_coef * (jnp.maximum(x, 0.0) + jnp.log(1.0 + jnp.exp(-jnp.abs(x))))

    ii = lax.broadcasted_iota(jnp.int32, (grp, grp), 0)
    jj = lax.broadcasted_iota(jnp.int32, (grp, grp), 1)
    same = (ii // c) == (jj // c)
    hi = lax.Precision.HIGHEST
    g1 = log_decay(a1_ref[0, 0])
    g2 = log_decay(a2_ref[0, 0])
    ones_b = jnp.where(same, 1.0, 0.0)
    gc1_ref[...] = jnp.dot(g1, jnp.where(same & (ii <= jj), 1.0, 0.0),
                           preferred_element_type=F32, precision=hi)
    gc2_ref[...] = jnp.dot(jnp.where(same & (jj <= ii), 1.0, 0.0), g2,
                           preferred_element_type=F32, precision=hi)
    gl2_ref[...] = jnp.dot(ones_b, g2, preferred_element_type=F32, precision=hi)
    gl_ref[0, 0] = jnp.exp(jnp.dot(g1, ones_b, preferred_element_type=F32, precision=hi))
    be2_ref[...] = _sigmoid(b2_ref[0, 0])

    causal = same & (ii >= jj)
    strict = same & (ii > jj)
    n_par = GDN_GROUPS_IN_FLIGHT

    def groups(it, carry):
        gis = [it * n_par + t for t in range(n_par)]
        rows = [pl.ds(pl.multiple_of(gi * grp, grp), grp) for gi in gis]
        q = [qs_ref[r, :] for r in rows]
        k = [ks_ref[r, :] for r in rows]
        v = [vs_ref[r, :] for r in rows]
        g_row = [gc1_ref[pl.ds(gi, 1), :] for gi in gis]
        g_col = [_pick_lane(gc2_ref[...], gi) for gi in gis]
        gl_col = [_pick_lane(gl2_ref[...], gi) for gi in gis]
        b_col = [_pick_lane(be2_ref[...], gi) for gi in gis]
        kbf = [x.astype(BF16) for x in k]
        kb = [x * b for x, b in zip(k, b_col)]
        kk = [_dot_nt(x.astype(BF16), y) for x, y in zip(kb, kbf)]
        qk = [_dot_nt(x.astype(BF16), y) for x, y in zip(q, kbf)]
        decay = [jnp.where(causal, jnp.exp(jnp.where(causal, gc - gr, 0.0)), 0.0)
                 for gc, gr in zip(g_col, g_row)]
        t_inv = _tri_inv_many([jnp.where(strict, x * d, 0.0) for x, d in zip(kk, decay)], c)
        e_col = [jnp.exp(gc) for gc in g_col]
        rhs = [jnp.concatenate([x * b, y * e], axis=1).astype(BF16)
               for x, b, y, e in zip(v, b_col, kb, e_col)]
        uw = [_dot(t.astype(BF16), r) for t, r in zip(t_inv, rhs)]
        for t in range(n_par):
            u_ref[rows[t], :] = uw[t][:, :dk].astype(u_ref.dtype)
            w_ref[rows[t], :] = uw[t][:, dk:].astype(w_ref.dtype)
            a_qk = qk[t] * decay[t]
            a_c = a_qk[:, 0:c]
            for blk in range(1, grp // c):
                a_c = a_c + a_qk[:, blk * c:(blk + 1) * c]
            aqk_ref[0, 0, rows[t], :] = a_c.astype(aqk_ref.dtype)
            qd_ref[rows[t], :] = (q[t] * e_col[t]).astype(qd_ref.dtype)
            kd_ref[rows[t], :] = (k[t] * jnp.exp(gl_col[t] - g_col[t])).astype(kd_ref.dtype)
        return carry

    lax.fori_loop(0, n_groups // n_par, groups, 0)


def _gdn_scan_kernel(gl_ref, u_ref, w_ref, qd_ref, kd_ref, aqk_ref, z_ref, on_ref, o_ref, *,
                     seq, heads_per_step):
    b = pl.program_id(0)
    hb = pl.program_id(1)
    c = GDN_CHUNK
    dk = on_ref.shape[1]

    def step(ci, states):
        rows = pl.ds(pl.multiple_of(ci * c, c), c)
        hs = range(heads_per_step)
        cols = [slice(hh * dk, (hh + 1) * dk) for hh in hs]
        ws_qs = [_dot(jnp.concatenate([w_ref[rows, cols[hh]], qd_ref[rows, cols[hh]]], axis=0),
                      states[hh].astype(BF16)) for hh in hs]
        v_new = [(u_ref[rows, cols[hh]].astype(F32) - ws_qs[hh][:c]).astype(BF16) for hh in hs]
        intra = [_dot(aqk_ref[0, hh, rows, :], v_new[hh]) for hh in hs]
        outer = [_dot_tn(kd_ref[rows, cols[hh]], v_new[hh]) for hh in hs]
        new_states = []
        for hh in hs:
            decay = gl_ref[b, hb * heads_per_step + hh, ci]
            new_states.append(states[hh] * decay + outer[hh])
            z = z_ref[rows, cols[hh]].astype(F32)
            o = ws_qs[hh][c:] + intra[hh]
            o_ref[rows, cols[hh]] = (_rms(o, on_ref[...]) * (z * _sigmoid(z))).astype(o_ref.dtype)
        return tuple(new_states)

    lax.fori_loop(0, seq // c, step,
                  tuple(jnp.zeros((dk, dk), F32) for _ in range(heads_per_step)))


def gdn_core(proj, gates, conv_w, a_log, dt_bias, out_norm, *, batch, seq):
    n = proj.shape[0]
    nh = GDN_HEADS
    dk = proj.shape[1] // (4 * nh)
    c = GDN_CHUNK
    grp = GDN_GROUP
    n_groups = seq // grp
    hps = GDN_HEADS_PER_STEP
    bl = gates[:, :nh].reshape(batch, seq, nh).transpose(0, 2, 1)
    al = gates[:, nh:2 * nh].reshape(batch, seq, nh).transpose(0, 2, 1)
    a1 = al.reshape(batch, nh, n_groups, grp)
    a2 = a1.transpose(0, 1, 3, 2)
    b2 = bl.reshape(batch, nh, n_groups, grp).transpose(0, 1, 3, 2)
    smem = pl.BlockSpec(memory_space=pltpu.SMEM)
    col = lambda off: pl.BlockSpec((seq, dk), lambda b, h, off=off: (b, off * nh + h))
    cw = lambda off: pl.BlockSpec((GDN_CONV, dk), lambda b, h, off=off: (0, off * nh + h))
    g1 = pl.BlockSpec((1, 1, n_groups, grp), lambda b, h: (b, h, 0, 0))
    g2 = pl.BlockSpec((1, 1, grp, n_groups), lambda b, h: (b, h, 0, 0))
    head_out = pl.BlockSpec((seq, dk), lambda b, h: (b, h))
    wide = jax.ShapeDtypeStruct((n, nh * dk), BF16)
    u, w, qd, kd, aqk, gl = pl.pallas_call(
        functools.partial(_gdn_local_kernel, seq=seq, row_chunk=256),
        grid=(batch, nh),
        in_specs=[smem, smem, col(0), col(1), col(2), cw(0), cw(1), cw(2), g1, g2, g2],
        out_specs=[head_out, head_out, head_out, head_out,
                   pl.BlockSpec((1, 1, seq, c), lambda b, h: (b, h, 0, 0)), g1],
        out_shape=[wide, wide, wide, wide,
                   jax.ShapeDtypeStruct((batch, nh, seq, c), BF16),
                   jax.ShapeDtypeStruct((batch, nh, n_groups, grp), F32)],
        scratch_shapes=[
            pltpu.VMEM((seq + 8, dk), F32),
            pltpu.VMEM((seq, dk), F32), pltpu.VMEM((seq, dk), F32), pltpu.VMEM((seq, dk), F32),
            pltpu.VMEM((n_groups, grp), F32), pltpu.VMEM((grp, n_groups), F32),
            pltpu.VMEM((grp, n_groups), F32), pltpu.VMEM((grp, n_groups), F32),
        ],
        compiler_params=_params(2),
    )(a_log, dt_bias, proj, proj, proj, conv_w, conv_w, conv_w, a1, a2, b2)
    chunk_decay = gl[..., c - 1::c].reshape(batch, nh, seq // c)
    heads = pl.BlockSpec((seq, hps * dk), lambda b, h: (b, h))
    return pl.pallas_call(
        functools.partial(_gdn_scan_kernel, seq=seq, heads_per_step=hps),
        grid=(batch, nh // hps),
        in_specs=[smem, heads, heads, heads, heads,
                  pl.BlockSpec((1, hps, seq, c), lambda b, h: (b, h, 0, 0)),
                  pl.BlockSpec((seq, hps * dk), lambda b, h: (b, 3 * (nh // hps) + h)),
                  pl.BlockSpec((1, dk), lambda b, h: (0, 0))],
        out_specs=heads,
        out_shape=wide,
        compiler_params=_params(2),
    )(chunk_decay, u, w, qd, kd, aqk, proj, out_norm.reshape(1, dk))


def _rope(x, cos, sin_a, sin_b):
    half = ROPE_DIM // 2
    return (x * cos + pltpu.roll(x, LANES - half, 1) * sin_a + pltpu.roll(x, half, 1) * sin_b)


def _compress_kernel(kc_ref, vc_ref, pk_ref, pv_ref, w1k_ref, w1v_ref, w2k_ref, w2v_ref,
                     ok_ref, ov_ref, tp_ref, *, seq):
    dh = kc_ref.shape[1]
    n_rows = ok_ref.shape[2]
    hidden = w1k_ref.shape[1]

    def run(t_ref, pos_ref, w1_ref, w2_ref, out_ref):
        tp_ref[0:seq, :] = t_ref[...].astype(F32)
        tp_ref[seq:seq + CMP_STRIDE, :] = jnp.zeros((CMP_STRIDE, dh), F32)
        acc = jnp.zeros((n_rows, hidden), F32)
        for l in range(CMP_BLOCK):
            x = tp_ref[pl.ds(l, n_rows, stride=CMP_STRIDE), :] + pos_ref[l:l + 1, :]
            acc = acc + _dot(x.astype(BF16), w1_ref[l * dh:(l + 1) * dh, :])
        hid = acc * _sigmoid(acc)
        out_ref[0, 0] = _dot(hid.astype(BF16), w2_ref[...]).astype(out_ref.dtype)

    run(kc_ref, pk_ref, w1k_ref, w2k_ref, ok_ref)
    run(vc_ref, pv_ref, w1v_ref, w2v_ref, ov_ref)


def nsa_compress(kv, pos_k, pos_v, w1k, w1v, w2k, w2v, *, batch, seq):
    dh = pos_k.shape[1]
    n_rows = seq // CMP_STRIDE
    hidden = w1k.shape[1]
    full = lambda shape: pl.BlockSpec(shape, lambda b, g: (0,) * len(shape))
    out = pl.BlockSpec((1, 1, n_rows, dh), lambda b, g: (b, g, 0, 0))
    return pl.pallas_call(
        functools.partial(_compress_kernel, seq=seq),
        grid=(batch, NSA_GROUPS),
        in_specs=[
            pl.BlockSpec((seq, dh), lambda b, g: (b, g)),
            pl.BlockSpec((seq, dh), lambda b, g: (b, NSA_GROUPS + g)),
            full((CMP_BLOCK, dh)), full((CMP_BLOCK, dh)),
            full((CMP_BLOCK * dh, hidden)), full((CMP_BLOCK * dh, hidden)),
            full((hidden, dh)), full((hidden, dh)),
        ],
        out_specs=[out, out],
        out_shape=[jax.ShapeDtypeStruct((batch, NSA_GROUPS, n_rows, dh), BF16)] * 2,
        scratch_shapes=[pltpu.VMEM((seq + CMP_STRIDE, dh), F32)],
        compiler_params=_params(2),
    )(kv, kv, pos_k, pos_v, w1k, w1v, w2k, w2v)


def _masked_softmax(s, mask):
    sm = jnp.where(mask, s, NEG)
    m = jnp.max(sm, axis=-1, keepdims=True)
    p = jnp.where(mask, jnp.exp(sm - m), 0.0)
    return p / jnp.maximum(jnp.sum(p, axis=-1, keepdims=True), 1e-30)


def _nsa_kernel(q_ref, gate_ref, cos_ref, sa_ref, sb_ref, kcos_ref, ksa_ref, ksb_ref,
                kc_ref, vc_ref, ks_raw_ref, vs_ref, kw_raw_ref, vw_ref, ovt_ref, eb_ref, wb_ref,
                o_ref, ks_ref, kw_ref, *, seq, tq, kc_len, row_chunk):
    g = pl.program_id(1)
    n = pl.program_id(2)
    t0 = n * tq
    dh = LANES
    hpg = NSA_HPG
    m_rows = hpg * tq
    n_sel = seq // SEL_BLOCK
    topk = min(SEL_TOPK, n_sel)
    scale = dh ** -0.5

    stack = lambda x: jnp.concatenate([x] * hpg, axis=0)
    @pl.when(n == 0)
    def _():
        for r in range(0, seq, row_chunk):
            rs = slice(r, r + row_chunk)
            tabs = (kcos_ref[rs, :], ksa_ref[rs, :], ksb_ref[rs, :])
            ks_ref[rs, :] = _rope(ks_raw_ref[rs, :].astype(F32), *tabs).astype(BF16)
            kw_ref[rs, :] = _rope(kw_raw_ref[rs, :].astype(F32), *tabs).astype(BF16)

    q = q_ref[...].astype(F32)
    qs = jnp.concatenate([q[:, h * dh:(h + 1) * dh] for h in range(hpg)], axis=0) * scale
    qr = _rope(qs, stack(cos_ref[...]), stack(sa_ref[...]), stack(sb_ref[...]))
    qb = qs.astype(BF16)
    qrb = qr.astype(BF16)
    row = lax.broadcasted_iota(jnp.int32, (m_rows, 1), 0)
    tq_col = t0 + (row & (tq - 1))

    wk = WINDOW + tq
    start = pl.multiple_of(jnp.maximum(t0 - WINDOW, 0), LANES)
    case = jnp.minimum(n, WINDOW // tq)
    row_id = lax.broadcasted_iota(jnp.int32, (m_rows, LANES), 0) & (tq - 1)
    lane_id = lax.broadcasted_iota(jnp.int32, (m_rows, LANES), 1)
    lhs_w = jnp.concatenate([qrb, jnp.where(row_id == lane_id, 1.0, 0.0).astype(BF16)], axis=1)
    keys_w = jnp.concatenate([kw_ref[pl.ds(start, wk), :], wb_ref[case]], axis=1)
    s_c = _dot_nt(qb, kc_ref[0, 0])
    s_w = _dot_nt(lhs_w, keys_w)
    lane_c = lax.broadcasted_iota(jnp.int32, s_c.shape, 1)
    p_c = _masked_softmax(s_c, lane_c * CMP_STRIDE + (CMP_BLOCK - 1) <= tq_col)
    p_w = jnp.exp(s_w - jnp.max(s_w, axis=-1, keepdims=True))
    l_w = jnp.maximum(jnp.sum(p_w, axis=-1, keepdims=True), 1e-30)
    o_c = _dot(p_c.astype(BF16), vc_ref[0, 0])
    o_w = _dot(p_w.astype(BF16), vw_ref[pl.ds(start, wk), :]) / l_w

    p_sum = p_c[0:tq]
    for h in range(1, hpg):
        p_sum = p_sum + p_c[h * tq:(h + 1) * tq]
    imp_t = lax.dot_general(ovt_ref[...], p_sum, (((1,), (1,)), ((), ())),
                            preferred_element_type=F32, precision=lax.Precision.HIGHEST)
    nsp = -(-n_sel // 8) * 8
    imp_t = imp_t[0:nsp]
    blk = lax.broadcasted_iota(jnp.int32, (nsp, tq), 0)
    t1 = t0 + lax.broadcasted_iota(jnp.int32, (nsp, tq), 1)
    cur = t1 // SEL_BLOCK
    forced = (blk == 0) | (blk == cur) | (blk == cur - 1)
    visible = blk * SEL_BLOCK <= t1
    score = jnp.where(forced, BIG, jnp.where(visible, imp_t, -BIG))
    score = jnp.where(blk < n_sel, score, -3e38)
    rank = jnp.zeros((nsp, tq), F32)
    for j2 in range(n_sel):
        other = score[j2:j2 + 1, :]
        rank = rank + jnp.where(other > score, 1.0,
                                jnp.where((other == score) & (blk > j2), 1.0, 0.0))
    chosen = (rank < topk) & visible & (blk < n_sel)
    bias_t = jnp.where(chosen, 0.0, NEG)
    if nsp < LANES:
        bias_t = jnp.concatenate([bias_t, jnp.full((LANES - nsp, tq), NEG, F32)], axis=0)
    sel_bias = stack(bias_t.T).astype(BF16)

    lhs_sel = jnp.concatenate([qrb, sel_bias], axis=1)

    def sel_step(ci, carry, diagonal):
        m_i, l_i, acc = carry
        k0 = pl.multiple_of(ci * kc_len, kc_len)
        keys = jnp.concatenate([ks_ref[pl.ds(k0, kc_len), :], eb_ref[pl.ds(k0, kc_len), :]], axis=1)
        s = _dot_nt(lhs_sel, keys)
        if diagonal:
            kpos = k0 + lax.broadcasted_iota(jnp.int32, (m_rows, kc_len), 1)
            s = jnp.where(kpos <= tq_col, s, NEG)
        m_new = jnp.maximum(m_i, jnp.max(s, axis=-1, keepdims=True))
        alpha = jnp.exp(m_i - m_new)
        p = jnp.exp(s - m_new)
        l_new = alpha * l_i + jnp.sum(p, axis=-1, keepdims=True)
        acc_new = alpha * acc + _dot(p.astype(BF16), vs_ref[pl.ds(k0, kc_len), :])
        return m_new, l_new, acc_new

    n_kc = (t0 + tq + kc_len - 1) // kc_len
    init = (jnp.full((m_rows, 1), NEG, F32), jnp.zeros((m_rows, 1), F32),
            jnp.zeros((m_rows, dh), F32))
    carry = lax.fori_loop(0, n_kc - 1, functools.partial(sel_step, diagonal=False), init)
    _, l_s, acc_s = sel_step(n_kc - 1, carry, True)
    o_s = acc_s / jnp.maximum(l_s, 1e-30)

    gates = _sigmoid(gate_ref[...])
    for h in range(hpg):
        base = (g * hpg + h) * NSA_N_BRANCH
        rows = slice(h * tq, (h + 1) * tq)
        out = (_pick_lane(gates, base) * o_c[rows] + _pick_lane(gates, base + 1) * o_s[rows]
               + _pick_lane(gates, base + 2) * o_w[rows])
        o_ref[:, h * dh:(h + 1) * dh] = out.astype(o_ref.dtype)


def nsa_attention(q, gates, cos, sin_a, sin_b, k_cmp, v_cmp, kvr, tables, *, batch, seq, kc_len=512):
    overlap_t, block_onehot, window_bias = tables
    n = q.shape[0]
    dh = LANES
    tq = LANES
    gw = NSA_HPG * dh
    nq = seq // tq
    kc_len = min(kc_len, seq)
    rows = lambda width: pl.BlockSpec((tq, width), lambda b, g, i: (b * nq + i, 0))
    cmp_spec = pl.BlockSpec((1, 1, k_cmp.shape[2], dh), lambda b, g, i: (b, g, 0, 0))
    kv_spec = lambda t: pl.BlockSpec((seq, dh), lambda b, g, i, t=t: (b, t * NSA_GROUPS + g))
    const = lambda shape: pl.BlockSpec(shape, lambda b, g, i: (0,) * len(shape))
    key_tab = pl.BlockSpec((seq, LANES), lambda b, g, i: (b, 0))
    return pl.pallas_call(
        functools.partial(_nsa_kernel, seq=seq, tq=tq, kc_len=kc_len, row_chunk=256),
        grid=(batch, NSA_GROUPS, nq),
        in_specs=[
            pl.BlockSpec((tq, gw), lambda b, g, i: (b * nq + i, g)),
            rows(LANES), rows(LANES), rows(LANES), rows(LANES),
            key_tab, key_tab, key_tab,
            cmp_spec, cmp_spec, kv_spec(2), kv_spec(3), kv_spec(4), kv_spec(5),
            const(overlap_t.shape), const(block_onehot.shape), const(window_bias.shape),
        ],
        out_specs=pl.BlockSpec((tq, gw), lambda b, g, i: (b * nq + i, g)),
        out_shape=jax.ShapeDtypeStruct((n, NSA_HEADS * dh), BF16),
        scratch_shapes=[pltpu.VMEM((seq, dh), BF16), pltpu.VMEM((seq, dh), BF16)],
        compiler_params=_params(3),
    )(q, gates, cos, sin_a, sin_b, cos, sin_a, sin_b, k_cmp, v_cmp, kvr, kvr, kvr, kvr,
      overlap_t, block_onehot, window_bias)


def _rope_tables(positions):
    half = ROPE_DIM // 2
    inv = ROPE_THETA ** (-jnp.arange(0, ROPE_DIM, 2, dtype=F32) / ROPE_DIM)
    ang = positions.astype(F32).reshape(-1, 1) * inv
    cos, sin = jnp.cos(ang), jnp.sin(ang)
    n = ang.shape[0]
    ones = jnp.ones((n, LANES - ROPE_DIM), F32)
    zeros_h = jnp.zeros((n, half), F32)
    zeros_t = jnp.zeros((n, LANES - ROPE_DIM), F32)
    cos_t = jnp.concatenate([cos, cos, ones], axis=1)
    sin_a = jnp.concatenate([-sin, zeros_h, zeros_t], axis=1)
    sin_b = jnp.concatenate([zeros_h, sin, zeros_t], axis=1)
    return cos_t, sin_a, sin_b


def _overlap_table(seq):
    n_cmp = (seq - CMP_BLOCK) // CMP_STRIDE + 1
    n_sel = seq // SEL_BLOCK
    c_idx = jnp.arange(n_cmp)
    j_idx = jnp.arange(n_sel)
    ov = jnp.clip(jnp.minimum(c_idx[:, None] * CMP_STRIDE + CMP_BLOCK, (j_idx[None, :] + 1) * SEL_BLOCK)
                  - jnp.maximum(c_idx[:, None] * CMP_STRIDE, j_idx[None, :] * SEL_BLOCK), 0)
    ov = ov.astype(F32) / CMP_STRIDE
    return jnp.pad(ov, ((0, LANES - n_cmp), (0, LANES - n_sel)))


def _nsa_tables(seq):
    tq = LANES
    key = jnp.arange(seq)
    block_onehot = (key[:, None] // SEL_BLOCK == jnp.arange(LANES)[None, :]).astype(BF16)
    x = jnp.arange(WINDOW + tq)[None, :, None]
    t = (jnp.arange(WINDOW // tq + 1) * tq)[:, None, None] + jnp.arange(tq)[None, None, :]
    window_bias = jnp.where((x <= t) & (x > t - WINDOW), 0.0, NEG).astype(BF16)
    return _overlap_table(seq).T, block_onehot, window_bias


def _split_pad(w, main):
    extra = w[:, main:]
    return w[:, :main].astype(BF16), jnp.pad(extra, ((0, 0), (0, LANES - extra.shape[1]))).astype(BF16)


def kernel(x, positions, attn_norm, ffn_norm, final_norm, gdn_w_in, gdn_conv, gdn_a_log, gdn_dt_bias,
           gdn_out_norm, gdn_w_out, kv_norm, nsa_w_kv, cmp_pos_k, cmp_w1_k, cmp_w2_k, cmp_pos_v, cmp_w1_v,
           cmp_w2_v, nsa_w_q, nsa_w_o, ffn_w_gate, ffn_w_up, ffn_w_down, moe_router, moe_w_gate, moe_w_up,
           moe_w_down):
    batch, seq, d = x.shape
    depth = attn_norm.shape[0]
    n_a = depth // 2
    h = x.reshape(batch * seq, d)
    gdn_width = gdn_w_out.shape[1]
    nsa_width = nsa_w_o.shape[1]
    shared = None
    for l in range(depth):
        if l < n_a:
            w_main, w_gates = _split_pad(gdn_w_in[l], 4 * gdn_width)
            proj, gates = rms_matmul(h, attn_norm[l], w_main, w_gates, out_dtype=BF16)
            o = gdn_core(proj, gates, gdn_conv[l], gdn_a_log[l], gdn_dt_bias[l], gdn_out_norm[l],
                         batch=batch, seq=seq)
            h = matmul_residual(o, gdn_w_out[l].astype(BF16), h)
        else:
            cos, sin_a, sin_b, k_cmp, v_cmp, kvr, tables = shared
            w_main, w_gates = _split_pad(nsa_w_q[l - n_a], nsa_width)
            q, gates = rms_matmul(h, attn_norm[l], w_main, w_gates, tn=nsa_width, out_dtype=BF16)
            o = nsa_attention(q, gates, cos, sin_a, sin_b, k_cmp, v_cmp, kvr, tables,
                              batch=batch, seq=seq)
            h = matmul_residual(o, nsa_w_o[l - n_a].astype(BF16), h)
        final_gain = final_norm if l == depth - 1 else None
        if l % 2 == 0:
            i = l // 2
            h = ffn_residual(h, ffn_norm[l], None, ffn_w_gate[i][None], ffn_w_up[i][None],
                             ffn_w_down[i][None], final_gain)
        else:
            i = l // 2
            comb = route_tokens(h, ffn_norm[l], moe_router[i])
            h = moe_residual(h, ffn_norm[l], comb, moe_w_gate[i], moe_w_up[i], moe_w_down[i], final_gain)
        if l == n_a - 1:
            cos, sin_a, sin_b = _rope_tables(positions)
            kv, _ = rms_matmul(h, kv_norm, nsa_w_kv.astype(BF16), jnp.zeros((d, LANES), BF16),
                               out_dtype=BF16)
            kvr = kv
            k_cmp, v_cmp = nsa_compress(kv, cmp_pos_k, cmp_pos_v, cmp_w1_k.astype(BF16),
                                        cmp_w1_v.astype(BF16), cmp_w2_k.astype(BF16),
                                        cmp_w2_v.astype(BF16), batch=batch, seq=seq)
            shared = (cos, sin_a, sin_b, k_cmp, v_cmp, kvr, _nsa_tables(seq))
    return h.reshape(batch, seq, d)
```
